```python
import jax, jax.numpy as jnp
from jax import lax
import numpy as np

D_MODEL = 1024
BATCH = 4
SEQ = 4096
DEPTH = 2

CHUNK = 64
Q_BLOCK = 128
HEAD_DIM = 64
NORM_EPS = 1e-6

A_HEADS = 4
A_W = A_HEADS * HEAD_DIM
A_LEFT_CHUNKS = 8
A_MAX_REL = 128

B_HEADS = 4
B_Q_LORA = 256
B_KV_LORA = 128
B_NOPE = 64
B_ROPE = 32
B_V = 64
ROPE_THETA = 10000.0

C_HEADS = 4
C_W = C_HEADS * HEAD_DIM
FORGET_BIAS_MEAN = 3.0

CONV_CH = 256
CONV_K = 31

N_BRANCH = 4
BRANCH_W = 256
D_FF = 4 * D_MODEL

IN_SPLITS = (A_W, A_W, A_W,
             B_Q_LORA, B_KV_LORA, B_ROPE,
             C_W, C_W, C_W, C_HEADS,
             CONV_CH, CONV_CH)
IN_COLS = sum(IN_SPLITS)

kernel_name = 'hybrid_gated_streaming_encoder'


def _normal(key, shape, scale):
    return scale * jax.random.normal(key, shape, jnp.float32)


def rms_norm(x, g):
    xf = x.astype(jnp.float32)
    y = xf * lax.rsqrt(jnp.mean(xf * xf, axis=-1, keepdims=True) + NORM_EPS)
    return (y * g.astype(jnp.float32)).astype(x.dtype)


def layer_norm(x, g, b):
    xf = x.astype(jnp.float32)
    mu = jnp.mean(xf, axis=-1, keepdims=True)
    xc = xf - mu
    var = jnp.mean(xc * xc, axis=-1, keepdims=True)
    y = xc * lax.rsqrt(var + NORM_EPS) * g.astype(jnp.float32) + b.astype(jnp.float32)
    return y.astype(x.dtype)


def apply_rope(x, positions):
    half = x.shape[-1] // 2
    inv_freq = 1.0 / (ROPE_THETA ** (jnp.arange(half, dtype=jnp.float32) / half))
    ang = positions.astype(jnp.float32)[..., None] * inv_freq
    if x.ndim == 4:
        ang = ang[:, :, None, :]
    cos, sin = jnp.cos(ang), jnp.sin(ang)
    xf = x.astype(jnp.float32)
    x1, x2 = xf[..., :half], xf[..., half:]
    return jnp.concatenate([x1 * cos - x2 * sin, x2 * cos + x1 * sin], axis=-1).astype(x.dtype)


def chunk_band_attention(q, k, v, rel_table):
    b, s, h, d = q.shape
    nc = s // CHUNK
    w = A_LEFT_CHUNKS + 1
    qc = q.reshape(b, nc, CHUNK, h, d)

    def band(t):
        tc = t.reshape(b, nc, CHUNK, h, t.shape[-1])
        tp = jnp.pad(tc, ((0, 0), (A_LEFT_CHUNKS, 0), (0, 0), (0, 0), (0, 0)))
        return jnp.concatenate([tp[:, j:j + nc] for j in range(w)], axis=2)

    kb, vb = band(k), band(v)
    key_chunk = jnp.arange(nc)[:, None] + jnp.arange(w)[None, :] - A_LEFT_CHUNKS
    valid = jnp.repeat(key_chunk >= 0, CHUNK, axis=1)
    rel = (jnp.arange(CHUNK)[:, None] + A_LEFT_CHUNKS * CHUNK) - jnp.arange(w * CHUNK)[None, :]
    rel_idx = jnp.clip(rel, -A_MAX_REL, A_MAX_REL) + A_MAX_REL
    bias = rel_table[:, rel_idx].astype(jnp.float32)
    scores = jnp.einsum('bcqhd,bckhd->bchqk', qc, kb).astype(jnp.float32) * (d ** -0.5)
    scores = scores + bias[None, None]
    scores = jnp.where(valid[None, :, None, None, :], scores, -jnp.inf)
    p = jax.nn.softmax(scores, axis=-1).astype(v.dtype)
    out = jnp.einsum('bchqk,bckhd->bcqhd', p, vb)
    return out.reshape(b, s, h, -1)


def block_causal_attention(q, k, v, causal_unit, log_forget_cum=None):
    b, s, h, dk = q.shape
    nb = s // Q_BLOCK
    scale = dk ** -0.5
    key_unit = jnp.arange(s) // causal_unit
    qb = q.reshape(b, nb, Q_BLOCK, h, dk).swapaxes(0, 1)
    starts = jnp.arange(nb) * Q_BLOCK

    def attend(q_blk, start, f_blk):
        sc = jnp.einsum('bqhd,bkhd->bhqk', q_blk, k).astype(jnp.float32) * scale
        if f_blk is not None:
            f_k = log_forget_cum.transpose(0, 2, 1)[:, :, None, :]
            sc = sc + f_blk.transpose(0, 2, 1)[..., None] - f_k
        q_unit = (start + jnp.arange(Q_BLOCK)) // causal_unit
        mask = key_unit[None, :] <= q_unit[:, None]
        sc = jnp.where(mask[None, None], sc, -jnp.inf)
        p = jax.nn.softmax(sc, axis=-1).astype(v.dtype)
        return jnp.einsum('bhqk,bkhd->bqhd', p, v)

    if log_forget_cum is None:
        out = lax.map(lambda xs: attend(xs[0], xs[1], None), (qb, starts))
    else:
        fb = log_forget_cum.reshape(b, nb, Q_BLOCK, h).swapaxes(0, 1)
        out = lax.map(lambda xs: attend(xs[0], xs[1], xs[2]), (qb, starts, fb))
    return out.swapaxes(0, 1).reshape(b, s, h, -1)


def hybrid_layer(x, positions, g_mix, w_in, w_gate, b_gate, rel_bias, g_q_lat, w_uq, g_kv_lat, w_ukv,
                 b_forget, w_dw, b_dw, g_conv_ln, b_conv_ln, w_branch, w_o, g_ffn, w_up, w_down):
    b, s, _ = x.shape
    h = rms_norm(x, g_mix)
    split_points = [int(i) for i in np.cumsum(IN_SPLITS)[:-1]]
    (a_q, a_k, a_v, b_qlat, b_kvlat, b_krope,
     c_q, c_k, c_v, c_f, d_val, d_gate) = jnp.split(h @ w_in, split_points, axis=-1)

    def heads(t, n):
        return t.reshape(b, s, n, -1)

    out_a = chunk_band_attention(heads(a_q, A_HEADS), heads(a_k, A_HEADS), heads(a_v, A_HEADS), rel_bias)

    q = (rms_norm(b_qlat, g_q_lat) @ w_uq).reshape(b, s, B_HEADS, B_NOPE + B_ROPE)
    q = jnp.concatenate([q[..., :B_NOPE], apply_rope(q[..., B_NOPE:], positions)], axis=-1)
    kv = (rms_norm(b_kvlat, g_kv_lat) @ w_ukv).reshape(b, s, B_HEADS, B_NOPE + B_V)
    k_rot = apply_rope(b_krope, positions)
    k = jnp.concatenate([kv[..., :B_NOPE],
                         jnp.broadcast_to(k_rot[:, :, None, :], (b, s, B_HEADS, B_ROPE))], axis=-1)
    out_b = block_causal_attention(q, k, kv[..., B_NOPE:], CHUNK)

    log_f = jax.nn.log_sigmoid(c_f.astype(jnp.float32) + b_forget.astype(jnp.float32))
    f_cum = jnp.cumsum(log_f, axis=1)
    out_c = block_causal_attention(heads(c_q, C_HEADS), heads(c_k, C_HEADS), heads(c_v, C_HEADS), 1, f_cum)

    u = d_val * jax.nn.sigmoid(d_gate)
    u = lax.conv_general_dilated(u, w_dw[:, None, :], window_strides=(1,), padding=[(CONV_K - 1, 0)],
                                 dimension_numbers=('NWC', 'WIO', 'NWC'),
                                 feature_group_count=CONV_CH) + b_dw
    out_d = jax.nn.silu(layer_norm(u, g_conv_ln, b_conv_ln))

    branches = jnp.stack([out_a.reshape(b, s, BRANCH_W), out_b.reshape(b, s, BRANCH_W),
                          out_c.reshape(b, s, BRANCH_W), out_d], axis=2)
    proj = jnp.einsum('bsnc,ncd->bsnd', branches, w_branch)
    gates = jax.nn.sigmoid(h @ w_gate + b_gate).reshape(b, s, N_BRANCH, D_MODEL)
    x = x + jnp.sum(gates * proj, axis=2) @ w_o

    hf = rms_norm(x, g_ffn)
    x = x + jnp.square(jax.nn.relu(hf @ w_up)) @ w_down
    return x


def setup_inputs(seed: int = 0) -> dict:
    key = jax.random.key(seed)
    ks = jax.random.split(key, 24)
    L = DEPTH
    x = _normal(ks[0], (BATCH, SEQ, D_MODEL), 1.0)
    offset = jax.random.randint(ks[1], (BATCH, 1), 0, 4 * SEQ, dtype=jnp.int32)
    positions = offset + jnp.arange(SEQ, dtype=jnp.int32)[None, :]
    return dict(
        x=x,
        positions=positions,
        g_mix=1.0 + _normal(ks[2], (L, D_MODEL), 0.05),
        w_in=_normal(ks[3], (L, D_MODEL, IN_COLS), D_MODEL ** -0.5),
        w_gate=_normal(ks[4], (L, D_MODEL, N_BRANCH * D_MODEL), D_MODEL ** -0.5),
        b_gate=_normal(ks[5], (L, N_BRANCH * D_MODEL), 0.02),
        rel_bias=_normal(ks[6], (L, A_HEADS, 2 * A_MAX_REL + 1), 0.2),
        g_q_lat=1.0 + _normal(ks[7], (L, B_Q_LORA), 0.05),
        w_uq=_normal(ks[8], (L, B_Q_LORA, B_HEADS * (B_NOPE + B_ROPE)), B_Q_LORA ** -0.5),
        g_kv_lat=1.0 + _normal(ks[9], (L, B_KV_LORA), 0.05),
        w_ukv=_normal(ks[10], (L, B_KV_LORA, B_HEADS * (B_NOPE + B_V)), B_KV_LORA ** -0.5),
        b_forget=FORGET_BIAS_MEAN + _normal(ks[11], (L, C_HEADS), 0.5),
        w_dw=_normal(ks[12], (L, CONV_K, CONV_CH), CONV_K ** -0.5),
        b_dw=_normal(ks[13], (L, CONV_CH), 0.02),
        g_conv_ln=1.0 + _normal(ks[14], (L, CONV_CH), 0.05),
        b_conv_ln=_normal(ks[15], (L, CONV_CH), 0.02),
        w_branch=_normal(ks[16], (L, N_BRANCH, BRANCH_W, D_MODEL), BRANCH_W ** -0.5),
        w_o=_normal(ks[17], (L, D_MODEL, D_MODEL), D_MODEL ** -0.5),
        g_ffn=1.0 + _normal(ks[18], (L, D_MODEL), 0.05),
        w_up=_normal(ks[19], (L, D_MODEL, D_FF), D_MODEL ** -0.5),
        w_down=_normal(ks[20], (L, D_FF, D_MODEL), D_FF ** -0.5),
        g_final=1.0 + _normal(ks[21], (D_MODEL,), 0.05),
    )


def reference(x, positions, g_mix, w_in, w_gate, b_gate, rel_bias, g_q_lat, w_uq, g_kv_lat, w_ukv,
              b_forget, w_dw, b_dw, g_conv_ln, b_conv_ln, w_branch, w_o, g_ffn, w_up, w_down, g_final):
    for l in range(DEPTH):
        x = hybrid_layer(x, positions, g_mix[l], w_in[l], w_gate[l], b_gate[l], rel_bias[l],
                         g_q_lat[l], w_uq[l], g_kv_lat[l], w_ukv[l], b_forget[l], w_dw[l], b_dw[l],
                         g_conv_ln[l], b_conv_ln[l], w_branch[l], w_o[l], g_ffn[l], w_up[l], w_down[l])
    return rms_norm(x, g_final)
```

```python
import functools

import numpy as np
import jax
import jax.numpy as jnp
from jax import lax
from jax.experimental import pallas as pl
from jax.experimental.pallas import tpu as pltpu

D_MODEL = 1024
BATCH = 4
SEQ = 4096
DEPTH = 2
N_TOK = BATCH * SEQ
CHUNK = 64
HEAD_DIM = 64
N_HEADS = 4
NORM_EPS = 1e-6
A_LEFT_CHUNKS = 8
A_MAX_REL = 128
B_Q_LORA = 256
B_KV_LORA = 128
B_NOPE = 64
B_ROPE = 32
B_V = 64
ROPE_THETA = 10000.0
CONV_CH = 256
CONV_K = 31
BRANCH_W = 256
D_FF = 4 * D_MODEL

LANES = 128
HEAD_PAD = LANES
QKV_PAD = N_HEADS * HEAD_PAD
NEG = -1e30

TM = 512
TILES_PER_SEQ = SEQ // TM
T_ATT = 512
TQ_A = 256
CONV_HALO = 32
VMEM_LIMIT = 56 * 1024 * 1024

F32 = jnp.float32
BF16 = jnp.bfloat16


def _dot(a, b):
    return jnp.dot(a, b, preferred_element_type=F32)


def _dot_nt(a, b):
    return lax.dot_general(a, b, (((1,), (1,)), ((), ())), preferred_element_type=F32)


def _rms(x, g):
    return x * lax.rsqrt(jnp.mean(x * x, axis=-1, keepdims=True) + NORM_EPS) * g


def _sigmoid(x):
    return 1.0 / (1.0 + jnp.exp(-x))


def _split3(x):
    hi = x.astype(BF16)
    r1 = x - hi.astype(F32)
    mid = r1.astype(BF16)
    lo = (r1 - mid.astype(F32)).astype(BF16)
    return hi, mid, lo


def _const_spec(shape):
    return pl.BlockSpec(shape, lambda *_: (0,) * len(shape))


def _params(*sem):
    return pltpu.CompilerParams(dimension_semantics=sem, vmem_limit_bytes=VMEM_LIMIT)


def _rope_kernel(pos_ref, invf_ref, cos_ref, sin_ref):
    ang = pos_ref[...].astype(F32) * invf_ref[...]
    cos_ref[...] = jnp.cos(ang)
    sin_ref[...] = jnp.sin(ang)


def _rope_tables(positions):
    half = B_ROPE // 2
    inv_freq = 1.0 / (ROPE_THETA ** (jnp.arange(half, dtype=F32) / half))
    row = jnp.zeros((LANES,), F32).at[B_NOPE:B_NOPE + B_ROPE].set(jnp.tile(inv_freq, 2))
    pos = positions.reshape(N_TOK, 1)
    tm = 1024
    return pl.pallas_call(
        _rope_kernel,
        grid=(N_TOK // tm,),
        in_specs=[pl.BlockSpec((tm, 1), lambda i: (i, 0)), _const_spec((1, LANES))],
        out_specs=[pl.BlockSpec((tm, LANES), lambda i: (i, 0))] * 2,
        out_shape=[jax.ShapeDtypeStruct((N_TOK, LANES), F32)] * 2,
        compiler_params=_params("parallel"),
        name="rope_tables",
    )(pos, row.reshape(1, LANES))


def _inproj_kernel(x_ref, gmix_ref, wa_ref, wb_ref, wc_ref, wd_ref, cos_ref, sin_ref,
                   gq_ref, wuq_ref, wuqr_ref, gkv_ref, wuk_ref, wuv_ref, bf_ref,
                   tri_ref, eq_ref, ek_ref, oneq_ref, onek_ref, onev_ref,
                   h_ref, qa_ref, ka_ref, va_ref, qb_ref, kb_ref, vb_ref,
                   qc_ref, kc_ref, vc_ref, u_ref, carry_ref):
    i = pl.program_id(0)
    h = _rms(x_ref[...], gmix_ref[...]).astype(BF16)
    h_ref[...] = h

    pa = _dot(h, wa_ref[...])
    qa_ref[...] = pa[:, 0:256].astype(BF16)
    ka_ref[...] = pa[:, 256:512].astype(BF16)
    va_ref[...] = pa[:, 512:768].astype(BF16)

    pb = _dot(h, wb_ref[...])
    cos = cos_ref[...]
    sin = sin_ref[...]
    cos4 = jnp.concatenate([cos] * N_HEADS, axis=-1)
    sin4 = jnp.concatenate([sin] * N_HEADS, axis=-1)
    qn = _rms(pb[:, 0:B_Q_LORA], gq_ref[...]).astype(BF16)
    qb = _dot(qn, wuq_ref[...]) * cos4 + _dot(qn, wuqr_ref[...]) * sin4
    qb_ref[...] = (qb * ((B_NOPE + B_ROPE) ** -0.5)).astype(BF16)
    kvn = _rms(pb[:, B_Q_LORA:B_Q_LORA + B_KV_LORA], gkv_ref[...]).astype(BF16)
    krot = pb[:, 384:512] * cos + pb[:, 512:640] * sin
    kb = _dot(kvn, wuk_ref[...]) + jnp.concatenate([krot] * N_HEADS, axis=-1)
    kb_ref[...] = kb.astype(BF16)
    vb_ref[...] = (_dot(kvn, wuv_ref[...]) + onev_ref[...]).astype(BF16)

    pc = _dot(h, wc_ref[...])

    @pl.when(i % TILES_PER_SEQ == 0)
    def _():
        carry_ref[...] = jnp.zeros_like(carry_ref)

    z = pc[:, 3 * QKV_PAD:3 * QKV_PAD + LANES] + bf_ref[...]
    logf = jnp.minimum(z, 0.0) - jnp.log(1.0 + jnp.exp(-jnp.abs(z)))
    tri = tri_ref[...]
    l_hi, l_mid, l_lo = _split3(logf)
    fcum = _dot(tri, l_hi) + _dot(tri, l_mid) + _dot(tri, l_lo) + carry_ref[...]
    carry_ref[...] = fcum[TM - 1:TM, :]
    fcat = jnp.concatenate(_split3(fcum), axis=-1)
    qc_ref[...] = (pc[:, 0:QKV_PAD] + _dot(fcat, eq_ref[...]) + oneq_ref[...]).astype(BF16)
    kc_ref[...] = (pc[:, QKV_PAD:2 * QKV_PAD] + _dot(fcat, ek_ref[...]) + onek_ref[...]).astype(BF16)
    vc_ref[...] = (pc[:, 2 * QKV_PAD:3 * QKV_PAD] + onev_ref[...]).astype(BF16)

    pd = _dot(h, wd_ref[...])
    u_ref[...] = pd[:, 0:CONV_CH] * _sigmoid(pd[:, CONV_CH:2 * CONV_CH])


def _pad_heads(w, width):
    k = w.shape[0]
    w = w.reshape(k, N_HEADS, width)
    return jnp.pad(w, ((0, 0), (0, 0), (0, HEAD_PAD - width))).reshape(k, QKV_PAD)


def _rot_half_cols(w):
    half = B_ROPE // 2
    return jnp.concatenate([-w[..., half:], w[..., :half]], axis=-1)


def _inproj_consts():
    tri = np.tril(np.ones((TM, TM), np.float32))
    eq = np.zeros((3 * LANES, QKV_PAD), np.float32)
    ek = np.zeros((3 * LANES, QKV_PAD), np.float32)
    oneq = np.zeros((1, QKV_PAD), np.float32)
    onek = np.zeros((1, QKV_PAD), np.float32)
    onev = np.zeros((1, QKV_PAD), np.float32)
    for hd in range(N_HEADS):
        base = hd * HEAD_PAD + HEAD_DIM
        for p in range(3):
            eq[p * LANES + hd, base + p] = 1.0
            ek[p * LANES + hd, base + 3 + p] = -1.0
            onek[0, base + p] = 1.0
            oneq[0, base + 3 + p] = 1.0
        onev[0, base] = 1.0
    return (jnp.asarray(tri, BF16), jnp.asarray(eq, BF16), jnp.asarray(ek, BF16),
            jnp.asarray(oneq), jnp.asarray(onek), jnp.asarray(onev))


def _inproj_weights(w_in, w_uq, w_ukv, b_forget):
    o = 0
    cols = {}
    for name, width in (("aq", 256), ("ak", 256), ("av", 256), ("bq", B_Q_LORA), ("bkv", B_KV_LORA),
                        ("bkr", B_ROPE), ("cq", 256), ("ck", 256), ("cv", 256), ("cf", N_HEADS),
                        ("dv", CONV_CH), ("dg", CONV_CH)):
        cols[name] = w_in[:, o:o + width]
        o += width
    scale = HEAD_DIM ** -0.5
    wa = jnp.concatenate([cols["aq"] * scale, cols["ak"], cols["av"]], axis=1)

    def rope_block(w):
        return jnp.pad(w, ((0, 0), (B_NOPE, HEAD_PAD - B_NOPE - B_ROPE)))

    wb = jnp.concatenate([cols["bq"], cols["bkv"], rope_block(cols["bkr"]),
                          rope_block(_rot_half_cols(cols["bkr"]))], axis=1)
    wc = jnp.concatenate([_pad_heads(cols["cq"] * scale, HEAD_DIM), _pad_heads(cols["ck"], HEAD_DIM),
                          _pad_heads(cols["cv"], HEAD_DIM),
                          jnp.pad(cols["cf"], ((0, 0), (0, LANES - N_HEADS)))], axis=1)
    wd = jnp.concatenate([cols["dv"], cols["dg"]], axis=1)

    uq = w_uq.reshape(B_Q_LORA, N_HEADS, B_NOPE + B_ROPE)
    uq_rot = jnp.concatenate([jnp.zeros_like(uq[..., :B_NOPE]), _rot_half_cols(uq[..., B_NOPE:])], axis=-1)
    wuq = _pad_heads(uq.reshape(B_Q_LORA, -1), B_NOPE + B_ROPE)
    wuqr = _pad_heads(uq_rot.reshape(B_Q_LORA, -1), B_NOPE + B_ROPE)
    ukv = w_ukv.reshape(B_KV_LORA, N_HEADS, B_NOPE + B_V)
    wuk = _pad_heads(ukv[..., :B_NOPE].reshape(B_KV_LORA, -1), B_NOPE)
    wuv = _pad_heads(ukv[..., B_NOPE:].reshape(B_KV_LORA, -1), B_V)
    bf = jnp.pad(b_forget, (0, LANES - N_HEADS)).reshape(1, LANES)
    return [t.astype(BF16) for t in (wa, wb, wc, wd)], [t.astype(BF16) for t in (wuq, wuqr, wuk, wuv)], bf


def _inproj(x2, g_mix, wmain, wup, bf, g_q_lat, g_kv_lat, cos_t, sin_t, consts):
    wa, wb, wc, wd = wmain
    wuq, wuqr, wuk, wuv = wup
    tri, eq, ek, oneq, onek, onev = consts
    row = lambda w: pl.BlockSpec((TM, w), lambda i: (i, 0))
    ins = [
        (x2, row(D_MODEL)), (g_mix.reshape(1, -1), None), (wa, None), (wb, None), (wc, None), (wd, None),
        (cos_t, row(LANES)), (sin_t, row(LANES)),
        (g_q_lat.reshape(1, -1), None), (wuq, None), (wuqr, None),
        (g_kv_lat.reshape(1, -1), None), (wuk, None), (wuv, None), (bf, None),
        (tri, None), (eq, None), (ek, None), (oneq, None), (onek, None), (onev, None),
    ]
    args = [a for a, _ in ins]
    specs = [s if s is not None else _const_spec(a.shape) for a, s in ins]
    outs = [(D_MODEL, BF16)] + [(256, BF16)] * 3 + [(QKV_PAD, BF16)] * 6 + [(CONV_CH, F32)]
    return pl.pallas_call(
        _inproj_kernel,
        grid=(N_TOK // TM,),
        in_specs=specs,
        out_specs=[row(w) for w, _ in outs],
        out_shape=[jax.ShapeDtypeStruct((N_TOK, w), dt) for w, dt in outs],
        scratch_shapes=[pltpu.VMEM((1, LANES), F32)],
        compiler_params=_params("arbitrary"),
        name="inproj",
    )(*args)


def _band_kernel(q_ref, k0_ref, k1_ref, k2_ref, v0_ref, v1_ref, v2_ref, tbl_ref, o_ref):
    i = pl.program_id(1)
    k_refs = (k0_ref, k1_ref, k2_ref)
    v_refs = (v0_ref, v1_ref, v2_ref)
    outs = []
    for hd in range(N_HEADS):
        sl = slice(hd * HEAD_DIM, (hd + 1) * HEAD_DIM)
        q = q_ref[:, sl]
        s = []
        for j in range(3):
            sj = _dot_nt(q, k_refs[j][:, sl]) + tbl_ref[j, hd]
            if j < 2:
                sj = jnp.where(i >= 2 - j, sj, NEG)
            s.append(sj)
        m = jnp.maximum(jnp.maximum(s[0].max(-1, keepdims=True), s[1].max(-1, keepdims=True)),
                        s[2].max(-1, keepdims=True))
        p = [jnp.exp(sj - m) for sj in s]
        l = p[0].sum(-1, keepdims=True) + p[1].sum(-1, keepdims=True) + p[2].sum(-1, keepdims=True)
        o = sum(_dot(p[j].astype(BF16), v_refs[j][:, sl]) for j in range(3))
        outs.append(o / l)
    o_ref[...] = jnp.concatenate(outs, axis=-1).astype(BF16)


def _band_table(rel_bias):
    r = np.arange(TQ_A)[:, None]
    c = np.arange(TQ_A)[None, :]
    tabs = []
    for j in range(3):
        rel = (2 - j) * TQ_A + r - c
        idx = np.clip(rel, -A_MAX_REL, A_MAX_REL) + A_MAX_REL
        cd = (2 - j) * (TQ_A // CHUNK) + r // CHUNK - c // CHUNK
        valid = (cd >= 0) & (cd <= A_LEFT_CHUNKS)
        tabs.append(jnp.where(valid[None], rel_bias[:, idx], NEG))
    return jnp.stack(tabs).astype(F32)


def _band_attention(qa, ka, va, tbl):
    nt = SEQ // TQ_A
    q_spec = pl.BlockSpec((TQ_A, 256), lambda b, i: (b * nt + i, 0))

    def kv_spec(j):
        return pl.BlockSpec((TQ_A, 256), lambda b, i: (b * nt + jnp.maximum(i - 2 + j, 0), 0))

    return pl.pallas_call(
        _band_kernel,
        grid=(BATCH, nt),
        in_specs=[q_spec] + [kv_spec(j) for j in range(3)] * 2 + [_const_spec(tbl.shape)],
        out_specs=q_spec,
        out_shape=jax.ShapeDtypeStruct((N_TOK, 256), BF16),
        compiler_params=_params("parallel", "parallel"),
        name="band_attention",
    )(qa, ka, ka, ka, va, va, va, tbl)


def _flash_kernel(q_ref, k_ref, v_ref, o_ref, *, unit):
    i = pl.program_id(1)
    t = T_ATT
    shift = unit.bit_length() - 1
    row = lax.broadcasted_iota(jnp.int32, (t, t), 0) >> shift
    col = lax.broadcasted_iota(jnp.int32, (t, t), 1) >> shift
    visible = col <= row
    outs = []
    for hd in range(N_HEADS):
        sl = slice(hd * HEAD_PAD, (hd + 1) * HEAD_PAD)
        q = q_ref[:, sl]

        def step(j, m, acc, diag, sl=sl, q=q):
            off = pl.multiple_of(j * t, t)
            s = _dot_nt(q, k_ref[pl.ds(off, t), sl])
            if diag:
                s = jnp.where(visible, s, NEG)
            m_new = jnp.maximum(m, s.max(-1, keepdims=True))
            alpha = jnp.exp(m - m_new)
            p = jnp.exp(s - m_new).astype(BF16)
            acc = alpha * acc + _dot(p, v_ref[pl.ds(off, t), sl])
            return m_new, acc

        m0 = jnp.full((t, 1), NEG, F32)
        acc0 = jnp.zeros((t, HEAD_PAD), F32)
        m, acc = lax.fori_loop(0, i, lambda j, c: step(j, c[0], c[1], False), (m0, acc0))
        m, acc = step(i, m, acc, True)
        outs.append(acc[:, :HEAD_DIM] / acc[:, HEAD_DIM:HEAD_DIM + 1])
    o_ref[...] = jnp.concatenate(outs, axis=-1).astype(BF16)


def _flash_attention(q, k, v, unit, name):
    nt = SEQ // T_ATT
    return pl.pallas_call(
        functools.partial(_flash_kernel, unit=unit),
        grid=(BATCH, nt),
        in_specs=[pl.BlockSpec((T_ATT, QKV_PAD), lambda b, i: (b * nt + i, 0)),
                  pl.BlockSpec((SEQ, QKV_PAD), lambda b, i: (b, 0)),
                  pl.BlockSpec((SEQ, QKV_PAD), lambda b, i: (b, 0))],
        out_specs=pl.BlockSpec((T_ATT, 256), lambda b, i: (b * nt + i, 0)),
        out_shape=jax.ShapeDtypeStruct((N_TOK, 256), BF16),
        compiler_params=_params("parallel", "parallel"),
        name=name,
    )(q, k, v)


def _conv_kernel(cur_ref, prev_ref, w_ref, b_ref, g_ref, beta_ref, o_ref, ext_ref):
    i = pl.program_id(1)
    ext_ref[0:CONV_HALO, :] = jnp.where(i > 0, prev_ref[...], 0.0)
    ext_ref[CONV_HALO:CONV_HALO + TM, :] = cur_ref[...]
    acc = jnp.zeros((TM, CONV_CH), F32) + b_ref[...]
    first = CONV_HALO - (CONV_K - 1)
    for kk in range(CONV_K):
        acc = acc + ext_ref[first + kk:first + kk + TM, :] * w_ref[kk:kk + 1, :]
    mu = jnp.mean(acc, axis=-1, keepdims=True)
    xc = acc - mu
    var = jnp.mean(xc * xc, axis=-1, keepdims=True)
    y = xc * lax.rsqrt(var + NORM_EPS) * g_ref[...] + beta_ref[...]
    o_ref[...] = (y * _sigmoid(y)).astype(BF16)


def _conv_module(u, w_dw, b_dw, g_ln, b_ln):
    halo_per_tile = TM // CONV_HALO
    w = jnp.pad(w_dw, ((0, 1), (0, 0)))
    return pl.pallas_call(
        _conv_kernel,
        grid=(BATCH, TILES_PER_SEQ),
        in_specs=[pl.BlockSpec((TM, CONV_CH), lambda b, i: (b * TILES_PER_SEQ + i, 0)),
                  pl.BlockSpec((CONV_HALO, CONV_CH),
                               lambda b, i: (jnp.maximum((b * TILES_PER_SEQ + i) * halo_per_tile - 1, 0), 0)),
                  _const_spec(w.shape)] + [_const_spec((1, CONV_CH))] * 3,
        out_specs=pl.BlockSpec((TM, CONV_CH), lambda b, i: (b * TILES_PER_SEQ + i, 0)),
        out_shape=jax.ShapeDtypeStruct((N_TOK, CONV_CH), BF16),
        scratch_shapes=[pltpu.VMEM((CONV_HALO + TM, CONV_CH), F32)],
        compiler_params=_params("parallel", "parallel"),
        name="conv_module",
    )(u, u, w, b_dw.reshape(1, -1), g_ln.reshape(1, -1), b_ln.reshape(1, -1))


def _merge_kernel(x_ref, h_ref, oa_ref, ob_ref, oc_ref, od_ref, wg_ref, bg_ref, wbr_ref, wo_ref, out_ref):
    h = h_ref[...]
    acc = jnp.zeros((TM, D_MODEL), F32)
    for n, o_ref in enumerate((oa_ref, ob_ref, oc_ref, od_ref)):
        sl = slice(n * D_MODEL, (n + 1) * D_MODEL)
        gate = _sigmoid(_dot(h, wg_ref[:, sl]) + bg_ref[:, sl])
        acc = acc + gate * _dot(o_ref[...], wbr_ref[n])
    out_ref[...] = x_ref[...] + _dot(acc.astype(BF16), wo_ref[...])


def _merge(x2, h, oa, ob, oc, od, w_gate, b_gate, w_branch, w_o):
    row = lambda w: pl.BlockSpec((TM, w), lambda i: (i, 0))
    wg = w_gate.astype(BF16)
    wbr = w_branch.astype(BF16)
    wo = w_o.astype(BF16)
    bg = b_gate.reshape(1, -1)
    return pl.pallas_call(
        _merge_kernel,
        grid=(N_TOK // TM,),
        in_specs=[row(D_MODEL), row(D_MODEL)] + [row(BRANCH_W)] * 4
                 + [_const_spec(wg.shape), _const_spec(bg.shape), _const_spec(wbr.shape), _const_spec(wo.shape)],
        out_specs=row(D_MODEL),
        out_shape=jax.ShapeDtypeStruct((N_TOK, D_MODEL), F32),
        compiler_params=_params("parallel"),
        name="merge",
    )(x2, h, oa, ob, oc, od, wg, bg, wbr, wo)


def _ffn_kernel(x_ref, g_ref, wup_ref, wdn_ref, gfin_ref, out_ref, *, final):
    x = x_ref[...]
    hf = _rms(x, g_ref[...]).astype(BF16)
    acc = x
    for c in range(D_FF // D_MODEL):
        sl = slice(c * D_MODEL, (c + 1) * D_MODEL)
        up = jnp.maximum(_dot(hf, wup_ref[:, sl]), 0.0)
        acc = acc + _dot((up * up).astype(BF16), wdn_ref[sl, :])
    if final:
        acc = _rms(acc, gfin_ref[...])
    out_ref[...] = acc


def _ffn(x2, g_ffn, w_up, w_down, g_final, final):
    row = pl.BlockSpec((TM, D_MODEL), lambda i: (i, 0))
    wup = w_up.astype(BF16)
    wdn = w_down.astype(BF16)
    return pl.pallas_call(
        functools.partial(_ffn_kernel, final=final),
        grid=(N_TOK // TM,),
        in_specs=[row, _const_spec((1, D_MODEL)), _const_spec(wup.shape), _const_spec(wdn.shape),
                  _const_spec((1, D_MODEL))],
        out_specs=row,
        out_shape=jax.ShapeDtypeStruct((N_TOK, D_MODEL), F32),
        compiler_params=_params("parallel"),
        name="ffn",
    )(x2, g_ffn.reshape(1, -1), wup, wdn, g_final.reshape(1, -1))


def kernel(x, positions, g_mix, w_in, w_gate, b_gate, rel_bias, g_q_lat, w_uq, g_kv_lat, w_ukv, b_forget,
           w_dw, b_dw, g_conv_ln, b_conv_ln, w_branch, w_o, g_ffn, w_up, w_down, g_final):
    x2 = x.reshape(N_TOK, D_MODEL)
    cos_t, sin_t = _rope_tables(positions)
    consts = _inproj_consts()
    for l in range(DEPTH):
        wmain, wup, bf = _inproj_weights(w_in[l], w_uq[l], w_ukv[l], b_forget[l])
        (h, qa, ka, va, qb, kb, vb, qc, kc, vc, u) = _inproj(
            x2, g_mix[l], wmain, wup, bf, g_q_lat[l], g_kv_lat[l], cos_t, sin_t, consts)
        oa = _band_attention(qa, ka, va, _band_table(rel_bias[l]))
        ob = _flash_attention(qb, kb, vb, CHUNK, "flash_b")
        oc = _flash_attention(qc, kc, vc, 1, "flash_c")
        od = _conv_module(u, w_dw[l], b_dw[l], g_conv_ln[l], b_conv_ln[l])
        x2 = _merge(x2, h, oa, ob, oc, od, w_gate[l], b_gate[l], w_branch[l], w_o[l])
        x2 = _ffn(x2, g_ffn[l], w_up[l], w_down[l], g_final, final=(l == DEPTH - 1))
    return x2.reshape(BATCH, SEQ, D_MODEL)
```

```python
import functools

import numpy as np
import jax
import jax.numpy as jnp
from jax import lax
from jax.experimental import pallas as pl
from jax.experimental.pallas import tpu as pltpu

D_MODEL = 1024
BATCH = 4
SEQ = 4096
DEPTH = 2
N_TOK = BATCH * SEQ
CHUNK = 64
HEAD_DIM = 64
N_HEADS = 4
NORM_EPS = 1e-6
A_LEFT_CHUNKS = 8
A_MAX_REL = 128
B_Q_LORA = 256
B_KV_LORA = 128
B_NOPE = 64
B_ROPE = 32
B_V = 64
ROPE_THETA = 10000.0
CONV_CH = 256
CONV_K = 31
BRANCH_W = 256
D_FF = 4 * D_MODEL

LANES = 128
HEAD_PAD = LANES
QKV_PAD = N_HEADS * HEAD_PAD
NEG = -1e30

TM = 512
TILES_PER_SEQ = SEQ // TM
T_ATT = 512
TQ_A = 256
CONV_HALO = 32
VMEM_LIMIT = 56 * 1024 * 1024

F32 = jnp.float32
BF16 = jnp.bfloat16


def _dot(a, b):
    return jnp.dot(a, b, preferred_element_type=F32)


def _dot_nt(a, b):
    return lax.dot_general(a, b, (((1,), (1,)), ((), ())), preferred_element_type=F32)


def _rms(x, g):
    return x * lax.rsqrt(jnp.mean(x * x, axis=-1, keepdims=True) + NORM_EPS) * g


def _sigmoid(x):
    return 1.0 / (1.0 + jnp.exp(-x))


def _split3(x):
    hi = x.astype(BF16)
    r1 = x - hi.astype(F32)
    mid = r1.astype(BF16)
    lo = (r1 - mid.astype(F32)).astype(BF16)
    return hi, mid, lo


def _const_spec(shape):
    return pl.BlockSpec(shape, lambda *_: (0,) * len(shape))


def _params(*sem):
    return pltpu.CompilerParams(dimension_semantics=sem, vmem_limit_bytes=VMEM_LIMIT)


def _rope_kernel(pos_ref, invf_ref, cos_ref, sin_ref):
    ang = pos_ref[...].astype(F32) * invf_ref[...]
    cos_ref[...] = jnp.cos(ang)
    sin_ref[...] = jnp.sin(ang)


def _rope_tables(positions):
    half = B_ROPE // 2
    inv_freq = 1.0 / (ROPE_THETA ** (jnp.arange(half, dtype=F32) / half))
    row = jnp.zeros((LANES,), F32).at[B_NOPE:B_NOPE + B_ROPE].set(jnp.tile(inv_freq, 2))
    pos = positions.reshape(N_TOK, 1)
    tm = 1024
    return pl.pallas_call(
        _rope_kernel,
        grid=(N_TOK // tm,),
        in_specs=[pl.BlockSpec((tm, 1), lambda i: (i, 0)), _const_spec((1, LANES))],
        out_specs=[pl.BlockSpec((tm, LANES), lambda i: (i, 0))] * 2,
        out_shape=[jax.ShapeDtypeStruct((N_TOK, LANES), F32)] * 2,
        compiler_params=_params("parallel"),
        name="rope_tables",
    )(pos, row.reshape(1, LANES))


def _inproj_kernel(x_ref, gmix_ref, wa_ref, wb_ref, wc_ref, wd_ref, cos_ref, sin_ref,
                   gq_ref, wuq_ref, wuqr_ref, gkv_ref, wuk_ref, wuv_ref, bf_ref,
                   tri_ref, eq_ref, ek_ref, oneq_ref, onek_ref, onev_ref,
                   h_ref, qa_ref, ka_ref, va_ref, qb_ref, kb_ref, vb_ref,
                   qc_ref, kc_ref, vc_ref, u_ref, carry_ref):
    i = pl.program_id(0)
    h = _rms(x_ref[...], gmix_ref[...]).astype(BF16)
    h_ref[...] = h

    pa = _dot(h, wa_ref[...])
    qa_ref[...] = pa[:, 0:256].astype(BF16)
    ka_ref[...] = pa[:, 256:512].astype(BF16)
    va_ref[...] = pa[:, 512:768].astype(BF16)

    pb = _dot(h, wb_ref[...])
    cos = cos_ref[...]
    sin = sin_ref[...]
    cos4 = jnp.concatenate([cos] * N_HEADS, axis=-1)
    sin4 = jnp.concatenate([sin] * N_HEADS, axis=-1)
    qn = _rms(pb[:, 0:B_Q_LORA], gq_ref[...]).astype(BF16)
    qb = _dot(qn, wuq_ref[...]) * cos4 + _dot(qn, wuqr_ref[...]) * sin4
    qb_ref[...] = (qb * ((B_NOPE + B_ROPE) ** -0.5)).astype(BF16)
    kvn = _rms(pb[:, B_Q_LORA:B_Q_LORA + B_KV_LORA], gkv_ref[...]).astype(BF16)
    krot = pb[:, 384:512] * cos + pb[:, 512:640] * sin
    kb = _dot(kvn, wuk_ref[...]) + jnp.concatenate([krot] * N_HEADS, axis=-1)
    kb_ref[...] = kb.astype(BF16)
    vb_ref[...] = (_dot(kvn, wuv_ref[...]) + onev_ref[...]).astype(BF16)

    pc = _dot(h, wc_ref[...])

    @pl.when(i % TILES_PER_SEQ == 0)
    def _():
        carry_ref[...] = jnp.zeros_like(carry_ref)

    z = pc[:, 3 * QKV_PAD:3 * QKV_PAD + LANES] + bf_ref[...]
    logf = jnp.minimum(z, 0.0) - jnp.log(1.0 + jnp.exp(-jnp.abs(z)))
    tri = tri_ref[...]
    l_hi, l_mid, l_lo = _split3(logf)
    fcum = _dot(tri, l_hi) + _dot(tri, l_mid) + _dot(tri, l_lo) + carry_ref[...]
    carry_ref[...] = fcum[TM - 1:TM, :]
    fcat = jnp.concatenate(_split3(fcum), axis=-1)
    qc_ref[...] = (pc[:, 0:QKV_PAD] + _dot(fcat, eq_ref[...]) + oneq_ref[...]).astype(BF16)
    kc_ref[...] = (pc[:, QKV_PAD:2 * QKV_PAD] + _dot(fcat, ek_ref[...]) + onek_ref[...]).astype(BF16)
    vc_ref[...] = (pc[:, 2 * QKV_PAD:3 * QKV_PAD] + onev_ref[...]).astype(BF16)

    pd = _dot(h, wd_ref[...])
    u_ref[...] = pd[:, 0:CONV_CH] * _sigmoid(pd[:, CONV_CH:2 * CONV_CH])


def _pad_heads(w, width):
    k = w.shape[0]
    w = w.reshape(k, N_HEADS, width)
    return jnp.pad(w, ((0, 0), (0, 0), (0, HEAD_PAD - width))).reshape(k, QKV_PAD)


def _rot_half_cols(w):
    half = B_ROPE // 2
    return jnp.concatenate([-w[..., half:], w[..., :half]], axis=-1)


def _inproj_consts():
    tri = np.tril(np.ones((TM, TM), np.float32))
    eq = np.zeros((3 * LANES, QKV_PAD), np.float32)
    ek = np.zeros((3 * LANES, QKV_PAD), np.float32)
    oneq = np.zeros((1, QKV_PAD), np.float32)
    onek = np.zeros((1, QKV_PAD), np.float32)
    onev = np.zeros((1, QKV_PAD), np.float32)
    for hd in range(N_HEADS):
        base = hd * HEAD_PAD + HEAD_DIM
        for p in range(3):
            eq[p * LANES + hd, base + p] = 1.0
            ek[p * LANES + hd, base + 3 + p] = -1.0
            onek[0, base + p] = 1.0
            oneq[0, base + 3 + p] = 1.0
        onev[0, base] = 1.0
    return (jnp.asarray(tri, BF16), jnp.asarray(eq, BF16), jnp.asarray(ek, BF16),
            jnp.asarray(oneq), jnp.asarray(onek), jnp.asarray(onev))


def _inproj_weights(w_in, w_uq, w_ukv, b_forget):
    o = 0
    cols = {}
    for name, width in (("aq", 256), ("ak", 256), ("av", 256), ("bq", B_Q_LORA), ("bkv", B_KV_LORA),
                        ("bkr", B_ROPE), ("cq", 256), ("ck", 256), ("cv", 256), ("cf", N_HEADS),
                        ("dv", CONV_CH), ("dg", CONV_CH)):
        cols[name] = w_in[:, o:o + width]
        o += width
    scale = HEAD_DIM ** -0.5
    wa = jnp.concatenate([cols["aq"] * scale, cols["ak"], cols["av"]], axis=1)

    def rope_block(w):
        return jnp.pad(w, ((0, 0), (B_NOPE, HEAD_PAD - B_NOPE - B_ROPE)))

    wb = jnp.concatenate([cols["bq"], cols["bkv"], rope_block(cols["bkr"]),
                          rope_block(_rot_half_cols(cols["bkr"]))], axis=1)
    wc = jnp.concatenate([_pad_heads(cols["cq"] * scale, HEAD_DIM), _pad_heads(cols["ck"], HEAD_DIM),
                          _pad_heads(cols["cv"], HEAD_DIM),
                          jnp.pad(cols["cf"], ((0, 0), (0, LANES - N_HEADS)))], axis=1)
    wd = jnp.concatenate([cols["dv"], cols["dg"]], axis=1)

    uq = w_uq.reshape(B_Q_LORA, N_HEADS, B_NOPE + B_ROPE)
    uq_rot = jnp.concatenate([jnp.zeros_like(uq[..., :B_NOPE]), _rot_half_cols(uq[..., B_NOPE:])], axis=-1)
    wuq = _pad_heads(uq.reshape(B_Q_LORA, -1), B_NOPE + B_ROPE)
    wuqr = _pad_heads(uq_rot.reshape(B_Q_LORA, -1), B_NOPE + B_ROPE)
    ukv = w_ukv.reshape(B_KV_LORA, N_HEADS, B_NOPE + B_V)
    wuk = _pad_heads(ukv[..., :B_NOPE].reshape(B_KV_LORA, -1), B_NOPE)
    wuv = _pad_heads(ukv[..., B_NOPE:].reshape(B_KV_LORA, -1), B_V)
    bf = jnp.pad(b_forget, (0, LANES - N_HEADS)).reshape(1, LANES)
    return [t.astype(BF16) for t in (wa, wb, wc, wd)], [t.astype(BF16) for t in (wuq, wuqr, wuk, wuv)], bf


def _inproj(x2, g_mix, wmain, wup, bf, g_q_lat, g_kv_lat, cos_t, sin_t, consts):
    wa, wb, wc, wd = wmain
    wuq, wuqr, wuk, wuv = wup
    tri, eq, ek, oneq, onek, onev = consts
    row = lambda w: pl.BlockSpec((TM, w), lambda i: (i, 0))
    ins = [
        (x2, row(D_MODEL)), (g_mix.reshape(1, -1), None), (wa, None), (wb, None), (wc, None), (wd, None),
        (cos_t, row(LANES)), (sin_t, row(LANES)),
        (g_q_lat.reshape(1, -1), None), (wuq, None), (wuqr, None),
        (g_kv_lat.reshape(1, -1), None), (wuk, None), (wuv, None), (bf, None),
        (tri, None), (eq, None), (ek, None), (oneq, None), (onek, None), (onev, None),
    ]
    args = [a for a, _ in ins]
    specs = [s if s is not None else _const_spec(a.shape) for a, s in ins]
    outs = [(D_MODEL, BF16)] + [(256, BF16)] * 3 + [(QKV_PAD, BF16)] * 6 + [(CONV_CH, F32)]
    return pl.pallas_call(
        _inproj_kernel,
        grid=(N_TOK // TM,),
        in_specs=specs,
        out_specs=[row(w) for w, _ in outs],
        out_shape=[jax.ShapeDtypeStruct((N_TOK, w), dt) for w, dt in outs],
        scratch_shapes=[pltpu.VMEM((1, LANES), F32)],
        compiler_params=_params("arbitrary"),
        name="inproj",
    )(*args)


def _band_kernel(q_ref, k0_ref, k1_ref, k2_ref, v0_ref, v1_ref, v2_ref, tbl_ref, o_ref):
    i = pl.program_id(1)
    k_refs = (k0_ref, k1_ref, k2_ref)
    v_refs = (v0_ref, v1_ref, v2_ref)
    outs = []
    for hd in range(N_HEADS):
        sl = slice(hd * HEAD_DIM, (hd + 1) * HEAD_DIM)
        q = q_ref[:, sl]
        s = []
        for j in range(3):
            sj = _dot_nt(q, k_refs[j][:, sl]) + tbl_ref[j, hd]
            if j < 2:
                sj = jnp.where(i >= 2 - j, sj, NEG)
            s.append(sj)
        m = jnp.maximum(jnp.maximum(s[0].max(-1, keepdims=True), s[1].max(-1, keepdims=True)),
                        s[2].max(-1, keepdims=True))
        p = [jnp.exp(sj - m) for sj in s]
        l = p[0].sum(-1, keepdims=True) + p[1].sum(-1, keepdims=True) + p[2].sum(-1, keepdims=True)
        o = sum(_dot(p[j].astype(BF16), v_refs[j][:, sl]) for j in range(3))
        outs.append(o / l)
    o_ref[...] = jnp.concatenate(outs, axis=-1).astype(BF16)


def _band_table(rel_bias):
    w = TQ_A
    r = np.arange(w)[:, None]
    c = np.arange(w)[None, :]
    tabs = []
    for j in range(3):
        rel = (2 - j) * w - (np.arange(2 * w - 1) - (w - 1))
        idx = np.clip(rel, -A_MAX_REL, A_MAX_REL) + A_MAX_REL
        diag = jnp.pad(rel_bias[:, idx], ((0, 0), (0, 1)))
        skew = jnp.tile(diag, (1, w))[:, :w * (2 * w - 1)].reshape(N_HEADS, w, 2 * w - 1)
        bias = skew[:, :, w - 1:]
        cd = (2 - j) * (w // CHUNK) + r // CHUNK - c // CHUNK
        valid = (cd >= 0) & (cd <= A_LEFT_CHUNKS)
        tabs.append(jnp.where(valid[None], bias, NEG))
    return jnp.stack(tabs).astype(F32)


def _band_attention(qa, ka, va, tbl):
    nt = SEQ // TQ_A
    q_spec = pl.BlockSpec((TQ_A, 256), lambda b, i: (b * nt + i, 0))

    def kv_spec(j):
        return pl.BlockSpec((TQ_A, 256), lambda b, i: (b * nt + jnp.maximum(i - 2 + j, 0), 0))

    return pl.pallas_call(
        _band_kernel,
        grid=(BATCH, nt),
        in_specs=[q_spec] + [kv_spec(j) for j in range(3)] * 2 + [_const_spec(tbl.shape)],
        out_specs=q_spec,
        out_shape=jax.ShapeDtypeStruct((N_TOK, 256), BF16),
        compiler_params=_params("parallel", "parallel"),
        name="band_attention",
    )(qa, ka, ka, ka, va, va, va, tbl)


def _flash_kernel(q_ref, k_ref, v_ref, o_ref, *, unit):
    i = pl.program_id(1)
    t = T_ATT
    shift = unit.bit_length() - 1
    row = lax.broadcasted_iota(jnp.int32, (t, t), 0) >> shift
    col = lax.broadcasted_iota(jnp.int32, (t, t), 1) >> shift
    visible = col <= row
    outs = []
    for hd in range(N_HEADS):
        sl = slice(hd * HEAD_PAD, (hd + 1) * HEAD_PAD)
        q = q_ref[:, sl]

        def step(j, m, acc, diag, sl=sl, q=q):
            off = pl.multiple_of(j * t, t)
            s = _dot_nt(q, k_ref[pl.ds(off, t), sl])
            if diag:
                s = jnp.where(visible, s, NEG)
            m_new = jnp.maximum(m, s.max(-1, keepdims=True))
            alpha = jnp.exp(m - m_new)
            p = jnp.exp(s - m_new).astype(BF16)
            acc = alpha * acc + _dot(p, v_ref[pl.ds(off, t), sl])
            return m_new, acc

        m0 = jnp.full((t, 1), NEG, F32)
        acc0 = jnp.zeros((t, HEAD_PAD), F32)
        m, acc = lax.fori_loop(0, i, lambda j, c: step(j, c[0], c[1], False), (m0, acc0))
        m, acc = step(i, m, acc, True)
        outs.append(acc[:, :HEAD_DIM] / acc[:, HEAD_DIM:HEAD_DIM + 1])
    o_ref[...] = jnp.concatenate(outs, axis=-1).astype(BF16)


def _flash_attention(q, k, v, unit, name):
    nt = SEQ // T_ATT
    return pl.pallas_call(
        functools.partial(_flash_kernel, unit=unit),
        grid=(BATCH, nt),
        in_specs=[pl.BlockSpec((T_ATT, QKV_PAD), lambda b, i: (b * nt + i, 0)),
                  pl.BlockSpec((SEQ, QKV_PAD), lambda b, i: (b, 0)),
                  pl.BlockSpec((SEQ, QKV_PAD), lambda b, i: (b, 0))],
        out_specs=pl.BlockSpec((T_ATT, 256), lambda b, i: (b * nt + i, 0)),
        out_shape=jax.ShapeDtypeStruct((N_TOK, 256), BF16),
        compiler_params=_params("parallel", "parallel"),
        name=name,
    )(q, k, v)


def _conv_kernel(cur_ref, prev_ref, w_ref, b_ref, g_ref, beta_ref, o_ref, ext_ref):
    i = pl.program_id(1)
    ext_ref[0:CONV_HALO, :] = jnp.where(i > 0, prev_ref[...], 0.0)
    ext_ref[CONV_HALO:CONV_HALO + TM, :] = cur_ref[...]
    acc = jnp.zeros((TM, CONV_CH), F32) + b_ref[...]
    first = CONV_HALO - (CONV_K - 1)
    for kk in range(CONV_K):
        acc = acc + ext_ref[first + kk:first + kk + TM, :] * w_ref[kk:kk + 1, :]
    mu = jnp.mean(acc, axis=-1, keepdims=True)
    xc = acc - mu
    var = jnp.mean(xc * xc, axis=-1, keepdims=True)
    y = xc * lax.rsqrt(var + NORM_EPS) * g_ref[...] + beta_ref[...]
    o_ref[...] = (y * _sigmoid(y)).astype(BF16)


def _conv_module(u, w_dw, b_dw, g_ln, b_ln):
    halo_per_tile = TM // CONV_HALO
    w = jnp.pad(w_dw, ((0, 1), (0, 0)))
    return pl.pallas_call(
        _conv_kernel,
        grid=(BATCH, TILES_PER_SEQ),
        in_specs=[pl.BlockSpec((TM, CONV_CH), lambda b, i: (b * TILES_PER_SEQ + i, 0)),
                  pl.BlockSpec((CONV_HALO, CONV_CH),
                               lambda b, i: (jnp.maximum((b * TILES_PER_SEQ + i) * halo_per_tile - 1, 0), 0)),
                  _const_spec(w.shape)] + [_const_spec((1, CONV_CH))] * 3,
        out_specs=pl.BlockSpec((TM, CONV_CH), lambda b, i: (b * TILES_PER_SEQ + i, 0)),
        out_shape=jax.ShapeDtypeStruct((N_TOK, CONV_CH), BF16),
        scratch_shapes=[pltpu.VMEM((CONV_HALO + TM, CONV_CH), F32)],
        compiler_params=_params("parallel", "parallel"),
        name="conv_module",
    )(u, u, w, b_dw.reshape(1, -1), g_ln.reshape(1, -1), b_ln.reshape(1, -1))


def _merge_kernel(x_ref, h_ref, oa_ref, ob_ref, oc_ref, od_ref, wg_ref, bg_ref, wbr_ref, wo_ref, out_ref):
    h = h_ref[...]
    acc = jnp.zeros((TM, D_MODEL), F32)
    for n, o_ref in enumerate((oa_ref, ob_ref, oc_ref, od_ref)):
        sl = slice(n * D_MODEL, (n + 1) * D_MODEL)
        gate = _sigmoid(_dot(h, wg_ref[:, sl]) + bg_ref[:, sl])
        acc = acc + gate * _dot(o_ref[...], wbr_ref[n])
    out_ref[...] = x_ref[...] + _dot(acc.astype(BF16), wo_ref[...])


def _merge(x2, h, oa, ob, oc, od, w_gate, b_gate, w_branch, w_o):
    row = lambda w: pl.BlockSpec((TM, w), lambda i: (i, 0))
    wg = w_gate.astype(BF16)
    wbr = w_branch.astype(BF16)
    wo = w_o.astype(BF16)
    bg = b_gate.reshape(1, -1)
    return pl.pallas_call(
        _merge_kernel,
        grid=(N_TOK // TM,),
        in_specs=[row(D_MODEL), row(D_MODEL)] + [row(BRANCH_W)] * 4
                 + [_const_spec(wg.shape), _const_spec(bg.shape), _const_spec(wbr.shape), _const_spec(wo.shape)],
        out_specs=row(D_MODEL),
        out_shape=jax.ShapeDtypeStruct((N_TOK, D_MODEL), F32),
        compiler_params=_params("parallel"),
        name="merge",
    )(x2, h, oa, ob, oc, od, wg, bg, wbr, wo)


def _ffn_kernel(x_ref, g_ref, wup_ref, wdn_ref, gfin_ref, out_ref, *, final):
    x = x_ref[...]
    hf = _rms(x, g_ref[...]).astype(BF16)
    acc = x
    for c in range(D_FF // D_MODEL):
        sl = slice(c * D_MODEL, (c + 1) * D_MODEL)
        up = jnp.maximum(_dot(hf, wup_ref[:, sl]), 0.0)
        acc = acc + _dot((up * up).astype(BF16), wdn_ref[sl, :])
    if final:
        acc = _rms(acc, gfin_ref[...])
    out_ref[...] = acc


def _ffn(x2, g_ffn, w_up, w_down, g_final, final):
    row = pl.BlockSpec((TM, D_MODEL), lambda i: (i, 0))
    wup = w_up.astype(BF16)
    wdn = w_down.astype(BF16)
    return pl.pallas_call(
        functools.partial(_ffn_kernel, final=final),
        grid=(N_TOK // TM,),
        in_specs=[row, _const_spec((1, D_MODEL)), _const_spec(wup.shape), _const_spec(wdn.shape),
                  _const_spec((1, D_MODEL))],
        out_specs=row,
        out_shape=jax.ShapeDtypeStruct((N_TOK, D_MODEL), F32),
        compiler_params=_params("parallel"),
        name="ffn",
    )(x2, g_ffn.reshape(1, -1), wup, wdn, g_final.reshape(1, -1))


def kernel(x, positions, g_mix, w_in, w_gate, b_gate, rel_bias, g_q_lat, w_uq, g_kv_lat, w_ukv, b_forget,
           w_dw, b_dw, g_conv_ln, b_conv_ln, w_branch, w_o, g_ffn, w_up, w_down, g_final):
    x2 = x.reshape(N_TOK, D_MODEL)
    cos_t, sin_t = _rope_tables(positions)
    consts = _inproj_consts()
    for l in range(DEPTH):
        wmain, wup, bf = _inproj_weights(w_in[l], w_uq[l], w_ukv[l], b_forget[l])
        (h, qa, ka, va, qb, kb, vb, qc, kc, vc, u) = _inproj(
            x2, g_mix[l], wmain, wup, bf, g_q_lat[l], g_kv_lat[l], cos_t, sin_t, consts)
        oa = _band_attention(qa, ka, va, _band_table(rel_bias[l]))
        ob = _flash_attention(qb, kb, vb, CHUNK, "flash_b")
        oc = _flash_attention(qc, kc, vc, 1, "flash_c")
        od = _conv_module(u, w_dw[l], b_dw[l], g_conv_ln[l], b_conv_ln[l])
        x2 = _merge(x2, h, oa, ob, oc, od, w_gate[l], b_gate[l], w_branch[l], w_o[l])
        x2 = _ffn(x2, g_ffn[l], w_up[l], w_down[l], g_final, final=(l == DEPTH - 1))
    return x2.reshape(BATCH, SEQ, D_MODEL)
```

```python
import functools

import numpy as np
import jax
import jax.numpy as jnp
from jax import lax
from jax.experimental import pallas as pl
from jax.experimental.pallas import tpu as pltpu

D_MODEL = 1024
BATCH = 4
SEQ = 4096
DEPTH = 2
N_TOK = BATCH * SEQ
CHUNK = 64
HEAD_DIM = 64
N_HEADS = 4
NORM_EPS = 1e-6
A_LEFT_CHUNKS = 8
A_MAX_REL = 128
B_Q_LORA = 256
B_KV_LORA = 128
B_NOPE = 64
B_ROPE = 32
B_V = 64
ROPE_THETA = 10000.0
CONV_CH = 256
CONV_K = 31
BRANCH_W = 256
D_FF = 4 * D_MODEL

LANES = 128
HEAD_PAD = LANES
QKV_PAD = N_HEADS * HEAD_PAD
NEG = -1e30
LOG2E = 1.4426950408889634

TM = 512
TILES_PER_SEQ = SEQ // TM
T_ATT = 512
TQ_A = 256
CONV_HALO = 32
VMEM_LIMIT = 56 * 1024 * 1024

F32 = jnp.float32
BF16 = jnp.bfloat16


def _dot(a, b):
    return jnp.dot(a, b, preferred_element_type=F32)


def _dot_nt(a, b):
    return lax.dot_general(a, b, (((1,), (1,)), ((), ())), preferred_element_type=F32)


def _rms(x, g):
    return x * lax.rsqrt(jnp.mean(x * x, axis=-1, keepdims=True) + NORM_EPS) * g


def _sigmoid(x):
    return 1.0 / (1.0 + jnp.exp(-x))


def _split3(x):
    hi = x.astype(BF16)
    r1 = x - hi.astype(F32)
    mid = r1.astype(BF16)
    lo = (r1 - mid.astype(F32)).astype(BF16)
    return hi, mid, lo


def _const_spec(shape):
    return pl.BlockSpec(shape, lambda *_: (0,) * len(shape))


def _params(*sem):
    return pltpu.CompilerParams(dimension_semantics=sem, vmem_limit_bytes=VMEM_LIMIT)


def _rope_kernel(pos_ref, invf_ref, cos_ref, sin_ref):
    ang = pos_ref[...].astype(F32) * invf_ref[...]
    cos_ref[...] = jnp.cos(ang)
    sin_ref[...] = jnp.sin(ang)


def _rope_tables(positions):
    half = B_ROPE // 2
    inv_freq = 1.0 / (ROPE_THETA ** (jnp.arange(half, dtype=F32) / half))
    row = jnp.zeros((LANES,), F32).at[B_NOPE:B_NOPE + B_ROPE].set(jnp.tile(inv_freq, 2))
    pos = positions.reshape(N_TOK, 1)
    tm = 1024
    return pl.pallas_call(
        _rope_kernel,
        grid=(N_TOK // tm,),
        in_specs=[pl.BlockSpec((tm, 1), lambda i: (i, 0)), _const_spec((1, LANES))],
        out_specs=[pl.BlockSpec((tm, LANES), lambda i: (i, 0))] * 2,
        out_shape=[jax.ShapeDtypeStruct((N_TOK, LANES), F32)] * 2,
        compiler_params=_params("parallel"),
        name="rope_tables",
    )(pos, row.reshape(1, LANES))


def _inproj_kernel(x_ref, gmix_ref, wa_ref, wb_ref, wc_ref, wd_ref, cos_ref, sin_ref,
                   gq_ref, wuq_ref, wuqr_ref, gkv_ref, wuk_ref, wuv_ref, bf_ref,
                   tri_ref, eq_ref, ek_ref, oneq_ref, onek_ref, onev_ref,
                   h_ref, qa_ref, ka_ref, va_ref, qb_ref, kb_ref, vb_ref,
                   qc_ref, kc_ref, vc_ref, u_ref, carry_ref):
    i = pl.program_id(0)
    h = _rms(x_ref[...], gmix_ref[...]).astype(BF16)
    h_ref[...] = h

    pa = _dot(h, wa_ref[...])
    qa_ref[...] = pa[:, 0:256].astype(BF16)
    ka_ref[...] = pa[:, 256:512].astype(BF16)
    va_ref[...] = pa[:, 512:768].astype(BF16)

    pb = _dot(h, wb_ref[...])
    cos = cos_ref[...]
    sin = sin_ref[...]
    cos4 = jnp.concatenate([cos] * N_HEADS, axis=-1)
    sin4 = jnp.concatenate([sin] * N_HEADS, axis=-1)
    qn = _rms(pb[:, 0:B_Q_LORA], gq_ref[...]).astype(BF16)
    qb = _dot(qn, wuq_ref[...]) * cos4 + _dot(qn, wuqr_ref[...]) * sin4
    qb_ref[...] = (qb * ((B_NOPE + B_ROPE) ** -0.5 * LOG2E)).T.astype(BF16)
    kvn = _rms(pb[:, B_Q_LORA:B_Q_LORA + B_KV_LORA], gkv_ref[...]).astype(BF16)
    krot = pb[:, 384:512] * cos + pb[:, 512:640] * sin
    kb = _dot(kvn, wuk_ref[...]) + jnp.concatenate([krot] * N_HEADS, axis=-1)
    kb_ref[...] = kb.astype(BF16)
    vb_ref[...] = (_dot(kvn, wuv_ref[...]) + onev_ref[...]).T.astype(BF16)

    pc = _dot(h, wc_ref[...])

    @pl.when(i % TILES_PER_SEQ == 0)
    def _():
        carry_ref[...] = jnp.zeros_like(carry_ref)

    z = pc[:, 3 * QKV_PAD:3 * QKV_PAD + LANES] + bf_ref[...]
    logf = jnp.minimum(z, 0.0) - jnp.log(1.0 + jnp.exp(-jnp.abs(z)))
    tri = tri_ref[...]
    l_hi, l_mid, l_lo = _split3(logf)
    fcum = _dot(tri, l_hi) + _dot(tri, l_mid) + _dot(tri, l_lo) + carry_ref[...]
    carry_ref[...] = fcum[TM - 1:TM, :]
    fcat = jnp.concatenate(_split3(fcum * LOG2E), axis=-1)
    qc = pc[:, 0:QKV_PAD] * LOG2E + _dot(fcat, eq_ref[...]) + oneq_ref[...]
    qc_ref[...] = qc.T.astype(BF16)
    kc_ref[...] = (pc[:, QKV_PAD:2 * QKV_PAD] + _dot(fcat, ek_ref[...]) + onek_ref[...]).astype(BF16)
    vc_ref[...] = (pc[:, 2 * QKV_PAD:3 * QKV_PAD] + onev_ref[...]).T.astype(BF16)

    pd = _dot(h, wd_ref[...])
    u_ref[...] = pd[:, 0:CONV_CH] * _sigmoid(pd[:, CONV_CH:2 * CONV_CH])


def _pad_heads(w, width):
    k = w.shape[0]
    w = w.reshape(k, N_HEADS, width)
    return jnp.pad(w, ((0, 0), (0, 0), (0, HEAD_PAD - width))).reshape(k, QKV_PAD)


def _rot_half_cols(w):
    half = B_ROPE // 2
    return jnp.concatenate([-w[..., half:], w[..., :half]], axis=-1)


def _inproj_consts():
    tri = np.tril(np.ones((TM, TM), np.float32))
    eq = np.zeros((3 * LANES, QKV_PAD), np.float32)
    ek = np.zeros((3 * LANES, QKV_PAD), np.float32)
    oneq = np.zeros((1, QKV_PAD), np.float32)
    onek = np.zeros((1, QKV_PAD), np.float32)
    onev = np.zeros((1, QKV_PAD), np.float32)
    for hd in range(N_HEADS):
        base = hd * HEAD_PAD + HEAD_DIM
        for p in range(3):
            eq[p * LANES + hd, base + p] = 1.0
            ek[p * LANES + hd, base + 3 + p] = -1.0
            onek[0, base + p] = 1.0
            oneq[0, base + 3 + p] = 1.0
        onev[0, base] = 1.0
    return (jnp.asarray(tri, BF16), jnp.asarray(eq, BF16), jnp.asarray(ek, BF16),
            jnp.asarray(oneq), jnp.asarray(onek), jnp.asarray(onev))


def _inproj_weights(w_in, w_uq, w_ukv, b_forget):
    o = 0
    cols = {}
    for name, width in (("aq", 256), ("ak", 256), ("av", 256), ("bq", B_Q_LORA), ("bkv", B_KV_LORA),
                        ("bkr", B_ROPE), ("cq", 256), ("ck", 256), ("cv", 256), ("cf", N_HEADS),
                        ("dv", CONV_CH), ("dg", CONV_CH)):
        cols[name] = w_in[:, o:o + width]
        o += width
    scale = HEAD_DIM ** -0.5
    wa = jnp.concatenate([cols["aq"] * scale, cols["ak"], cols["av"]], axis=1)

    def rope_block(w):
        return jnp.pad(w, ((0, 0), (B_NOPE, HEAD_PAD - B_NOPE - B_ROPE)))

    wb = jnp.concatenate([cols["bq"], cols["bkv"], rope_block(cols["bkr"]),
                          rope_block(_rot_half_cols(cols["bkr"]))], axis=1)
    wc = jnp.concatenate([_pad_heads(cols["cq"] * scale, HEAD_DIM), _pad_heads(cols["ck"], HEAD_DIM),
                          _pad_heads(cols["cv"], HEAD_DIM),
                          jnp.pad(cols["cf"], ((0, 0), (0, LANES - N_HEADS)))], axis=1)
    wd = jnp.concatenate([cols["dv"], cols["dg"]], axis=1)

    uq = w_uq.reshape(B_Q_LORA, N_HEADS, B_NOPE + B_ROPE)
    uq_rot = jnp.concatenate([jnp.zeros_like(uq[..., :B_NOPE]), _rot_half_cols(uq[..., B_NOPE:])], axis=-1)
    wuq = _pad_heads(uq.reshape(B_Q_LORA, -1), B_NOPE + B_ROPE)
    wuqr = _pad_heads(uq_rot.reshape(B_Q_LORA, -1), B_NOPE + B_ROPE)
    ukv = w_ukv.reshape(B_KV_LORA, N_HEADS, B_NOPE + B_V)
    wuk = _pad_heads(ukv[..., :B_NOPE].reshape(B_KV_LORA, -1), B_NOPE)
    wuv = _pad_heads(ukv[..., B_NOPE:].reshape(B_KV_LORA, -1), B_V)
    bf = jnp.pad(b_forget, (0, LANES - N_HEADS)).reshape(1, LANES)
    return [t.astype(BF16) for t in (wa, wb, wc, wd)], [t.astype(BF16) for t in (wuq, wuqr, wuk, wuv)], bf


def _inproj(x2, g_mix, wmain, wup, bf, g_q_lat, g_kv_lat, cos_t, sin_t, consts):
    wa, wb, wc, wd = wmain
    wuq, wuqr, wuk, wuv = wup
    tri, eq, ek, oneq, onek, onev = consts
    row = lambda w: pl.BlockSpec((TM, w), lambda i: (i, 0))
    ins = [
        (x2, row(D_MODEL)), (g_mix.reshape(1, -1), None), (wa, None), (wb, None), (wc, None), (wd, None),
        (cos_t, row(LANES)), (sin_t, row(LANES)),
        (g_q_lat.reshape(1, -1), None), (wuq, None), (wuqr, None),
        (g_kv_lat.reshape(1, -1), None), (wuk, None), (wuv, None), (bf, None),
        (tri, None), (eq, None), (ek, None), (oneq, None), (onek, None), (onev, None),
    ]
    args = [a for a, _ in ins]
    specs = [s if s is not None else _const_spec(a.shape) for a, s in ins]
    def rows(w, dt):
        return jax.ShapeDtypeStruct((N_TOK, w), dt), row(w)

    slab = (jax.ShapeDtypeStruct((N_TOK // TM, QKV_PAD, TM), BF16),
            pl.BlockSpec((None, QKV_PAD, TM), lambda i: (i, 0, 0)))
    outs = ([rows(D_MODEL, BF16)] + [rows(256, BF16)] * 3
            + [slab, rows(QKV_PAD, BF16), slab] * 2 + [rows(CONV_CH, F32)])
    return pl.pallas_call(
        _inproj_kernel,
        grid=(N_TOK // TM,),
        in_specs=specs,
        out_specs=[s for _, s in outs],
        out_shape=[a for a, _ in outs],
        scratch_shapes=[pltpu.VMEM((1, LANES), F32)],
        compiler_params=_params("arbitrary"),
        name="inproj",
    )(*args)


def _band_kernel(q_ref, k0_ref, k1_ref, k2_ref, v0_ref, v1_ref, v2_ref, tbl_ref, o_ref):
    i = pl.program_id(1)
    k_refs = (k0_ref, k1_ref, k2_ref)
    v_refs = (v0_ref, v1_ref, v2_ref)
    outs = []
    for hd in range(N_HEADS):
        sl = slice(hd * HEAD_DIM, (hd + 1) * HEAD_DIM)
        q = q_ref[:, sl]
        s = []
        for j in range(3):
            sj = _dot_nt(q, k_refs[j][:, sl]) + tbl_ref[j, hd]
            if j < 2:
                sj = jnp.where(i >= 2 - j, sj, NEG)
            s.append(sj)
        m = jnp.maximum(jnp.maximum(s[0].max(-1, keepdims=True), s[1].max(-1, keepdims=True)),
                        s[2].max(-1, keepdims=True))
        p = [jnp.exp(sj - m) for sj in s]
        l = p[0].sum(-1, keepdims=True) + p[1].sum(-1, keepdims=True) + p[2].sum(-1, keepdims=True)
        o = sum(_dot(p[j].astype(BF16), v_refs[j][:, sl]) for j in range(3))
        outs.append(o / l)
    o_ref[...] = jnp.concatenate(outs, axis=-1).astype(BF16)


def _band_table(rel_bias):
    w = TQ_A
    r = np.arange(w)[:, None]
    c = np.arange(w)[None, :]
    tabs = []
    for j in range(3):
        rel = (2 - j) * w - (np.arange(2 * w - 1) - (w - 1))
        idx = np.clip(rel, -A_MAX_REL, A_MAX_REL) + A_MAX_REL
        diag = jnp.pad(rel_bias[:, idx], ((0, 0), (0, 1)))
        skew = jnp.tile(diag, (1, w))[:, :w * (2 * w - 1)].reshape(N_HEADS, w, 2 * w - 1)
        bias = skew[:, :, w - 1:]
        cd = (2 - j) * (w // CHUNK) + r // CHUNK - c // CHUNK
        valid = (cd >= 0) & (cd <= A_LEFT_CHUNKS)
        tabs.append(jnp.where(valid[None], bias, NEG))
    return jnp.stack(tabs).astype(F32)


def _band_attention(qa, ka, va, tbl):
    nt = SEQ // TQ_A
    q_spec = pl.BlockSpec((TQ_A, 256), lambda b, i: (b * nt + i, 0))

    def kv_spec(j):
        return pl.BlockSpec((TQ_A, 256), lambda b, i: (b * nt + jnp.maximum(i - 2 + j, 0), 0))

    return pl.pallas_call(
        _band_kernel,
        grid=(BATCH, nt),
        in_specs=[q_spec] + [kv_spec(j) for j in range(3)] * 2 + [_const_spec(tbl.shape)],
        out_specs=q_spec,
        out_shape=jax.ShapeDtypeStruct((N_TOK, 256), BF16),
        compiler_params=_params("parallel", "parallel"),
        name="band_attention",
    )(qa, ka, ka, ka, va, va, va, tbl)


def _flash_kernel(qt_ref, k_ref, vt_ref, o_ref, *, unit):
    i = pl.program_id(1)
    t = T_ATT
    shift = unit.bit_length() - 1
    key = lax.broadcasted_iota(jnp.int32, (t, t), 0) >> shift
    qry = lax.broadcasted_iota(jnp.int32, (t, t), 1) >> shift
    visible = key <= qry

    def step(j, carry, diag):
        off = pl.multiple_of(j * t, t)
        heads = [slice(hd * HEAD_PAD, (hd + 1) * HEAD_PAD) for hd in range(N_HEADS)]

        def scores(hd):
            s = _dot(k_ref[pl.ds(off, t), heads[hd]], qt_ref[heads[hd], :])
            return jnp.where(visible, s, NEG) if diag else s

        def softmax(hd, s):
            m_new = jnp.maximum(carry[hd][0], s.max(0, keepdims=True))
            return m_new, jnp.exp2(carry[hd][0] - m_new), jnp.exp2(s - m_new).astype(BF16)

        def update(hd, m_new, alpha, p):
            return m_new, alpha * carry[hd][1] + _dot(vt_ref[j, heads[hd], :], p)

        s, sm, new = {}, {}, {}
        for tick in range(N_HEADS + 2):
            if tick < N_HEADS:
                s[tick] = scores(tick)
            if 0 <= tick - 1 < N_HEADS:
                sm[tick - 1] = softmax(tick - 1, s.pop(tick - 1))
            if 0 <= tick - 2 < N_HEADS:
                new[tick - 2] = update(tick - 2, *sm.pop(tick - 2))
        return tuple(new[hd] for hd in range(N_HEADS))

    init = tuple((jnp.full((1, t), NEG, F32), jnp.zeros((HEAD_PAD, t), F32)) for _ in range(N_HEADS))
    carry = lax.fori_loop(0, i, lambda j, c: step(j, c, False), init)
    carry = step(i, carry, True)
    outs = [acc[:HEAD_DIM, :] / acc[HEAD_DIM:HEAD_DIM + 1, :] for _, acc in carry]
    o_ref[...] = jnp.concatenate(outs, axis=0).T.astype(BF16)


def _flash_attention(qt, k, vt, unit, name):
    nt = SEQ // T_ATT
    return pl.pallas_call(
        functools.partial(_flash_kernel, unit=unit),
        grid=(BATCH, nt),
        in_specs=[pl.BlockSpec((None, QKV_PAD, T_ATT), lambda b, i: (b * nt + i, 0, 0)),
                  pl.BlockSpec((SEQ, QKV_PAD), lambda b, i: (b, 0)),
                  pl.BlockSpec((nt, QKV_PAD, T_ATT), lambda b, i: (b, 0, 0))],
        out_specs=pl.BlockSpec((T_ATT, 256), lambda b, i: (b * nt + i, 0)),
        out_shape=jax.ShapeDtypeStruct((N_TOK, 256), BF16),
        compiler_params=_params("parallel", "parallel"),
        name=name,
    )(qt, k, vt)


def _conv_kernel(cur_ref, prev_ref, w_ref, b_ref, g_ref, beta_ref, o_ref, ext_ref):
    i = pl.program_id(1)
    ext_ref[0:CONV_HALO, :] = jnp.where(i > 0, prev_ref[...], 0.0)
    ext_ref[CONV_HALO:CONV_HALO + TM, :] = cur_ref[...]
    acc = jnp.zeros((TM, CONV_CH), F32) + b_ref[...]
    first = CONV_HALO - (CONV_K - 1)
    for kk in range(CONV_K):
        acc = acc + ext_ref[first + kk:first + kk + TM, :] * w_ref[kk:kk + 1, :]
    mu = jnp.mean(acc, axis=-1, keepdims=True)
    xc = acc - mu
    var = jnp.mean(xc * xc, axis=-1, keepdims=True)
    y = xc * lax.rsqrt(var + NORM_EPS) * g_ref[...] + beta_ref[...]
    o_ref[...] = (y * _sigmoid(y)).astype(BF16)


def _conv_module(u, w_dw, b_dw, g_ln, b_ln):
    halo_per_tile = TM // CONV_HALO
    w = jnp.pad(w_dw, ((0, 1), (0, 0)))
    return pl.pallas_call(
        _conv_kernel,
        grid=(BATCH, TILES_PER_SEQ),
        in_specs=[pl.BlockSpec((TM, CONV_CH), lambda b, i: (b * TILES_PER_SEQ + i, 0)),
                  pl.BlockSpec((CONV_HALO, CONV_CH),
                               lambda b, i: (jnp.maximum((b * TILES_PER_SEQ + i) * halo_per_tile - 1, 0), 0)),
                  _const_spec(w.shape)] + [_const_spec((1, CONV_CH))] * 3,
        out_specs=pl.BlockSpec((TM, CONV_CH), lambda b, i: (b * TILES_PER_SEQ + i, 0)),
        out_shape=jax.ShapeDtypeStruct((N_TOK, CONV_CH), BF16),
        scratch_shapes=[pltpu.VMEM((CONV_HALO + TM, CONV_CH), F32)],
        compiler_params=_params("parallel", "parallel"),
        name="conv_module",
    )(u, u, w, b_dw.reshape(1, -1), g_ln.reshape(1, -1), b_ln.reshape(1, -1))


def _merge_kernel(x_ref, h_ref, oa_ref, ob_ref, oc_ref, od_ref, wg_ref, bg_ref, wbr_ref, wo_ref, out_ref):
    h = h_ref[...]
    acc = jnp.zeros((TM, D_MODEL), F32)
    for n, o_ref in enumerate((oa_ref, ob_ref, oc_ref, od_ref)):
        sl = slice(n * D_MODEL, (n + 1) * D_MODEL)
        gate = _sigmoid(_dot(h, wg_ref[:, sl]) + bg_ref[:, sl])
        acc = acc + gate * _dot(o_ref[...], wbr_ref[n])
    out_ref[...] = x_ref[...] + _dot(acc.astype(BF16), wo_ref[...])


def _merge(x2, h, oa, ob, oc, od, w_gate, b_gate, w_branch, w_o):
    row = lambda w: pl.BlockSpec((TM, w), lambda i: (i, 0))
    wg = w_gate.astype(BF16)
    wbr = w_branch.astype(BF16)
    wo = w_o.astype(BF16)
    bg = b_gate.reshape(1, -1)
    return pl.pallas_call(
        _merge_kernel,
        grid=(N_TOK // TM,),
        in_specs=[row(D_MODEL), row(D_MODEL)] + [row(BRANCH_W)] * 4
                 + [_const_spec(wg.shape), _const_spec(bg.shape), _const_spec(wbr.shape), _const_spec(wo.shape)],
        out_specs=row(D_MODEL),
        out_shape=jax.ShapeDtypeStruct((N_TOK, D_MODEL), F32),
        compiler_params=_params("parallel"),
        name="merge",
    )(x2, h, oa, ob, oc, od, wg, bg, wbr, wo)


def _ffn_kernel(x_ref, g_ref, wup_ref, wdn_ref, gfin_ref, out_ref, *, final):
    x = x_ref[...]
    hf = _rms(x, g_ref[...]).astype(BF16)
    acc = x
    for c in range(D_FF // D_MODEL):
        sl = slice(c * D_MODEL, (c + 1) * D_MODEL)
        up = jnp.maximum(_dot(hf, wup_ref[:, sl]), 0.0)
        acc = acc + _dot((up * up).astype(BF16), wdn_ref[sl, :])
    if final:
        acc = _rms(acc, gfin_ref[...])
    out_ref[...] = acc


def _ffn(x2, g_ffn, w_up, w_down, g_final, final):
    row = pl.BlockSpec((TM, D_MODEL), lambda i: (i, 0))
    wup = w_up.astype(BF16)
    wdn = w_down.astype(BF16)
    return pl.pallas_call(
        functools.partial(_ffn_kernel, final=final),
        grid=(N_TOK // TM,),
        in_specs=[row, _const_spec((1, D_MODEL)), _const_spec(wup.shape), _const_spec(wdn.shape),
                  _const_spec((1, D_MODEL))],
        out_specs=row,
        out_shape=jax.ShapeDtypeStruct((N_TOK, D_MODEL), F32),
        compiler_params=_params("parallel"),
        name="ffn",
    )(x2, g_ffn.reshape(1, -1), wup, wdn, g_final.reshape(1, -1))


def kernel(x, positions, g_mix, w_in, w_gate, b_gate, rel_bias, g_q_lat, w_uq, g_kv_lat, w_ukv, b_forget,
           w_dw, b_dw, g_conv_ln, b_conv_ln, w_branch, w_o, g_ffn, w_up, w_down, g_final):
    x2 = x.reshape(N_TOK, D_MODEL)
    cos_t, sin_t = _rope_tables(positions)
    consts = _inproj_consts()
    for l in range(DEPTH):
        wmain, wup, bf = _inproj_weights(w_in[l], w_uq[l], w_ukv[l], b_forget[l])
        (h, qa, ka, va, qb, kb, vb, qc, kc, vc, u) = _inproj(
            x2, g_mix[l], wmain, wup, bf, g_q_lat[l], g_kv_lat[l], cos_t, sin_t, consts)
        oa = _band_attention(qa, ka, va, _band_table(rel_bias[l]))
        ob = _flash_attention(qb, kb, vb, CHUNK, "flash_b")
        oc = _flash_attention(qc, kc, vc, 1, "flash_c")
        od = _conv_module(u, w_dw[l], b_dw[l], g_conv_ln[l], b_conv_ln[l])
        x2 = _merge(x2, h, oa, ob, oc, od, w_gate[l], b_gate[l], w_branch[l], w_o[l])
        x2 = _ffn(x2, g_ffn[l], w_up[l], w_down[l], g_final, final=(l == DEPTH - 1))
    return x2.reshape(BATCH, SEQ, D_MODEL)
```

```python
import functools

import numpy as np
import jax
import jax.numpy as jnp
from jax import lax
from jax.experimental import pallas as pl
from jax.experimental.pallas import tpu as pltpu

D_MODEL = 1024
BATCH = 4
SEQ = 4096
DEPTH = 2
N_TOK = BATCH * SEQ
CHUNK = 64
HEAD_DIM = 64
N_HEADS = 4
NORM_EPS = 1e-6
A_LEFT_CHUNKS = 8
A_MAX_REL = 128
B_Q_LORA = 256
B_KV_LORA = 128
B_NOPE = 64
B_ROPE = 32
B_V = 64
ROPE_THETA = 10000.0
CONV_CH = 256
CONV_K = 31
BRANCH_W = 256
D_FF = 4 * D_MODEL

LANES = 128
SUBLANES = 8
HEAD_PAD = LANES
QKV_PAD = N_HEADS * HEAD_PAD
NEG = -1e30
LOG2E = 1.4426950408889634

TM = 512
TILES_PER_SEQ = SEQ // TM
T_ATT = 512
TQ_A = 256
CONV_HALO = 32
VMEM_LIMIT = 56 * 1024 * 1024

F32 = jnp.float32
BF16 = jnp.bfloat16


def _dot(a, b):
    return jnp.dot(a, b, preferred_element_type=F32)


def _dot_nt(a, b):
    return lax.dot_general(a, b, (((1,), (1,)), ((), ())), preferred_element_type=F32)


def _rms(x, g):
    return x * lax.rsqrt(jnp.mean(x * x, axis=-1, keepdims=True) + NORM_EPS) * g


def _sigmoid(x):
    return 1.0 / (1.0 + jnp.exp(-x))


def _split3(x):
    hi = x.astype(BF16)
    r1 = x - hi.astype(F32)
    mid = r1.astype(BF16)
    lo = (r1 - mid.astype(F32)).astype(BF16)
    return hi, mid, lo


def _const_spec(shape):
    return pl.BlockSpec(shape, lambda *_: (0,) * len(shape))


def _params(*sem):
    return pltpu.CompilerParams(dimension_semantics=sem, vmem_limit_bytes=VMEM_LIMIT)


def _rope_kernel(pos_ref, invf_ref, cos_ref, sin_ref):
    ang = pos_ref[...].astype(F32) * invf_ref[...]
    cos_ref[...] = jnp.cos(ang)
    sin_ref[...] = jnp.sin(ang)


def _rope_tables(positions):
    half = B_ROPE // 2
    inv_freq = 1.0 / (ROPE_THETA ** (jnp.arange(half, dtype=F32) / half))
    row = jnp.zeros((LANES,), F32).at[B_NOPE:B_NOPE + B_ROPE].set(jnp.tile(inv_freq, 2))
    pos = positions.reshape(N_TOK, 1)
    tm = 1024
    return pl.pallas_call(
        _rope_kernel,
        grid=(N_TOK // tm,),
        in_specs=[pl.BlockSpec((tm, 1), lambda i: (i, 0)), _const_spec((1, LANES))],
        out_specs=[pl.BlockSpec((tm, LANES), lambda i: (i, 0))] * 2,
        out_shape=[jax.ShapeDtypeStruct((N_TOK, LANES), F32)] * 2,
        compiler_params=_params("parallel"),
        name="rope_tables",
    )(pos, row.reshape(1, LANES))


def _inproj_kernel(x_ref, gmix_ref, wa_ref, wb_ref, wc_ref, wd_ref, cos_ref, sin_ref,
                   gq_ref, wuq_ref, wuqr_ref, gkv_ref, wuk_ref, wuv_ref, bf_ref,
                   tri_ref, eq_ref, ek_ref, oneq_ref, onek_ref, onev_ref, onesblk_ref,
                   h_ref, qa_ref, ka_ref, va_ref, qb_ref, kb_ref, vb_ref,
                   qc_ref, kc_ref, vc_ref, u_ref, carry_ref):
    i = pl.program_id(0)
    h = _rms(x_ref[...], gmix_ref[...]).astype(BF16)
    h_ref[...] = h

    pa = _dot(h, wa_ref[...])
    ka_ref[...] = pa[:, 256:512].astype(BF16)
    qt = (pa[:, 0:256] * LOG2E).T
    vt = pa[:, 512:768].T
    row = lax.broadcasted_iota(jnp.int32, (HEAD_PAD, TM), 0)
    own_rows = (row < HEAD_DIM, row >= HEAD_DIM)
    ones_blk = onesblk_ref[...]
    q_blocks, v_blocks = [], []
    for hd in range(N_HEADS):
        pair = qt[(hd // 2) * HEAD_PAD:(hd // 2 + 1) * HEAD_PAD]
        q_blocks.append(jnp.where(own_rows[hd % 2], pair, 0.0))
        v_blocks += [vt[hd * HEAD_DIM:(hd + 1) * HEAD_DIM], ones_blk]
    qa_ref[...] = jnp.concatenate(q_blocks, axis=0).astype(BF16)
    va_ref[...] = jnp.concatenate(v_blocks, axis=0).astype(BF16)

    pb = _dot(h, wb_ref[...])
    cos = cos_ref[...]
    sin = sin_ref[...]
    cos4 = jnp.concatenate([cos] * N_HEADS, axis=-1)
    sin4 = jnp.concatenate([sin] * N_HEADS, axis=-1)
    qn = _rms(pb[:, 0:B_Q_LORA], gq_ref[...]).astype(BF16)
    qb = _dot(qn, wuq_ref[...]) * cos4 + _dot(qn, wuqr_ref[...]) * sin4
    qb_ref[...] = (qb * ((B_NOPE + B_ROPE) ** -0.5 * LOG2E)).T.astype(BF16)
    kvn = _rms(pb[:, B_Q_LORA:B_Q_LORA + B_KV_LORA], gkv_ref[...]).astype(BF16)
    krot = pb[:, 384:512] * cos + pb[:, 512:640] * sin
    kb = _dot(kvn, wuk_ref[...]) + jnp.concatenate([krot] * N_HEADS, axis=-1)
    kb_ref[...] = kb.astype(BF16)
    vb_ref[...] = (_dot(kvn, wuv_ref[...]) + onev_ref[...]).T.astype(BF16)

    pc = _dot(h, wc_ref[...])

    @pl.when(i % TILES_PER_SEQ == 0)
    def _():
        carry_ref[...] = jnp.zeros_like(carry_ref)

    z = pc[:, 3 * QKV_PAD:3 * QKV_PAD + LANES] + bf_ref[...]
    logf = jnp.minimum(z, 0.0) - jnp.log(1.0 + jnp.exp(-jnp.abs(z)))
    tri = tri_ref[...]
    l_hi, l_mid, l_lo = _split3(logf)
    fcum = _dot(tri, l_hi) + _dot(tri, l_mid) + _dot(tri, l_lo) + carry_ref[...]
    carry_ref[...] = fcum[TM - 1:TM, :]
    fcat = jnp.concatenate(_split3(fcum * LOG2E), axis=-1)
    qc = pc[:, 0:QKV_PAD] * LOG2E + _dot(fcat, eq_ref[...]) + oneq_ref[...]
    qc_ref[...] = qc.T.astype(BF16)
    kc_ref[...] = (pc[:, QKV_PAD:2 * QKV_PAD] + _dot(fcat, ek_ref[...]) + onek_ref[...]).astype(BF16)
    vc_ref[...] = (pc[:, 2 * QKV_PAD:3 * QKV_PAD] + onev_ref[...]).T.astype(BF16)

    pd = _dot(h, wd_ref[...])
    u_ref[...] = pd[:, 0:CONV_CH] * _sigmoid(pd[:, CONV_CH:2 * CONV_CH])


def _pad_heads(w, width):
    k = w.shape[0]
    w = w.reshape(k, N_HEADS, width)
    return jnp.pad(w, ((0, 0), (0, 0), (0, HEAD_PAD - width))).reshape(k, QKV_PAD)


def _rot_half_cols(w):
    half = B_ROPE // 2
    return jnp.concatenate([-w[..., half:], w[..., :half]], axis=-1)


def _inproj_consts():
    tri = np.tril(np.ones((TM, TM), np.float32))
    eq = np.zeros((3 * LANES, QKV_PAD), np.float32)
    ek = np.zeros((3 * LANES, QKV_PAD), np.float32)
    oneq = np.zeros((1, QKV_PAD), np.float32)
    onek = np.zeros((1, QKV_PAD), np.float32)
    onev = np.zeros((1, QKV_PAD), np.float32)
    ones_blk = np.zeros((HEAD_DIM, TM), np.float32)
    ones_blk[0] = 1.0
    for hd in range(N_HEADS):
        base = hd * HEAD_PAD + HEAD_DIM
        for p in range(3):
            eq[p * LANES + hd, base + p] = 1.0
            ek[p * LANES + hd, base + 3 + p] = -1.0
            onek[0, base + p] = 1.0
            oneq[0, base + 3 + p] = 1.0
        onev[0, base] = 1.0
    return (jnp.asarray(tri, BF16), jnp.asarray(eq, BF16), jnp.asarray(ek, BF16),
            jnp.asarray(oneq), jnp.asarray(onek), jnp.asarray(onev), jnp.asarray(ones_blk))


def _inproj_weights(w_in, w_uq, w_ukv, b_forget):
    o = 0
    cols = {}
    for name, width in (("aq", 256), ("ak", 256), ("av", 256), ("bq", B_Q_LORA), ("bkv", B_KV_LORA),
                        ("bkr", B_ROPE), ("cq", 256), ("ck", 256), ("cv", 256), ("cf", N_HEADS),
                        ("dv", CONV_CH), ("dg", CONV_CH)):
        cols[name] = w_in[:, o:o + width]
        o += width
    scale = HEAD_DIM ** -0.5
    wa = jnp.concatenate([cols["aq"] * scale, cols["ak"], cols["av"]], axis=1)

    def rope_block(w):
        return jnp.pad(w, ((0, 0), (B_NOPE, HEAD_PAD - B_NOPE - B_ROPE)))

    wb = jnp.concatenate([cols["bq"], cols["bkv"], rope_block(cols["bkr"]),
                          rope_block(_rot_half_cols(cols["bkr"]))], axis=1)
    wc = jnp.concatenate([_pad_heads(cols["cq"] * scale, HEAD_DIM), _pad_heads(cols["ck"], HEAD_DIM),
                          _pad_heads(cols["cv"], HEAD_DIM),
                          jnp.pad(cols["cf"], ((0, 0), (0, LANES - N_HEADS)))], axis=1)
    wd = jnp.concatenate([cols["dv"], cols["dg"]], axis=1)

    uq = w_uq.reshape(B_Q_LORA, N_HEADS, B_NOPE + B_ROPE)
    uq_rot = jnp.concatenate([jnp.zeros_like(uq[..., :B_NOPE]), _rot_half_cols(uq[..., B_NOPE:])], axis=-1)
    wuq = _pad_heads(uq.reshape(B_Q_LORA, -1), B_NOPE + B_ROPE)
    wuqr = _pad_heads(uq_rot.reshape(B_Q_LORA, -1), B_NOPE + B_ROPE)
    ukv = w_ukv.reshape(B_KV_LORA, N_HEADS, B_NOPE + B_V)
    wuk = _pad_heads(ukv[..., :B_NOPE].reshape(B_KV_LORA, -1), B_NOPE)
    wuv = _pad_heads(ukv[..., B_NOPE:].reshape(B_KV_LORA, -1), B_V)
    bf = jnp.pad(b_forget, (0, LANES - N_HEADS)).reshape(1, LANES)
    return [t.astype(BF16) for t in (wa, wb, wc, wd)], [t.astype(BF16) for t in (wuq, wuqr, wuk, wuv)], bf


def _inproj(x2, g_mix, wmain, wup, bf, g_q_lat, g_kv_lat, cos_t, sin_t, consts):
    wa, wb, wc, wd = wmain
    wuq, wuqr, wuk, wuv = wup
    tri, eq, ek, oneq, onek, onev, ones_blk = consts
    row = lambda w: pl.BlockSpec((TM, w), lambda i: (i, 0))
    ins = [
        (x2, row(D_MODEL)), (g_mix.reshape(1, -1), None), (wa, None), (wb, None), (wc, None), (wd, None),
        (cos_t, row(LANES)), (sin_t, row(LANES)),
        (g_q_lat.reshape(1, -1), None), (wuq, None), (wuqr, None),
        (g_kv_lat.reshape(1, -1), None), (wuk, None), (wuv, None), (bf, None),
        (tri, None), (eq, None), (ek, None), (oneq, None), (onek, None), (onev, None), (ones_blk, None),
    ]
    args = [a for a, _ in ins]
    specs = [s if s is not None else _const_spec(a.shape) for a, s in ins]
    def rows(w, dt):
        return jax.ShapeDtypeStruct((N_TOK, w), dt), row(w)

    slab = (jax.ShapeDtypeStruct((N_TOK // TM, QKV_PAD, TM), BF16),
            pl.BlockSpec((None, QKV_PAD, TM), lambda i: (i, 0, 0)))
    outs = ([rows(D_MODEL, BF16), slab, rows(256, BF16), slab]
            + [slab, rows(QKV_PAD, BF16), slab] * 2 + [rows(CONV_CH, F32)])
    return pl.pallas_call(
        _inproj_kernel,
        grid=(N_TOK // TM,),
        in_specs=specs,
        out_specs=[s for _, s in outs],
        out_shape=[a for a, _ in outs],
        scratch_shapes=[pltpu.VMEM((1, LANES), F32)],
        compiler_params=_params("arbitrary"),
        name="inproj",
    )(*args)


def _band_kernel(qt_ref, k0_ref, k1_ref, k2_ref, vt0_ref, vt1_ref, vt2_ref, tbl_ref, o_ref):
    i = pl.program_id(1)
    k_refs = (k0_ref, k1_ref, k2_ref)
    vt_refs = (vt0_ref, vt1_ref, vt2_ref)

    def scores(hd):
        pair = slice((hd // 2) * HEAD_PAD, (hd // 2 + 1) * HEAD_PAD)
        qh = qt_ref[hd * HEAD_PAD:(hd + 1) * HEAD_PAD, :]
        s = []
        for j in range(3):
            sj = _dot(k_refs[j][:, pair], qh) + tbl_ref[j, hd]
            if j < 2:
                sj = jnp.where(i >= 2 - j, sj, NEG)
            s.append(sj)
        return s

    def softmax(s):
        m = jnp.maximum(jnp.maximum(s[0].max(0, keepdims=True), s[1].max(0, keepdims=True)),
                        s[2].max(0, keepdims=True))
        return [jnp.exp2(sj - m).astype(BF16) for sj in s]

    def weighted_sum(hd, p):
        rows = slice(hd * HEAD_PAD, (hd + 1) * HEAD_PAD)
        acc = _dot(vt_refs[0][rows, :], p[0]) + _dot(vt_refs[1][rows, :], p[1]) + _dot(vt_refs[2][rows, :], p[2])
        return acc[:HEAD_DIM, :] / acc[HEAD_DIM:HEAD_DIM + 1, :]

    s, p, outs = {}, {}, {}
    for tick in range(N_HEADS + 2):
        if tick < N_HEADS:
            s[tick] = scores(tick)
        if 0 <= tick - 1 < N_HEADS:
            p[tick - 1] = softmax(s.pop(tick - 1))
        if 0 <= tick - 2 < N_HEADS:
            outs[tick - 2] = weighted_sum(tick - 2, p.pop(tick - 2))
    o_ref[...] = jnp.concatenate([outs[hd] for hd in range(N_HEADS)], axis=0).T.astype(BF16)


def _band_table(rel_bias):
    w = TQ_A
    r = np.arange(w)[:, None]
    c = np.arange(w)[None, :]
    tabs = []
    for j in range(3):
        rel = (2 - j) * w - (np.arange(2 * w - 1) - (w - 1))
        idx = np.clip(rel, -A_MAX_REL, A_MAX_REL) + A_MAX_REL
        diag = jnp.pad(rel_bias[:, idx], ((0, 0), (0, 1)))
        skew = jnp.tile(diag, (1, w))[:, :w * (2 * w - 1)].reshape(N_HEADS, w, 2 * w - 1)
        bias = skew[:, :, w - 1:]
        cd = (2 - j) * (w // CHUNK) + r // CHUNK - c // CHUNK
        valid = (cd >= 0) & (cd <= A_LEFT_CHUNKS)
        tabs.append(jnp.where(valid[None], bias * LOG2E, NEG))
    return jnp.swapaxes(jnp.stack(tabs), 2, 3).astype(F32)


def _band_attention(qat, ka, vat, tbl):
    nt = SEQ // TQ_A
    per_slab = TM // TQ_A

    def block(b, i, j):
        return b * nt + jnp.maximum(i - 2 + j, 0)

    def slab_spec(j):
        return pl.BlockSpec((None, QKV_PAD, TQ_A),
                            lambda b, i: (block(b, i, j) // per_slab, 0, block(b, i, j) % per_slab))

    def k_spec(j):
        return pl.BlockSpec((TQ_A, 256), lambda b, i: (block(b, i, j), 0))

    return pl.pallas_call(
        _band_kernel,
        grid=(BATCH, nt),
        in_specs=([slab_spec(2)] + [k_spec(j) for j in range(3)] + [slab_spec(j) for j in range(3)]
                  + [_const_spec(tbl.shape)]),
        out_specs=pl.BlockSpec((TQ_A, 256), lambda b, i: (b * nt + i, 0)),
        out_shape=jax.ShapeDtypeStruct((N_TOK, 256), BF16),
        compiler_params=_params("parallel", "parallel"),
        name="band_attention",
    )(qat, ka, ka, ka, vat, vat, vat, tbl)


def _flash_kernel(qt_ref, k_ref, vt_ref, o_ref, *, unit):
    i = pl.program_id(1)
    t = T_ATT
    shift = unit.bit_length() - 1
    key = lax.broadcasted_iota(jnp.int32, (t, t), 0) >> shift
    qry = lax.broadcasted_iota(jnp.int32, (t, t), 1) >> shift
    visible = key <= qry

    def step(j, carry, diag):
        off = pl.multiple_of(j * t, t)
        heads = [slice(hd * HEAD_PAD, (hd + 1) * HEAD_PAD) for hd in range(N_HEADS)]

        def scores(hd):
            s = _dot(k_ref[pl.ds(off, t), heads[hd]], qt_ref[heads[hd], :])
            return jnp.where(visible, s, NEG) if diag else s

        def softmax(hd, s):
            m_new = jnp.maximum(carry[hd][0], s.max(0, keepdims=True))
            return m_new, jnp.exp2(carry[hd][0] - m_new), jnp.exp2(s - m_new).astype(BF16)

        def update(hd, m_new, alpha, p):
            return m_new, alpha * carry[hd][1] + _dot(vt_ref[j, heads[hd], :], p)

        s, sm, new = {}, {}, {}
        for tick in range(N_HEADS + 2):
            if tick < N_HEADS:
                s[tick] = scores(tick)
            if 0 <= tick - 1 < N_HEADS:
                sm[tick - 1] = softmax(tick - 1, s.pop(tick - 1))
            if 0 <= tick - 2 < N_HEADS:
                new[tick - 2] = update(tick - 2, *sm.pop(tick - 2))
        return tuple(new[hd] for hd in range(N_HEADS))

    init = tuple((jnp.full((1, t), NEG, F32), jnp.zeros((HEAD_PAD, t), F32)) for _ in range(N_HEADS))
    carry = lax.fori_loop(0, i, lambda j, c: step(j, c, False), init)
    carry = step(i, carry, True)
    outs = [acc[:HEAD_DIM, :] / acc[HEAD_DIM:HEAD_DIM + 1, :] for _, acc in carry]
    o_ref[...] = jnp.concatenate(outs, axis=0).T.astype(BF16)


def _flash_attention(qt, k, vt, unit, name):
    nt = SEQ // T_ATT
    return pl.pallas_call(
        functools.partial(_flash_kernel, unit=unit),
        grid=(BATCH, nt),
        in_specs=[pl.BlockSpec((None, QKV_PAD, T_ATT), lambda b, i: (b * nt + i, 0, 0)),
                  pl.BlockSpec((SEQ, QKV_PAD), lambda b, i: (b, 0)),
                  pl.BlockSpec((nt, QKV_PAD, T_ATT), lambda b, i: (b, 0, 0))],
        out_specs=pl.BlockSpec((T_ATT, 256), lambda b, i: (b * nt + i, 0)),
        out_shape=jax.ShapeDtypeStruct((N_TOK, 256), BF16),
        compiler_params=_params("parallel", "parallel"),
        name=name,
    )(qt, k, vt)


def _conv_kernel(cur_ref, prev_ref, w_ref, b_ref, g_ref, beta_ref, o_ref, ext_ref, shift_ref):
    i = pl.program_id(1)
    ext_ref[0:CONV_HALO, :] = jnp.where(i > 0, prev_ref[...], 0.0)
    ext_ref[CONV_HALO:CONV_HALO + TM, :] = cur_ref[...]
    acc = jnp.zeros((TM, CONV_CH), F32) + b_ref[...]
    first = CONV_HALO - (CONV_K - 1)
    for res in range(SUBLANES):
        taps = [kk for kk in range(CONV_K) if (first + kk) % SUBLANES == res]
        span = max((first + kk) // SUBLANES for kk in taps) * SUBLANES + TM
        shift_ref[0:span, :] = ext_ref[res:res + span, :]
        for kk in taps:
            start = (first + kk) // SUBLANES * SUBLANES
            acc = acc + shift_ref[start:start + TM, :] * w_ref[kk:kk + 1, :]
    mu = jnp.mean(acc, axis=-1, keepdims=True)
    xc = acc - mu
    var = jnp.mean(xc * xc, axis=-1, keepdims=True)
    y = xc * lax.rsqrt(var + NORM_EPS) * g_ref[...] + beta_ref[...]
    o_ref[...] = (y * _sigmoid(y)).astype(BF16)


def _conv_module(u, w_dw, b_dw, g_ln, b_ln):
    halo_per_tile = TM // CONV_HALO
    w = jnp.pad(w_dw, ((0, 1), (0, 0)))
    return pl.pallas_call(
        _conv_kernel,
        grid=(BATCH, TILES_PER_SEQ),
        in_specs=[pl.BlockSpec((TM, CONV_CH), lambda b, i: (b * TILES_PER_SEQ + i, 0)),
                  pl.BlockSpec((CONV_HALO, CONV_CH),
                               lambda b, i: (jnp.maximum((b * TILES_PER_SEQ + i) * halo_per_tile - 1, 0), 0)),
                  _const_spec(w.shape)] + [_const_spec((1, CONV_CH))] * 3,
        out_specs=pl.BlockSpec((TM, CONV_CH), lambda b, i: (b * TILES_PER_SEQ + i, 0)),
        out_shape=jax.ShapeDtypeStruct((N_TOK, CONV_CH), BF16),
        scratch_shapes=[pltpu.VMEM((CONV_HALO + TM, CONV_CH), F32)] * 2,
        compiler_params=_params("parallel", "parallel"),
        name="conv_module",
    )(u, u, w, b_dw.reshape(1, -1), g_ln.reshape(1, -1), b_ln.reshape(1, -1))


def _merge_kernel(x_ref, h_ref, oa_ref, ob_ref, oc_ref, od_ref, wg_ref, bg_ref, wbr_ref, wo_ref, out_ref):
    h = h_ref[...]
    acc = jnp.zeros((TM, D_MODEL), F32)
    for n, o_ref in enumerate((oa_ref, ob_ref, oc_ref, od_ref)):
        sl = slice(n * D_MODEL, (n + 1) * D_MODEL)
        gate = _sigmoid(_dot(h, wg_ref[:, sl]) + bg_ref[:, sl])
        acc = acc + gate * _dot(o_ref[...], wbr_ref[n])
    out_ref[...] = x_ref[...] + _dot(acc.astype(BF16), wo_ref[...])


def _merge(x2, h, oa, ob, oc, od, w_gate, b_gate, w_branch, w_o):
    row = lambda w: pl.BlockSpec((TM, w), lambda i: (i, 0))
    wg = w_gate.astype(BF16)
    wbr = w_branch.astype(BF16)
    wo = w_o.astype(BF16)
    bg = b_gate.reshape(1, -1)
    return pl.pallas_call(
        _merge_kernel,
        grid=(N_TOK // TM,),
        in_specs=[row(D_MODEL), row(D_MODEL)] + [row(BRANCH_W)] * 4
                 + [_const_spec(wg.shape), _const_spec(bg.shape), _const_spec(wbr.shape), _const_spec(wo.shape)],
        out_specs=row(D_MODEL),
        out_shape=jax.ShapeDtypeStruct((N_TOK, D_MODEL), F32),
        compiler_params=_params("parallel"),
        name="merge",
    )(x2, h, oa, ob, oc, od, wg, bg, wbr, wo)


def _ffn_kernel(x_ref, g_ref, wup_ref, wdn_ref, gfin_ref, out_ref, *, final):
    x = x_ref[...]
    hf = _rms(x, g_ref[...]).astype(BF16)
    acc = x
    for c in range(D_FF // D_MODEL):
        sl = slice(c * D_MODEL, (c + 1) * D_MODEL)
        up = jnp.maximum(_dot(hf, wup_ref[:, sl]), 0.0)
        acc = acc + _dot((up * up).astype(BF16), wdn_ref[sl, :])
    if final:
        acc = _rms(acc, gfin_ref[...])
    out_ref[...] = acc


def _ffn(x2, g_ffn, w_up, w_down, g_final, final):
    row = pl.BlockSpec((TM, D_MODEL), lambda i: (i, 0))
    wup = w_up.astype(BF16)
    wdn = w_down.astype(BF16)
    return pl.pallas_call(
        functools.partial(_ffn_kernel, final=final),
        grid=(N_TOK // TM,),
        in_specs=[row, _const_spec((1, D_MODEL)), _const_spec(wup.shape), _const_spec(wdn.shape),
                  _const_spec((1, D_MODEL))],
        out_specs=row,
        out_shape=jax.ShapeDtypeStruct((N_TOK, D_MODEL), F32),
        compiler_params=_params("parallel"),
        name="ffn",
    )(x2, g_ffn.reshape(1, -1), wup, wdn, g_final.reshape(1, -1))


def kernel(x, positions, g_mix, w_in, w_gate, b_gate, rel_bias, g_q_lat, w_uq, g_kv_lat, w_ukv, b_forget,
           w_dw, b_dw, g_conv_ln, b_conv_ln, w_branch, w_o, g_ffn, w_up, w_down, g_final):
    x2 = x.reshape(N_TOK, D_MODEL)
    cos_t, sin_t = _rope_tables(positions)
    consts = _inproj_consts()
    for l in range(DEPTH):
        wmain, wup, bf = _inproj_weights(w_in[l], w_uq[l], w_ukv[l], b_forget[l])
        (h, qa, ka, va, qb, kb, vb, qc, kc, vc, u) = _inproj(
            x2, g_mix[l], wmain, wup, bf, g_q_lat[l], g_kv_lat[l], cos_t, sin_t, consts)
        oa = _band_attention(qa, ka, va, _band_table(rel_bias[l]))
        ob = _flash_attention(qb, kb, vb, CHUNK, "flash_b")
        oc = _flash_attention(qc, kc, vc, 1, "flash_c")
        od = _conv_module(u, w_dw[l], b_dw[l], g_conv_ln[l], b_conv_ln[l])
        x2 = _merge(x2, h, oa, ob, oc, od, w_gate[l], b_gate[l], w_branch[l], w_o[l])
        x2 = _ffn(x2, g_ffn[l], w_up[l], w_down[l], g_final, final=(l == DEPTH - 1))
    return x2.reshape(BATCH, SEQ, D_MODEL)
```

```python
import functools

import numpy as np
import jax
import jax.numpy as jnp
from jax import lax
from jax.experimental import pallas as pl
from jax.experimental.pallas import tpu as pltpu

D_MODEL = 1024
BATCH = 4
SEQ = 4096
DEPTH = 2
N_TOK = BATCH * SEQ
CHUNK = 64
HEAD_DIM = 64
N_HEADS = 4
NORM_EPS = 1e-6
A_LEFT_CHUNKS = 8
A_MAX_REL = 128
B_Q_LORA = 256
B_KV_LORA = 128
B_NOPE = 64
B_ROPE = 32
B_V = 64
ROPE_THETA = 10000.0
CONV_CH = 256
CONV_K = 31
BRANCH_W = 256
D_FF = 4 * D_MODEL

LANES = 128
SUBLANES = 8
HEAD_PAD = LANES
QKV_PAD = N_HEADS * HEAD_PAD
NEG = -1e30
LOG2E = 1.4426950408889634

TM = 512
TILES_PER_SEQ = SEQ // TM
T_ATT = 512
TQ_A = 256
CONV_HALO = 32
VMEM_LIMIT = 56 * 1024 * 1024

F32 = jnp.float32
BF16 = jnp.bfloat16


def _dot(a, b):
    return jnp.dot(a, b, preferred_element_type=F32)


def _dot_nt(a, b):
    return lax.dot_general(a, b, (((1,), (1,)), ((), ())), preferred_element_type=F32)


def _rms(x, g):
    return x * lax.rsqrt(jnp.mean(x * x, axis=-1, keepdims=True) + NORM_EPS) * g


def _sigmoid(x):
    return 1.0 / (1.0 + jnp.exp(-x))


def _split3(x):
    hi = x.astype(BF16).astype(F32)
    r1 = x - hi
    mid = r1.astype(BF16).astype(F32)
    return hi, mid, r1 - mid


def _const_spec(shape):
    return pl.BlockSpec(shape, lambda *_: (0,) * len(shape))


def _params(*sem):
    return pltpu.CompilerParams(dimension_semantics=sem, vmem_limit_bytes=VMEM_LIMIT)


def _rope_kernel(pos_ref, invf_ref, cos_ref, sin_ref):
    ang = pos_ref[...].astype(F32) * invf_ref[...]
    cos_ref[...] = jnp.cos(ang)
    sin_ref[...] = jnp.sin(ang)


def _rope_tables(positions):
    half = B_ROPE // 2
    inv_freq = 1.0 / (ROPE_THETA ** (jnp.arange(half, dtype=F32) / half))
    row = jnp.zeros((LANES,), F32).at[B_NOPE:B_NOPE + B_ROPE].set(jnp.tile(inv_freq, 2))
    pos = positions.reshape(N_TOK, 1)
    tm = 1024
    return pl.pallas_call(
        _rope_kernel,
        grid=(N_TOK // tm,),
        in_specs=[pl.BlockSpec((tm, 1), lambda i: (i, 0)), _const_spec((1, LANES))],
        out_specs=[pl.BlockSpec((tm, LANES), lambda i: (i, 0))] * 2,
        out_shape=[jax.ShapeDtypeStruct((N_TOK, LANES), F32)] * 2,
        compiler_params=_params("parallel"),
        name="rope_tables",
    )(pos, row.reshape(1, LANES))


COL_A, COL_B, COL_C, COL_D, COL_END = 0, 768, 1536, 2304, 2816
F_COPIES = 9
AUG_PER_HEAD = 2 * F_COPIES


def _pick3(sel, parts):
    return jnp.where(sel == 0, parts[0], jnp.where(sel == 1, parts[1], parts[2]))


def _inproj_kernel(x_ref, gmix_ref, w_ref, cos_ref, sin_ref, gq_ref, wuq_ref, gkv_ref, wukv_ref, bf_ref,
                   sel_ref, tri_ref, e_ref, oneaug_ref, onev_ref, onesblk_ref,
                   h_ref, qa_ref, ka_ref, va_ref, qb_ref, kb_ref, vb_ref,
                   qc_ref, kc_ref, vc_ref, u_ref, carry_ref):
    i = pl.program_id(0)
    h = _rms(x_ref[...], gmix_ref[...]).astype(BF16)
    h_ref[...] = h
    row = lax.broadcasted_iota(jnp.int32, (HEAD_PAD, TM), 0)
    own_rows = (row < HEAD_DIM, row >= HEAD_DIM)
    ones_blk = onesblk_ref[...]

    def pair_rows(q):
        qt = q.T
        return [jnp.where(own_rows[hd % 2], qt[(hd // 2) * HEAD_PAD:(hd // 2 + 1) * HEAD_PAD], 0.0)
                for hd in range(N_HEADS)]

    def v_rows(v):
        vt = v.T
        blocks = []
        for hd in range(N_HEADS):
            blocks += [vt[hd * HEAD_DIM:(hd + 1) * HEAD_DIM], ones_blk]
        return jnp.concatenate(blocks, axis=0).astype(BF16)

    pa = _dot(h, w_ref[:, COL_A:COL_B])
    ka_ref[...] = pa[:, 256:512].astype(BF16)
    qa_ref[...] = jnp.concatenate(pair_rows(pa[:, 0:256] * LOG2E), axis=0).astype(BF16)
    va_ref[...] = v_rows(pa[:, 512:768])

    pb = _dot(h, w_ref[:, COL_B:COL_C])
    cos = cos_ref[...]
    sin = sin_ref[...]
    cos4 = jnp.concatenate([cos] * N_HEADS, axis=-1)
    sin4 = jnp.concatenate([sin] * N_HEADS, axis=-1)
    qn = _rms(pb[:, 0:B_Q_LORA], gq_ref[...]).astype(BF16)
    qq = _dot(qn, wuq_ref[...])
    qb = qq[:, :QKV_PAD] * cos4 + qq[:, QKV_PAD:] * sin4
    qb_ref[...] = (qb * ((B_NOPE + B_ROPE) ** -0.5 * LOG2E)).T.astype(BF16)
    kvn = _rms(pb[:, B_Q_LORA:B_Q_LORA + B_KV_LORA], gkv_ref[...]).astype(BF16)
    krot = pb[:, 384:512] * cos + pb[:, 512:640] * sin
    kv = _dot(kvn, wukv_ref[...])
    kb_ref[...] = (kv[:, :QKV_PAD] + jnp.concatenate([krot] * N_HEADS, axis=-1)).astype(BF16)
    vb_ref[...] = (kv[:, QKV_PAD:] + onev_ref[...]).T.astype(BF16)

    pc = _dot(h, w_ref[:, COL_C:COL_D])

    @pl.when(i % TILES_PER_SEQ == 0)
    def _():
        carry_ref[...] = jnp.zeros_like(carry_ref)

    piece_a = sel_ref[0:1, :]
    piece_b = sel_ref[1:2, :]
    z = pb[:, 640:768] + bf_ref[...]
    logf = jnp.minimum(z, 0.0) - jnp.log(1.0 + jnp.exp(-jnp.abs(z)))
    part = _dot(tri_ref[...], _pick3(piece_a, _split3(logf)).astype(BF16))
    g = part * LOG2E + carry_ref[...]
    last = g[TM - SUBLANES:TM, :]
    total = last + pltpu.roll(last, LANES - 3 * N_HEADS, 1) + pltpu.roll(last, LANES - 6 * N_HEADS, 1)
    carry_ref[...] = jnp.where(sel_ref[2:3, :] == 1, total[SUBLANES - 1:], 0.0)
    aug = _dot(_pick3(piece_b, _split3(g)).astype(BF16), e_ref[...]) + oneaug_ref[...]
    kc_ref[...] = jnp.concatenate([pc[:, 256:512], aug[:, LANES:]], axis=-1).astype(BF16)
    aug_t = aug[:, :LANES].T
    q_blocks = []
    for hd, pair in enumerate(pair_rows(pc[:, 0:256] * LOG2E)):
        mine = jnp.where(row >= hd * AUG_PER_HEAD, jnp.where(row < (hd + 1) * AUG_PER_HEAD, aug_t, 0.0), 0.0)
        q_blocks += [pair, mine]
    qc_ref[...] = jnp.concatenate(q_blocks, axis=0).astype(BF16)
    vc_ref[...] = v_rows(pc[:, 512:768])

    pd = _dot(h, w_ref[:, COL_D:COL_END])
    u_ref[...] = pd[:, 0:CONV_CH] * _sigmoid(pd[:, CONV_CH:2 * CONV_CH])


def _pad_heads(w, width):
    k = w.shape[0]
    w = w.reshape(k, N_HEADS, width)
    return jnp.pad(w, ((0, 0), (0, 0), (0, HEAD_PAD - width))).reshape(k, QKV_PAD)


def _rot_half_cols(w):
    half = B_ROPE // 2
    return jnp.concatenate([-w[..., half:], w[..., :half]], axis=-1)


def _inproj_consts():
    tri = np.tril(np.ones((TM, TM), np.float32))
    sel = np.full((SUBLANES, LANES), 2, np.int32)
    e = np.zeros((LANES, 2 * LANES), np.float32)
    oneaug = np.zeros((1, 2 * LANES), np.float32)
    for copy in range(F_COPIES):
        for hd in range(N_HEADS):
            lane = copy * N_HEADS + hd
            sel[0, lane], sel[1, lane] = divmod(copy, 3)
            e[lane, hd * AUG_PER_HEAD + copy] = 1.0
            oneaug[0, LANES + hd * AUG_PER_HEAD + copy] = 1.0
            e[lane, LANES + hd * AUG_PER_HEAD + F_COPIES + copy] = -1.0
            oneaug[0, hd * AUG_PER_HEAD + F_COPIES + copy] = 1.0
    sel[2, :] = 0
    sel[2, :3 * N_HEADS] = 1
    onev = np.zeros((1, QKV_PAD), np.float32)
    onev[0, HEAD_DIM::HEAD_PAD] = 1.0
    ones_blk = np.zeros((HEAD_DIM, TM), np.float32)
    ones_blk[0] = 1.0
    return (jnp.asarray(sel), jnp.asarray(tri, BF16), jnp.asarray(e, BF16), jnp.asarray(oneaug),
            jnp.asarray(onev), jnp.asarray(ones_blk))


def _inproj_weights(w_in, w_uq, w_ukv, b_forget):
    o = 0
    cols = {}
    for name, width in (("aq", 256), ("ak", 256), ("av", 256), ("bq", B_Q_LORA), ("bkv", B_KV_LORA),
                        ("bkr", B_ROPE), ("cq", 256), ("ck", 256), ("cv", 256), ("cf", N_HEADS),
                        ("dv", CONV_CH), ("dg", CONV_CH)):
        cols[name] = w_in[:, o:o + width]
        o += width
    scale = HEAD_DIM ** -0.5

    def rope_block(w):
        return jnp.pad(w, ((0, 0), (B_NOPE, HEAD_PAD - B_NOPE - B_ROPE)))

    cf = jnp.pad(jnp.tile(cols["cf"], (1, F_COPIES)), ((0, 0), (0, LANES - F_COPIES * N_HEADS)))
    w = jnp.concatenate([cols["aq"] * scale, cols["ak"], cols["av"],
                         cols["bq"], cols["bkv"], rope_block(cols["bkr"]), rope_block(_rot_half_cols(cols["bkr"])), cf,
                         cols["cq"] * scale, cols["ck"], cols["cv"],
                         cols["dv"], cols["dg"]], axis=1).astype(BF16)

    uq = w_uq.reshape(B_Q_LORA, N_HEADS, B_NOPE + B_ROPE)
    uq_rot = jnp.concatenate([jnp.zeros_like(uq[..., :B_NOPE]), _rot_half_cols(uq[..., B_NOPE:])], axis=-1)
    wuq = jnp.concatenate([_pad_heads(uq.reshape(B_Q_LORA, -1), B_NOPE + B_ROPE),
                           _pad_heads(uq_rot.reshape(B_Q_LORA, -1), B_NOPE + B_ROPE)], axis=1).astype(BF16)
    ukv = w_ukv.reshape(B_KV_LORA, N_HEADS, B_NOPE + B_V)
    wukv = jnp.concatenate([_pad_heads(ukv[..., :B_NOPE].reshape(B_KV_LORA, -1), B_NOPE),
                            _pad_heads(ukv[..., B_NOPE:].reshape(B_KV_LORA, -1), B_V)], axis=1).astype(BF16)
    bf = jnp.pad(jnp.tile(b_forget, F_COPIES), (0, LANES - F_COPIES * N_HEADS)).reshape(1, LANES)
    return w, wuq, wukv, bf


def _inproj(x2, g_mix, w, wuq, wukv, bf, g_q_lat, g_kv_lat, cos_t, sin_t, consts):
    assert w.shape == (D_MODEL, COL_END)
    row = lambda w: pl.BlockSpec((TM, w), lambda i: (i, 0))
    ins = [
        (x2, row(D_MODEL)), (g_mix.reshape(1, -1), None), (w, None), (cos_t, row(LANES)), (sin_t, row(LANES)),
        (g_q_lat.reshape(1, -1), None), (wuq, None), (g_kv_lat.reshape(1, -1), None), (wukv, None), (bf, None),
    ] + [(c, None) for c in consts]
    args = [a for a, _ in ins]
    specs = [s if s is not None else _const_spec(a.shape) for a, s in ins]

    def rows(w, dt):
        return jax.ShapeDtypeStruct((N_TOK, w), dt), row(w)

    def slab(nrows):
        return (jax.ShapeDtypeStruct((N_TOK // TM, nrows, TM), BF16),
                pl.BlockSpec((None, nrows, TM), lambda i: (i, 0, 0)))

    outs = [rows(D_MODEL, BF16), slab(QKV_PAD), rows(256, BF16), slab(QKV_PAD),
            slab(QKV_PAD), rows(QKV_PAD, BF16), slab(QKV_PAD),
            slab(2 * QKV_PAD), rows(3 * HEAD_PAD, BF16), slab(QKV_PAD), rows(CONV_CH, F32)]
    return pl.pallas_call(
        _inproj_kernel,
        grid=(N_TOK // TM,),
        in_specs=specs,
        out_specs=[s for _, s in outs],
        out_shape=[a for a, _ in outs],
        scratch_shapes=[pltpu.VMEM((1, LANES), F32)],
        compiler_params=_params("arbitrary"),
        name="inproj",
    )(*args)


def _band_kernel(qt_ref, k0_ref, k1_ref, k2_ref, vt0_ref, vt1_ref, vt2_ref, tbl_ref, o_ref):
    i = pl.program_id(1)
    k_refs = (k0_ref, k1_ref, k2_ref)
    vt_refs = (vt0_ref, vt1_ref, vt2_ref)

    def scores(hd):
        pair = slice((hd // 2) * HEAD_PAD, (hd // 2 + 1) * HEAD_PAD)
        qh = qt_ref[hd * HEAD_PAD:(hd + 1) * HEAD_PAD, :]
        s = []
        for j in range(3):
            sj = _dot(k_refs[j][:, pair], qh) + tbl_ref[j, hd]
            if j < 2:
                sj = jnp.where(i >= 2 - j, sj, NEG)
            s.append(sj)
        return s

    def softmax(s):
        m = jnp.maximum(jnp.maximum(s[0].max(0, keepdims=True), s[1].max(0, keepdims=True)),
                        s[2].max(0, keepdims=True))
        return [jnp.exp2(sj - m).astype(BF16) for sj in s]

    def weighted_sum(hd, p):
        rows = slice(hd * HEAD_PAD, (hd + 1) * HEAD_PAD)
        acc = _dot(vt_refs[0][rows, :], p[0]) + _dot(vt_refs[1][rows, :], p[1]) + _dot(vt_refs[2][rows, :], p[2])
        return acc[:HEAD_DIM, :] / acc[HEAD_DIM:HEAD_DIM + 1, :]

    s, p, outs = {}, {}, {}
    for tick in range(N_HEADS + 2):
        if tick < N_HEADS:
            s[tick] = scores(tick)
        if 0 <= tick - 1 < N_HEADS:
            p[tick - 1] = softmax(s.pop(tick - 1))
        if 0 <= tick - 2 < N_HEADS:
            outs[tick - 2] = weighted_sum(tick - 2, p.pop(tick - 2))
    o_ref[...] = jnp.concatenate([outs[hd] for hd in range(N_HEADS)], axis=0).T.astype(BF16)


def _band_table(rel_bias):
    w = TQ_A
    r = np.arange(w)[:, None]
    c = np.arange(w)[None, :]
    tabs = []
    for j in range(3):
        rel = (2 - j) * w - (np.arange(2 * w - 1) - (w - 1))
        idx = np.clip(rel, -A_MAX_REL, A_MAX_REL) + A_MAX_REL
        diag = jnp.pad(rel_bias[:, idx], ((0, 0), (0, 1)))
        skew = jnp.tile(diag, (1, w))[:, :w * (2 * w - 1)].reshape(N_HEADS, w, 2 * w - 1)
        bias = skew[:, :, w - 1:]
        cd = (2 - j) * (w // CHUNK) + r // CHUNK - c // CHUNK
        valid = (cd >= 0) & (cd <= A_LEFT_CHUNKS)
        tabs.append(jnp.where(valid[None], bias * LOG2E, NEG))
    return jnp.swapaxes(jnp.stack(tabs), 2, 3).astype(F32)


def _band_attention(qat, ka, vat, tbl):
    nt = SEQ // TQ_A
    per_slab = TM // TQ_A

    def block(b, i, j):
        return b * nt + jnp.maximum(i - 2 + j, 0)

    def slab_spec(j):
        return pl.BlockSpec((None, QKV_PAD, TQ_A),
                            lambda b, i: (block(b, i, j) // per_slab, 0, block(b, i, j) % per_slab))

    def k_spec(j):
        return pl.BlockSpec((TQ_A, 256), lambda b, i: (block(b, i, j), 0))

    return pl.pallas_call(
        _band_kernel,
        grid=(BATCH, nt),
        in_specs=([slab_spec(2)] + [k_spec(j) for j in range(3)] + [slab_spec(j) for j in range(3)]
                  + [_const_spec(tbl.shape)]),
        out_specs=pl.BlockSpec((TQ_A, 256), lambda b, i: (b * nt + i, 0)),
        out_shape=jax.ShapeDtypeStruct((N_TOK, 256), BF16),
        compiler_params=_params("parallel", "parallel"),
        name="band_attention",
    )(qat, ka, ka, ka, vat, vat, vat, tbl)


def _flash_kernel(qt_ref, k_ref, vt_ref, o_ref, *, unit, pair_layout):
    i = pl.program_id(1)
    t = T_ATT
    shift = unit.bit_length() - 1
    key = lax.broadcasted_iota(jnp.int32, (t, t), 0) >> shift
    qry = lax.broadcasted_iota(jnp.int32, (t, t), 1) >> shift
    visible = key <= qry
    qrows = 2 * HEAD_PAD if pair_layout else HEAD_PAD

    def k_tile(k_ref, off, hd):
        if not pair_layout:
            return k_ref[pl.ds(off, t), hd * HEAD_PAD:(hd + 1) * HEAD_PAD]
        pair = hd // 2
        return jnp.concatenate([k_ref[pl.ds(off, t), pair * HEAD_PAD:(pair + 1) * HEAD_PAD],
                                k_ref[pl.ds(off, t), 2 * HEAD_PAD:3 * HEAD_PAD]], axis=1)

    def step(j, carry, diag):
        off = pl.multiple_of(j * t, t)
        heads = [slice(hd * HEAD_PAD, (hd + 1) * HEAD_PAD) for hd in range(N_HEADS)]

        def scores(hd):
            s = _dot(k_tile(k_ref, off, hd), qt_ref[hd * qrows:(hd + 1) * qrows, :])
            return jnp.where(visible, s, NEG) if diag else s

        def softmax(hd, s):
            m_new = jnp.maximum(carry[hd][0], s.max(0, keepdims=True))
            return m_new, jnp.exp2(carry[hd][0] - m_new), jnp.exp2(s - m_new).astype(BF16)

        def update(hd, m_new, alpha, p):
            return m_new, alpha * carry[hd][1] + _dot(vt_ref[j, heads[hd], :], p)

        s, sm, new = {}, {}, {}
        for tick in range(N_HEADS + 2):
            if tick < N_HEADS:
                s[tick] = scores(tick)
            if 0 <= tick - 1 < N_HEADS:
                sm[tick - 1] = softmax(tick - 1, s.pop(tick - 1))
            if 0 <= tick - 2 < N_HEADS:
                new[tick - 2] = update(tick - 2, *sm.pop(tick - 2))
        return tuple(new[hd] for hd in range(N_HEADS))

    init = tuple((jnp.full((1, t), NEG, F32), jnp.zeros((HEAD_PAD, t), F32)) for _ in range(N_HEADS))
    carry = lax.fori_loop(0, i, lambda j, c: step(j, c, False), init)
    carry = step(i, carry, True)
    outs = [acc[:HEAD_DIM, :] / acc[HEAD_DIM:HEAD_DIM + 1, :] for _, acc in carry]
    o_ref[...] = jnp.concatenate(outs, axis=0).T.astype(BF16)


def _flash_attention(qt, k, vt, unit, name):
    nt = SEQ // T_ATT
    pair_layout = k.shape[1] == 3 * HEAD_PAD
    return pl.pallas_call(
        functools.partial(_flash_kernel, unit=unit, pair_layout=pair_layout),
        grid=(BATCH, nt),
        in_specs=[pl.BlockSpec((None, qt.shape[1], T_ATT), lambda b, i: (b * nt + i, 0, 0)),
                  pl.BlockSpec((SEQ, k.shape[1]), lambda b, i: (b, 0)),
                  pl.BlockSpec((nt, QKV_PAD, T_ATT), lambda b, i: (b, 0, 0))],
        out_specs=pl.BlockSpec((T_ATT, 256), lambda b, i: (b * nt + i, 0)),
        out_shape=jax.ShapeDtypeStruct((N_TOK, 256), BF16),
        compiler_params=_params("parallel", "parallel"),
        name=name,
    )(qt, k, vt)


def _conv_kernel(cur_ref, prev_ref, w_ref, b_ref, g_ref, beta_ref, o_ref, ext_ref, shift_ref):
    i = pl.program_id(1)
    ext_ref[0:CONV_HALO, :] = jnp.where(i > 0, prev_ref[...], 0.0)
    ext_ref[CONV_HALO:CONV_HALO + TM, :] = cur_ref[...]
    acc = jnp.zeros((TM, CONV_CH), F32) + b_ref[...]
    first = CONV_HALO - (CONV_K - 1)
    for res in range(SUBLANES):
        taps = [kk for kk in range(CONV_K) if (first + kk) % SUBLANES == res]
        span = max((first + kk) // SUBLANES for kk in taps) * SUBLANES + TM
        shift_ref[0:span, :] = ext_ref[res:res + span, :]
        for kk in taps:
            start = (first + kk) // SUBLANES * SUBLANES
            acc = acc + shift_ref[start:start + TM, :] * w_ref[kk:kk + 1, :]
    mu = jnp.mean(acc, axis=-1, keepdims=True)
    xc = acc - mu
    var = jnp.mean(xc * xc, axis=-1, keepdims=True)
    y = xc * lax.rsqrt(var + NORM_EPS) * g_ref[...] + beta_ref[...]
    o_ref[...] = (y * _sigmoid(y)).astype(BF16)


def _conv_module(u, w_dw, b_dw, g_ln, b_ln):
    halo_per_tile = TM // CONV_HALO
    w = jnp.pad(w_dw, ((0, 1), (0, 0)))
    return pl.pallas_call(
        _conv_kernel,
        grid=(BATCH, TILES_PER_SEQ),
        in_specs=[pl.BlockSpec((TM, CONV_CH), lambda b, i: (b * TILES_PER_SEQ + i, 0)),
                  pl.BlockSpec((CONV_HALO, CONV_CH),
                               lambda b, i: (jnp.maximum((b * TILES_PER_SEQ + i) * halo_per_tile - 1, 0), 0)),
                  _const_spec(w.shape)] + [_const_spec((1, CONV_CH))] * 3,
        out_specs=pl.BlockSpec((TM, CONV_CH), lambda b, i: (b * TILES_PER_SEQ + i, 0)),
        out_shape=jax.ShapeDtypeStruct((N_TOK, CONV_CH), BF16),
        scratch_shapes=[pltpu.VMEM((CONV_HALO + TM, CONV_CH), F32)] * 2,
        compiler_params=_params("parallel", "parallel"),
        name="conv_module",
    )(u, u, w, b_dw.reshape(1, -1), g_ln.reshape(1, -1), b_ln.reshape(1, -1))


def _merge_kernel(x_ref, h_ref, oa_ref, ob_ref, oc_ref, od_ref, wg_ref, bg_ref, wbr_ref, wo_ref, out_ref):
    h = h_ref[...]
    acc = jnp.zeros((TM, D_MODEL), F32)
    for n, o_ref in enumerate((oa_ref, ob_ref, oc_ref, od_ref)):
        sl = slice(n * D_MODEL, (n + 1) * D_MODEL)
        gate = _sigmoid(_dot(h, wg_ref[:, sl]) + bg_ref[:, sl])
        acc = acc + gate * _dot(o_ref[...], wbr_ref[n])
    out_ref[...] = x_ref[...] + _dot(acc.astype(BF16), wo_ref[...])


def _merge(x2, h, oa, ob, oc, od, w_gate, b_gate, w_branch, w_o):
    row = lambda w: pl.BlockSpec((TM, w), lambda i: (i, 0))
    wg = w_gate.astype(BF16)
    wbr = w_branch.astype(BF16)
    wo = w_o.astype(BF16)
    bg = b_gate.reshape(1, -1)
    return pl.pallas_call(
        _merge_kernel,
        grid=(N_TOK // TM,),
        in_specs=[row(D_MODEL), row(D_MODEL)] + [row(BRANCH_W)] * 4
                 + [_const_spec(wg.shape), _const_spec(bg.shape), _const_spec(wbr.shape), _const_spec(wo.shape)],
        out_specs=row(D_MODEL),
        out_shape=jax.ShapeDtypeStruct((N_TOK, D_MODEL), F32),
        compiler_params=_params("parallel"),
        name="merge",
    )(x2, h, oa, ob, oc, od, wg, bg, wbr, wo)


def _ffn_kernel(x_ref, g_ref, wup_ref, wdn_ref, gfin_ref, out_ref, *, final):
    x = x_ref[...]
    hf = _rms(x, g_ref[...]).astype(BF16)
    acc = x
    for c in range(D_FF // D_MODEL):
        sl = slice(c * D_MODEL, (c + 1) * D_MODEL)
        up = jnp.maximum(_dot(hf, wup_ref[:, sl]), 0.0)
        acc = acc + _dot((up * up).astype(BF16), wdn_ref[sl, :])
    if final:
        acc = _rms(acc, gfin_ref[...])
    out_ref[...] = acc


def _ffn(x2, g_ffn, w_up, w_down, g_final, final):
    row = pl.BlockSpec((TM, D_MODEL), lambda i: (i, 0))
    wup = w_up.astype(BF16)
    wdn = w_down.astype(BF16)
    return pl.pallas_call(
        functools.partial(_ffn_kernel, final=final),
        grid=(N_TOK // TM,),
        in_specs=[row, _const_spec((1, D_MODEL)), _const_spec(wup.shape), _const_spec(wdn.shape),
                  _const_spec((1, D_MODEL))],
        out_specs=row,
        out_shape=jax.ShapeDtypeStruct((N_TOK, D_MODEL), F32),
        compiler_params=_params("parallel"),
        name="ffn",
    )(x2, g_ffn.reshape(1, -1), wup, wdn, g_final.reshape(1, -1))


def kernel(x, positions, g_mix, w_in, w_gate, b_gate, rel_bias, g_q_lat, w_uq, g_kv_lat, w_ukv, b_forget,
           w_dw, b_dw, g_conv_ln, b_conv_ln, w_branch, w_o, g_ffn, w_up, w_down, g_final):
    x2 = x.reshape(N_TOK, D_MODEL)
    cos_t, sin_t = _rope_tables(positions)
    consts = _inproj_consts()
    for l in range(DEPTH):
        w, wuq, wukv, bf = _inproj_weights(w_in[l], w_uq[l], w_ukv[l], b_forget[l])
        (h, qa, ka, va, qb, kb, vb, qc, kc, vc, u) = _inproj(
            x2, g_mix[l], w, wuq, wukv, bf, g_q_lat[l], g_kv_lat[l], cos_t, sin_t, consts)
        oa = _band_attention(qa, ka, va, _band_table(rel_bias[l]))
        ob = _flash_attention(qb, kb, vb, CHUNK, "flash_b")
        oc = _flash_attention(qc, kc, vc, 1, "flash_c")
        od = _conv_module(u, w_dw[l], b_dw[l], g_conv_ln[l], b_conv_ln[l])
        x2 = _merge(x2, h, oa, ob, oc, od, w_gate[l], b_gate[l], w_branch[l], w_o[l])
        x2 = _ffn(x2, g_ffn[l], w_up[l], w_down[l], g_final, final=(l == DEPTH - 1))
    return x2.reshape(BATCH, SEQ, D_MODEL)
```

```python
import functools

import numpy as np
import jax
import jax.numpy as jnp
from jax import lax
from jax.experimental import pallas as pl
from jax.experimental.pallas import tpu as pltpu

D_MODEL = 1024
BATCH = 4
SEQ = 4096
DEPTH = 2
N_TOK = BATCH * SEQ
CHUNK = 64
HEAD_DIM = 64
N_HEADS = 4
NORM_EPS = 1e-6
A_LEFT_CHUNKS = 8
A_MAX_REL = 128
B_Q_LORA = 256
B_KV_LORA = 128
B_NOPE = 64
B_ROPE = 32
B_V = 64
ROPE_THETA = 10000.0
CONV_CH = 256
CONV_K = 31
BRANCH_W = 256
D_FF = 4 * D_MODEL

LANES = 128
SUBLANES = 8
HEAD_PAD = LANES
QKV_PAD = N_HEADS * HEAD_PAD
NEG = -1e30
LOG2E = 1.4426950408889634

TM = 512
TILES_PER_SEQ = SEQ // TM
T_ATT = 512
TQ_A = 256
CONV_HALO = 32
VMEM_LIMIT = 56 * 1024 * 1024

F32 = jnp.float32
BF16 = jnp.bfloat16


def _dot(a, b):
    return jnp.dot(a, b, preferred_element_type=F32)


def _dot_nt(a, b):
    return lax.dot_general(a, b, (((1,), (1,)), ((), ())), preferred_element_type=F32)


def _rms(x, g):
    return x * lax.rsqrt(jnp.mean(x * x, axis=-1, keepdims=True) + NORM_EPS) * g


def _sigmoid(x):
    return 1.0 / (1.0 + jnp.exp(-x))


def _split3(x):
    hi = x.astype(BF16).astype(F32)
    r1 = x - hi
    mid = r1.astype(BF16).astype(F32)
    return hi, mid, r1 - mid


def _const_spec(shape):
    return pl.BlockSpec(shape, lambda *_: (0,) * len(shape))


def _layer_spec(arr, l):
    nd = arr.ndim - 1
    return pl.BlockSpec((None,) + tuple(arr.shape[1:]), lambda *_: (l,) + (0,) * nd)


def _row(p):
    return p.reshape(p.shape[0], 1, p.shape[1])


def _params(*sem):
    return pltpu.CompilerParams(dimension_semantics=sem, vmem_limit_bytes=VMEM_LIMIT)


def _rope_kernel(pos_ref, invf_ref, cos_ref, sin_ref):
    ang = pos_ref[...].astype(F32) * invf_ref[...]
    cos_ref[...] = jnp.cos(ang)
    sin_ref[...] = jnp.sin(ang)


def _rope_tables(positions):
    half = B_ROPE // 2
    inv_freq = 1.0 / (ROPE_THETA ** (jnp.arange(half, dtype=F32) / half))
    row = jnp.zeros((LANES,), F32).at[B_NOPE:B_NOPE + B_ROPE].set(jnp.tile(inv_freq, 2))
    pos = positions.reshape(N_TOK, 1)
    tm = 1024
    return pl.pallas_call(
        _rope_kernel,
        grid=(N_TOK // tm,),
        in_specs=[pl.BlockSpec((tm, 1), lambda i: (i, 0)), _const_spec((1, LANES))],
        out_specs=[pl.BlockSpec((tm, LANES), lambda i: (i, 0))] * 2,
        out_shape=[jax.ShapeDtypeStruct((N_TOK, LANES), F32)] * 2,
        compiler_params=_params("parallel"),
        name="rope_tables",
    )(pos, row.reshape(1, LANES))


COL_A, COL_B, COL_C, COL_D, COL_END = 0, 768, 1536, 2304, 2816
F_COPIES = 9
AUG_PER_HEAD = 2 * F_COPIES


def _pick3(sel, parts):
    return jnp.where(sel == 0, parts[0], jnp.where(sel == 1, parts[1], parts[2]))


def _inproj_kernel(x_ref, gmix_ref, w_ref, cos_ref, sin_ref, gq_ref, wuq_ref, gkv_ref, wukv_ref, bf_ref,
                   sel_ref, tri_ref, e_ref, oneaug_ref, onev_ref, onesblk_ref,
                   h_ref, qa_ref, ka_ref, va_ref, qb_ref, kb_ref, vb_ref,
                   qc_ref, kc_ref, vc_ref, u_ref, carry_ref):
    i = pl.program_id(0)
    h = _rms(x_ref[...], gmix_ref[...]).astype(BF16)
    h_ref[...] = h
    row = lax.broadcasted_iota(jnp.int32, (HEAD_PAD, TM), 0)
    own_rows = (row < HEAD_DIM, row >= HEAD_DIM)
    ones_blk = onesblk_ref[...]

    def pair_rows(q):
        qt = q.T
        return [jnp.where(own_rows[hd % 2], qt[(hd // 2) * HEAD_PAD:(hd // 2 + 1) * HEAD_PAD], 0.0)
                for hd in range(N_HEADS)]

    def v_rows(v):
        vt = v.T
        blocks = []
        for hd in range(N_HEADS):
            blocks += [vt[hd * HEAD_DIM:(hd + 1) * HEAD_DIM], ones_blk]
        return jnp.concatenate(blocks, axis=0).astype(BF16)

    pa = _dot(h, w_ref[:, COL_A:COL_B])
    ka_ref[...] = pa[:, 256:512].astype(BF16)
    qa_ref[...] = jnp.concatenate(pair_rows(pa[:, 0:256] * LOG2E), axis=0).astype(BF16)
    va_ref[...] = v_rows(pa[:, 512:768])

    pb = _dot(h, w_ref[:, COL_B:COL_C])
    cos = cos_ref[...]
    sin = sin_ref[...]
    cos4 = jnp.concatenate([cos] * N_HEADS, axis=-1)
    sin4 = jnp.concatenate([sin] * N_HEADS, axis=-1)
    qn = _rms(pb[:, 0:B_Q_LORA], gq_ref[...]).astype(BF16)
    qq = _dot(qn, wuq_ref[...])
    qb = qq[:, :QKV_PAD] * cos4 + qq[:, QKV_PAD:] * sin4
    qb_ref[...] = (qb * ((B_NOPE + B_ROPE) ** -0.5 * LOG2E)).T.astype(BF16)
    kvn = _rms(pb[:, B_Q_LORA:B_Q_LORA + B_KV_LORA], gkv_ref[...]).astype(BF16)
    krot = pb[:, 384:512] * cos + pb[:, 512:640] * sin
    kv = _dot(kvn, wukv_ref[...])
    kb_ref[...] = (kv[:, :QKV_PAD] + jnp.concatenate([krot] * N_HEADS, axis=-1)).astype(BF16)
    vb_ref[...] = (kv[:, QKV_PAD:] + onev_ref[...]).T.astype(BF16)

    pc = _dot(h, w_ref[:, COL_C:COL_D])

    @pl.when(i % TILES_PER_SEQ == 0)
    def _():
        carry_ref[...] = jnp.zeros_like(carry_ref)

    piece_a = sel_ref[0:1, :]
    piece_b = sel_ref[1:2, :]
    z = pb[:, 640:768] + bf_ref[...]
    logf = jnp.minimum(z, 0.0) - jnp.log(1.0 + jnp.exp(-jnp.abs(z)))
    part = _dot(tri_ref[...], _pick3(piece_a, _split3(logf)).astype(BF16))
    g = part * LOG2E + carry_ref[...]
    last = g[TM - SUBLANES:TM, :]
    total = last + pltpu.roll(last, LANES - 3 * N_HEADS, 1) + pltpu.roll(last, LANES - 6 * N_HEADS, 1)
    carry_ref[...] = jnp.where(sel_ref[2:3, :] == 1, total[SUBLANES - 1:], 0.0)
    aug = _dot(_pick3(piece_b, _split3(g)).astype(BF16), e_ref[...]) + oneaug_ref[...]
    kc_ref[...] = jnp.concatenate([pc[:, 256:512], aug[:, LANES:]], axis=-1).astype(BF16)
    aug_t = aug[:, :LANES].T
    q_blocks = []
    for hd, pair in enumerate(pair_rows(pc[:, 0:256] * LOG2E)):
        mine = jnp.where(row >= hd * AUG_PER_HEAD, jnp.where(row < (hd + 1) * AUG_PER_HEAD, aug_t, 0.0), 0.0)
        q_blocks += [pair, mine]
    qc_ref[...] = jnp.concatenate(q_blocks, axis=0).astype(BF16)
    vc_ref[...] = v_rows(pc[:, 512:768])

    pd = _dot(h, w_ref[:, COL_D:COL_END])
    u_ref[...] = pd[:, 0:CONV_CH] * _sigmoid(pd[:, CONV_CH:2 * CONV_CH])


def _pad_last(w, before, after):
    return jnp.pad(w, ((0, 0),) * (w.ndim - 1) + ((before, after),))


def _pad_heads(w):
    return _pad_last(w, 0, HEAD_PAD - w.shape[-1]).reshape(w.shape[:-2] + (QKV_PAD,))


def _rot_half_cols(w):
    half = B_ROPE // 2
    return jnp.concatenate([-w[..., half:], w[..., :half]], axis=-1)


def _inproj_consts():
    tri = np.tril(np.ones((TM, TM), np.float32))
    sel = np.full((SUBLANES, LANES), 2, np.int32)
    e = np.zeros((LANES, 2 * LANES), np.float32)
    oneaug = np.zeros((1, 2 * LANES), np.float32)
    for copy in range(F_COPIES):
        for hd in range(N_HEADS):
            lane = copy * N_HEADS + hd
            sel[0, lane], sel[1, lane] = divmod(copy, 3)
            e[lane, hd * AUG_PER_HEAD + copy] = 1.0
            oneaug[0, LANES + hd * AUG_PER_HEAD + copy] = 1.0
            e[lane, LANES + hd * AUG_PER_HEAD + F_COPIES + copy] = -1.0
            oneaug[0, hd * AUG_PER_HEAD + F_COPIES + copy] = 1.0
    sel[2, :] = 0
    sel[2, :3 * N_HEADS] = 1
    onev = np.zeros((1, QKV_PAD), np.float32)
    onev[0, HEAD_DIM::HEAD_PAD] = 1.0
    ones_blk = np.zeros((HEAD_DIM, TM), np.float32)
    ones_blk[0] = 1.0
    return (jnp.asarray(sel), jnp.asarray(tri, BF16), jnp.asarray(e, BF16), jnp.asarray(oneaug),
            jnp.asarray(onev), jnp.asarray(ones_blk))


def _inproj_weights(w_in, w_uq, w_ukv, b_forget):
    o = 0
    cols = {}
    for name, width in (("aq", 256), ("ak", 256), ("av", 256), ("bq", B_Q_LORA), ("bkv", B_KV_LORA),
                        ("bkr", B_ROPE), ("cq", 256), ("ck", 256), ("cv", 256), ("cf", N_HEADS),
                        ("dv", CONV_CH), ("dg", CONV_CH)):
        cols[name] = w_in[..., o:o + width]
        o += width
    scale = HEAD_DIM ** -0.5

    def rope_block(w):
        return _pad_last(w, B_NOPE, HEAD_PAD - B_NOPE - B_ROPE)

    cf = _pad_last(jnp.tile(cols["cf"], (1, 1, F_COPIES)), 0, LANES - F_COPIES * N_HEADS)
    w = jnp.concatenate([cols["aq"] * scale, cols["ak"], cols["av"],
                         cols["bq"], cols["bkv"], rope_block(cols["bkr"]), rope_block(_rot_half_cols(cols["bkr"])), cf,
                         cols["cq"] * scale, cols["ck"], cols["cv"],
                         cols["dv"], cols["dg"]], axis=-1).astype(BF16)

    uq = w_uq.reshape(DEPTH, B_Q_LORA, N_HEADS, B_NOPE + B_ROPE)
    uq_rot = jnp.concatenate([jnp.zeros_like(uq[..., :B_NOPE]), _rot_half_cols(uq[..., B_NOPE:])], axis=-1)
    wuq = jnp.concatenate([_pad_heads(uq), _pad_heads(uq_rot)], axis=-1).astype(BF16)
    ukv = w_ukv.reshape(DEPTH, B_KV_LORA, N_HEADS, B_NOPE + B_V)
    wukv = jnp.concatenate([_pad_heads(ukv[..., :B_NOPE]), _pad_heads(ukv[..., B_NOPE:])], axis=-1).astype(BF16)
    bf = _row(_pad_last(jnp.tile(b_forget, (1, F_COPIES)), 0, LANES - F_COPIES * N_HEADS))
    return w, wuq, wukv, bf


def _inproj(l, x2, g_mix, w, wuq, wukv, bf, g_q_lat, g_kv_lat, cos_t, sin_t, consts):
    assert w.shape == (DEPTH, D_MODEL, COL_END)
    row = lambda w: pl.BlockSpec((TM, w), lambda i: (i, 0))
    layer = lambda a: (a, _layer_spec(a, l))
    ins = [
        (x2, row(D_MODEL)), layer(g_mix), layer(w), (cos_t, row(LANES)), (sin_t, row(LANES)),
        layer(g_q_lat), layer(wuq), layer(g_kv_lat), layer(wukv), layer(bf),
    ] + [(c, None) for c in consts]
    args = [a for a, _ in ins]
    specs = [s if s is not None else _const_spec(a.shape) for a, s in ins]

    def rows(w, dt):
        return jax.ShapeDtypeStruct((N_TOK, w), dt), row(w)

    def slab(nrows):
        return (jax.ShapeDtypeStruct((N_TOK // TM, nrows, TM), BF16),
                pl.BlockSpec((None, nrows, TM), lambda i: (i, 0, 0)))

    outs = [rows(D_MODEL, BF16), slab(QKV_PAD), rows(256, BF16), slab(QKV_PAD),
            slab(QKV_PAD), rows(QKV_PAD, BF16), slab(QKV_PAD),
            slab(2 * QKV_PAD), rows(3 * HEAD_PAD, BF16), slab(QKV_PAD), rows(CONV_CH, F32)]
    return pl.pallas_call(
        _inproj_kernel,
        grid=(N_TOK // TM,),
        in_specs=specs,
        out_specs=[s for _, s in outs],
        out_shape=[a for a, _ in outs],
        scratch_shapes=[pltpu.VMEM((1, LANES), F32)],
        compiler_params=_params("arbitrary"),
        name="inproj",
    )(*args)


def _band_kernel(qt_ref, k0_ref, k1_ref, k2_ref, vt0_ref, vt1_ref, vt2_ref, tbl_ref, o_ref):
    i = pl.program_id(1)
    k_refs = (k0_ref, k1_ref, k2_ref)
    vt_refs = (vt0_ref, vt1_ref, vt2_ref)

    def scores(hd):
        pair = slice((hd // 2) * HEAD_PAD, (hd // 2 + 1) * HEAD_PAD)
        qh = qt_ref[hd * HEAD_PAD:(hd + 1) * HEAD_PAD, :]
        s = []
        for j in range(3):
            sj = _dot(k_refs[j][:, pair], qh) + tbl_ref[j, hd]
            if j < 2:
                sj = jnp.where(i >= 2 - j, sj, NEG)
            s.append(sj)
        return s

    def softmax(s):
        m = jnp.maximum(jnp.maximum(s[0].max(0, keepdims=True), s[1].max(0, keepdims=True)),
                        s[2].max(0, keepdims=True))
        return [jnp.exp2(sj - m).astype(BF16) for sj in s]

    def weighted_sum(hd, p):
        rows = slice(hd * HEAD_PAD, (hd + 1) * HEAD_PAD)
        acc = _dot(vt_refs[0][rows, :], p[0]) + _dot(vt_refs[1][rows, :], p[1]) + _dot(vt_refs[2][rows, :], p[2])
        return acc[:HEAD_DIM, :] / acc[HEAD_DIM:HEAD_DIM + 1, :]

    s, p, outs = {}, {}, {}
    for tick in range(N_HEADS + 2):
        if tick < N_HEADS:
            s[tick] = scores(tick)
        if 0 <= tick - 1 < N_HEADS:
            p[tick - 1] = softmax(s.pop(tick - 1))
        if 0 <= tick - 2 < N_HEADS:
            outs[tick - 2] = weighted_sum(tick - 2, p.pop(tick - 2))
    o_ref[...] = jnp.concatenate([outs[hd] for hd in range(N_HEADS)], axis=0).T.astype(BF16)


def _band_table(rel_bias):
    w = TQ_A
    key = np.arange(w)[:, None]
    qry = np.arange(w)[None, :]
    flat = rel_bias.reshape(DEPTH * N_HEADS, -1)
    tabs = []
    for j in range(3):
        rel = (2 - j) * w + (np.arange(2 * w - 1) - (w - 1))
        idx = np.clip(rel, -A_MAX_REL, A_MAX_REL) + A_MAX_REL
        diag = _pad_last(flat[:, idx], 0, 1)
        skew = jnp.tile(diag, (1, w))[:, :w * (2 * w - 1)].reshape(-1, w, 2 * w - 1)
        bias = skew[:, :, w - 1:].reshape(DEPTH, N_HEADS, w, w)
        cd = (2 - j) * (w // CHUNK) + qry // CHUNK - key // CHUNK
        valid = (cd >= 0) & (cd <= A_LEFT_CHUNKS)
        tabs.append(jnp.where(valid, bias * LOG2E, NEG))
    return jnp.stack(tabs, axis=1).astype(F32)


def _band_attention(l, qat, ka, vat, tbl):
    nt = SEQ // TQ_A
    per_slab = TM // TQ_A

    def block(b, i, j):
        return b * nt + jnp.maximum(i - 2 + j, 0)

    def slab_spec(j):
        return pl.BlockSpec((None, QKV_PAD, TQ_A),
                            lambda b, i: (block(b, i, j) // per_slab, 0, block(b, i, j) % per_slab))

    def k_spec(j):
        return pl.BlockSpec((TQ_A, 256), lambda b, i: (block(b, i, j), 0))

    return pl.pallas_call(
        _band_kernel,
        grid=(BATCH, nt),
        in_specs=([slab_spec(2)] + [k_spec(j) for j in range(3)] + [slab_spec(j) for j in range(3)]
                  + [_layer_spec(tbl, l)]),
        out_specs=pl.BlockSpec((TQ_A, 256), lambda b, i: (b * nt + i, 0)),
        out_shape=jax.ShapeDtypeStruct((N_TOK, 256), BF16),
        compiler_params=_params("parallel", "parallel"),
        name="band_attention",
    )(qat, ka, ka, ka, vat, vat, vat, tbl)


def _flash_kernel(qt_ref, k_ref, vt_ref, o_ref, *, unit, pair_layout):
    i = pl.program_id(1)
    t = T_ATT
    key = lax.broadcasted_iota(jnp.int32, (t, t), 0)
    last_visible = lax.broadcasted_iota(jnp.int32, (1, t), 1) | (unit - 1)
    visible = key <= last_visible
    qrows = 2 * HEAD_PAD if pair_layout else HEAD_PAD

    def k_tile(k_ref, off, hd):
        if not pair_layout:
            return k_ref[pl.ds(off, t), hd * HEAD_PAD:(hd + 1) * HEAD_PAD]
        pair = hd // 2
        return jnp.concatenate([k_ref[pl.ds(off, t), pair * HEAD_PAD:(pair + 1) * HEAD_PAD],
                                k_ref[pl.ds(off, t), 2 * HEAD_PAD:3 * HEAD_PAD]], axis=1)

    def step(j, carry, diag):
        off = pl.multiple_of(j * t, t)
        heads = [slice(hd * HEAD_PAD, (hd + 1) * HEAD_PAD) for hd in range(N_HEADS)]

        def scores(hd):
            s = _dot(k_tile(k_ref, off, hd), qt_ref[hd * qrows:(hd + 1) * qrows, :])
            return jnp.where(visible, s, NEG) if diag else s

        def softmax(hd, s):
            m_new = jnp.maximum(carry[hd][0], s.max(0, keepdims=True))
            return m_new, jnp.exp2(carry[hd][0] - m_new), jnp.exp2(s - m_new).astype(BF16)

        def update(hd, m_new, alpha, p):
            return m_new, alpha * carry[hd][1] + _dot(vt_ref[j, heads[hd], :], p)

        s, sm, new = {}, {}, {}
        for tick in range(N_HEADS + 2):
            if tick < N_HEADS:
                s[tick] = scores(tick)
            if 0 <= tick - 1 < N_HEADS:
                sm[tick - 1] = softmax(tick - 1, s.pop(tick - 1))
            if 0 <= tick - 2 < N_HEADS:
                new[tick - 2] = update(tick - 2, *sm.pop(tick - 2))
        return tuple(new[hd] for hd in range(N_HEADS))

    init = tuple((jnp.full((1, t), NEG, F32), jnp.zeros((HEAD_PAD, t), F32)) for _ in range(N_HEADS))
    carry = lax.fori_loop(0, i, lambda j, c: step(j, c, False), init)
    carry = step(i, carry, True)
    outs = [acc[:HEAD_DIM, :] / acc[HEAD_DIM:HEAD_DIM + 1, :] for _, acc in carry]
    o_ref[...] = jnp.concatenate(outs, axis=0).T.astype(BF16)


def _flash_attention(qt, k, vt, unit, name):
    nt = SEQ // T_ATT
    pair_layout = k.shape[1] == 3 * HEAD_PAD
    return pl.pallas_call(
        functools.partial(_flash_kernel, unit=unit, pair_layout=pair_layout),
        grid=(BATCH, nt),
        in_specs=[pl.BlockSpec((None, qt.shape[1], T_ATT), lambda b, i: (b * nt + i, 0, 0)),
                  pl.BlockSpec((SEQ, k.shape[1]), lambda b, i: (b, 0)),
                  pl.BlockSpec((nt, QKV_PAD, T_ATT), lambda b, i: (b, 0, 0))],
        out_specs=pl.BlockSpec((T_ATT, 256), lambda b, i: (b * nt + i, 0)),
        out_shape=jax.ShapeDtypeStruct((N_TOK, 256), BF16),
        compiler_params=_params("parallel", "parallel"),
        name=name,
    )(qt, k, vt)


def _conv_kernel(cur_ref, prev_ref, w_ref, b_ref, g_ref, beta_ref, o_ref, ext_ref, shift_ref):
    i = pl.program_id(1)
    ext_ref[0:CONV_HALO, :] = jnp.where(i > 0, prev_ref[...], 0.0)
    ext_ref[CONV_HALO:CONV_HALO + TM, :] = cur_ref[...]
    acc = jnp.zeros((TM, CONV_CH), F32) + b_ref[...]
    first = CONV_HALO - (CONV_K - 1)
    for res in range(SUBLANES):
        taps = [kk for kk in range(CONV_K) if (first + kk) % SUBLANES == res]
        span = max((first + kk) // SUBLANES for kk in taps) * SUBLANES + TM
        shift_ref[0:span, :] = ext_ref[res:res + span, :]
        for kk in taps:
            start = (first + kk) // SUBLANES * SUBLANES
            acc = acc + shift_ref[start:start + TM, :] * w_ref[kk:kk + 1, :]
    mu = jnp.mean(acc, axis=-1, keepdims=True)
    xc = acc - mu
    var = jnp.mean(xc * xc, axis=-1, keepdims=True)
    y = xc * lax.rsqrt(var + NORM_EPS) * g_ref[...] + beta_ref[...]
    o_ref[...] = (y * _sigmoid(y)).astype(BF16)


def _conv_module(l, u, w_dw, b_dw, g_ln, b_ln):
    halo_per_tile = TM // CONV_HALO
    return pl.pallas_call(
        _conv_kernel,
        grid=(BATCH, TILES_PER_SEQ),
        in_specs=[pl.BlockSpec((TM, CONV_CH), lambda b, i: (b * TILES_PER_SEQ + i, 0)),
                  pl.BlockSpec((CONV_HALO, CONV_CH),
                               lambda b, i: (jnp.maximum((b * TILES_PER_SEQ + i) * halo_per_tile - 1, 0), 0))]
                 + [_layer_spec(p, l) for p in (w_dw, b_dw, g_ln, b_ln)],
        out_specs=pl.BlockSpec((TM, CONV_CH), lambda b, i: (b * TILES_PER_SEQ + i, 0)),
        out_shape=jax.ShapeDtypeStruct((N_TOK, CONV_CH), BF16),
        scratch_shapes=[pltpu.VMEM((CONV_HALO + TM, CONV_CH), F32)] * 2,
        compiler_params=_params("parallel", "parallel"),
        name="conv_module",
    )(u, u, w_dw, b_dw, g_ln, b_ln)


def _merge_kernel(x_ref, h_ref, oa_ref, ob_ref, oc_ref, od_ref, wg_ref, bg_ref, wbr_ref, wo_ref, out_ref):
    h = h_ref[...]
    acc = jnp.zeros((TM, D_MODEL), F32)
    for n, o_ref in enumerate((oa_ref, ob_ref, oc_ref, od_ref)):
        sl = slice(n * D_MODEL, (n + 1) * D_MODEL)
        gate = _sigmoid(_dot(h, wg_ref[:, sl]) + bg_ref[:, sl])
        acc = acc + gate * _dot(o_ref[...], wbr_ref[n])
    out_ref[...] = x_ref[...] + _dot(acc.astype(BF16), wo_ref[...])


def _merge(l, x2, h, oa, ob, oc, od, wg, bg, wbr, wo):
    row = lambda w: pl.BlockSpec((TM, w), lambda i: (i, 0))
    return pl.pallas_call(
        _merge_kernel,
        grid=(N_TOK // TM,),
        in_specs=[row(D_MODEL), row(D_MODEL)] + [row(BRANCH_W)] * 4
                 + [_layer_spec(p, l) for p in (wg, bg, wbr, wo)],
        out_specs=row(D_MODEL),
        out_shape=jax.ShapeDtypeStruct((N_TOK, D_MODEL), F32),
        compiler_params=_params("parallel"),
        name="merge",
    )(x2, h, oa, ob, oc, od, wg, bg, wbr, wo)


def _ffn_kernel(x_ref, g_ref, wup_ref, wdn_ref, gfin_ref, out_ref, *, final):
    x = x_ref[...]
    hf = _rms(x, g_ref[...]).astype(BF16)
    acc = x
    for c in range(D_FF // D_MODEL):
        sl = slice(c * D_MODEL, (c + 1) * D_MODEL)
        up = jnp.maximum(_dot(hf, wup_ref[:, sl]), 0.0)
        acc = acc + _dot((up * up).astype(BF16), wdn_ref[sl, :])
    if final:
        acc = _rms(acc, gfin_ref[...])
    out_ref[...] = acc


def _ffn(l, x2, g_ffn, wup, wdn, g_final, final):
    row = pl.BlockSpec((TM, D_MODEL), lambda i: (i, 0))
    return pl.pallas_call(
        functools.partial(_ffn_kernel, final=final),
        grid=(N_TOK // TM,),
        in_specs=[row] + [_layer_spec(p, l) for p in (g_ffn, wup, wdn)] + [_const_spec((1, D_MODEL))],
        out_specs=row,
        out_shape=jax.ShapeDtypeStruct((N_TOK, D_MODEL), F32),
        compiler_params=_params("parallel"),
        name="ffn",
    )(x2, g_ffn, wup, wdn, g_final)


def kernel(x, positions, g_mix, w_in, w_gate, b_gate, rel_bias, g_q_lat, w_uq, g_kv_lat, w_ukv, b_forget,
           w_dw, b_dw, g_conv_ln, b_conv_ln, w_branch, w_o, g_ffn, w_up, w_down, g_final):
    x2 = x.reshape(N_TOK, D_MODEL)
    cos_t, sin_t = _rope_tables(positions)
    consts = _inproj_consts()
    w, wuq, wukv, bf = _inproj_weights(w_in, w_uq, w_ukv, b_forget)
    tbl = _band_table(rel_bias)
    w_taps = jnp.pad(w_dw, ((0, 0), (0, 1), (0, 0)))
    wg, wbr, wo, wup, wdn = (p.astype(BF16) for p in (w_gate, w_branch, w_o, w_up, w_down))
    for l in range(DEPTH):
        (h, qa, ka, va, qb, kb, vb, qc, kc, vc, u) = _inproj(
            l, x2, _row(g_mix), w, wuq, wukv, bf, _row(g_q_lat), _row(g_kv_lat), cos_t, sin_t, consts)
        oa = _band_attention(l, qa, ka, va, tbl)
        ob = _flash_attention(qb, kb, vb, CHUNK, "flash_b")
        oc = _flash_attention(qc, kc, vc, 1, "flash_c")
        od = _conv_module(l, u, w_taps, _row(b_dw), _row(g_conv_ln), _row(b_conv_ln))
        x2 = _merge(l, x2, h, oa, ob, oc, od, wg, _row(b_gate), wbr, wo)
        x2 = _ffn(l, x2, _row(g_ffn), wup, wdn, g_final.reshape(1, -1), final=(l == DEPTH - 1))
    return x2.reshape(BATCH, SEQ, D_MODEL)
```

```python
import functools

import numpy as np
import jax
import jax.numpy as jnp
from jax import lax
from jax.experimental import pallas as pl
from jax.experimental.pallas import tpu as pltpu

D_MODEL = 1024
BATCH = 4
SEQ = 4096
DEPTH = 2
N_TOK = BATCH * SEQ
CHUNK = 64
HEAD_DIM = 64
N_HEADS = 4
NORM_EPS = 1e-6
A_LEFT_CHUNKS = 8
A_MAX_REL = 128
B_Q_LORA = 256
B_KV_LORA = 128
B_NOPE = 64
B_ROPE = 32
B_V = 64
ROPE_THETA = 10000.0
CONV_CH = 256
CONV_K = 31
BRANCH_W = 256
D_FF = 4 * D_MODEL

LANES = 128
SUBLANES = 8
HEAD_PAD = LANES
QKV_PAD = N_HEADS * HEAD_PAD
NEG = -1e30
LOG2E = 1.4426950408889634

TM = 512
TILES_PER_SEQ = SEQ // TM
T_ATT = 512
TQ_A = 256
CONV_HALO = 32
VMEM_LIMIT = 56 * 1024 * 1024

F32 = jnp.float32
BF16 = jnp.bfloat16


def _dot(a, b):
    return jnp.dot(a, b, preferred_element_type=F32)


def _dot_nt(a, b):
    return lax.dot_general(a, b, (((1,), (1,)), ((), ())), preferred_element_type=F32)


def _rms(x, g):
    return x * lax.rsqrt(jnp.mean(x * x, axis=-1, keepdims=True) + NORM_EPS) * g


def _sigmoid(x):
    return 1.0 / (1.0 + jnp.exp(-x))


def _split3(x):
    hi = x.astype(BF16).astype(F32)
    r1 = x - hi
    mid = r1.astype(BF16).astype(F32)
    return hi, mid, r1 - mid


def _const_spec(shape):
    return pl.BlockSpec(shape, lambda *_: (0,) * len(shape))


def _layer_spec(arr, l):
    nd = arr.ndim - 1
    return pl.BlockSpec((None,) + tuple(arr.shape[1:]), lambda *_: (l,) + (0,) * nd)


def _row(p):
    return p.reshape(p.shape[0], 1, p.shape[1])


def _params(*sem):
    return pltpu.CompilerParams(dimension_semantics=sem, vmem_limit_bytes=VMEM_LIMIT)


def _rope_kernel(pos_ref, invf_ref, cos_ref, sin_ref):
    ang = pos_ref[...].astype(F32) * invf_ref[...]
    cos_ref[...] = jnp.cos(ang)
    sin_ref[...] = jnp.sin(ang)


def _rope_tables(positions):
    half = B_ROPE // 2
    inv_freq = 1.0 / (ROPE_THETA ** (jnp.arange(half, dtype=F32) / half))
    row = jnp.zeros((LANES,), F32).at[B_NOPE:B_NOPE + B_ROPE].set(jnp.tile(inv_freq, 2))
    pos = positions.reshape(N_TOK, 1)
    tm = 1024
    return pl.pallas_call(
        _rope_kernel,
        grid=(N_TOK // tm,),
        in_specs=[pl.BlockSpec((tm, 1), lambda i: (i, 0)), _const_spec((1, LANES))],
        out_specs=[pl.BlockSpec((tm, LANES), lambda i: (i, 0))] * 2,
        out_shape=[jax.ShapeDtypeStruct((N_TOK, LANES), F32)] * 2,
        compiler_params=_params("parallel"),
        name="rope_tables",
    )(pos, row.reshape(1, LANES))


COL_A, COL_B, COL_C, COL_D, COL_END = 0, 768, 1536, 2304, 2816
F_COPIES = 9
AUG_PER_HEAD = 2 * F_COPIES


def _pick3(sel, parts):
    return jnp.where(sel == 0, parts[0], jnp.where(sel == 1, parts[1], parts[2]))


def _inproj_kernel(x_ref, gmix_ref, w_ref, cos_ref, sin_ref, gq_ref, wuq_ref, gkv_ref, wukv_ref, bf_ref,
                   sel_ref, tri_ref, e_ref, oneaug_ref, onev_ref, onesblk_ref,
                   h_ref, qa_ref, ka_ref, va_ref, qb_ref, kb_ref, vb_ref,
                   qc_ref, kc_ref, vc_ref, u_ref, carry_ref):
    i = pl.program_id(0)
    h = _rms(x_ref[...], gmix_ref[...]).astype(BF16)
    h_ref[...] = h
    row = lax.broadcasted_iota(jnp.int32, (HEAD_PAD, TM), 0)
    own_rows = (row < HEAD_DIM, row >= HEAD_DIM)
    ones_blk = onesblk_ref[...]

    def pair_rows(q):
        qt = q.T
        return [jnp.where(own_rows[hd % 2], qt[(hd // 2) * HEAD_PAD:(hd // 2 + 1) * HEAD_PAD], 0.0)
                for hd in range(N_HEADS)]

    def v_rows(v):
        vt = v.T
        blocks = []
        for hd in range(N_HEADS):
            blocks += [vt[hd * HEAD_DIM:(hd + 1) * HEAD_DIM], ones_blk]
        return jnp.concatenate(blocks, axis=0).astype(BF16)

    pa = _dot(h, w_ref[:, COL_A:COL_B])
    ka_ref[...] = pa[:, 256:512].astype(BF16)
    qa_ref[...] = jnp.concatenate(pair_rows(pa[:, 0:256] * LOG2E), axis=0).astype(BF16)
    va_ref[...] = v_rows(pa[:, 512:768])

    pb = _dot(h, w_ref[:, COL_B:COL_C])
    cos = cos_ref[...]
    sin = sin_ref[...]
    cos4 = jnp.concatenate([cos] * N_HEADS, axis=-1)
    sin4 = jnp.concatenate([sin] * N_HEADS, axis=-1)
    qn = _rms(pb[:, 0:B_Q_LORA], gq_ref[...]).astype(BF16)
    qq = _dot(qn, wuq_ref[...])
    qb = qq[:, :QKV_PAD] * cos4 + qq[:, QKV_PAD:] * sin4
    qb_ref[...] = (qb * ((B_NOPE + B_ROPE) ** -0.5 * LOG2E)).T.astype(BF16)
    kvn = _rms(pb[:, B_Q_LORA:B_Q_LORA + B_KV_LORA], gkv_ref[...]).astype(BF16)
    krot = pb[:, 384:512] * cos + pb[:, 512:640] * sin
    kv = _dot(kvn, wukv_ref[...])
    kb_ref[...] = (kv[:, :QKV_PAD] + jnp.concatenate([krot] * N_HEADS, axis=-1)).astype(BF16)
    vb_ref[...] = (kv[:, QKV_PAD:] + onev_ref[...]).T.astype(BF16)

    pc = _dot(h, w_ref[:, COL_C:COL_D])

    @pl.when(i % TILES_PER_SEQ == 0)
    def _():
        carry_ref[...] = jnp.zeros_like(carry_ref)

    piece_a = sel_ref[0:1, :]
    piece_b = sel_ref[1:2, :]
    z = pb[:, 640:768] + bf_ref[...]
    logf = jnp.minimum(z, 0.0) - jnp.log(1.0 + jnp.exp(-jnp.abs(z)))
    part = _dot(tri_ref[...], _pick3(piece_a, _split3(logf)).astype(BF16))
    g = part * LOG2E + carry_ref[...]
    last = g[TM - SUBLANES:TM, :]
    total = last + pltpu.roll(last, LANES - 3 * N_HEADS, 1) + pltpu.roll(last, LANES - 6 * N_HEADS, 1)
    carry_ref[...] = jnp.where(sel_ref[2:3, :] == 1, total[SUBLANES - 1:], 0.0)
    aug = _dot(_pick3(piece_b, _split3(g)).astype(BF16), e_ref[...]) + oneaug_ref[...]
    kc_ref[...] = jnp.concatenate([pc[:, 256:512], aug[:, LANES:]], axis=-1).astype(BF16)
    aug_t = aug[:, :LANES].T
    q_blocks = []
    for hd, pair in enumerate(pair_rows(pc[:, 0:256] * LOG2E)):
        mine = jnp.where(row >= hd * AUG_PER_HEAD, jnp.where(row < (hd + 1) * AUG_PER_HEAD, aug_t, 0.0), 0.0)
        q_blocks += [pair, mine]
    qc_ref[...] = jnp.concatenate(q_blocks, axis=0).astype(BF16)
    vc_ref[...] = v_rows(pc[:, 512:768])

    pd = _dot(h, w_ref[:, COL_D:COL_END])
    u_ref[...] = pd[:, 0:CONV_CH] * _sigmoid(pd[:, CONV_CH:2 * CONV_CH])


def _pad_last(w, before, after):
    return jnp.pad(w, ((0, 0),) * (w.ndim - 1) + ((before, after),))


def _pad_heads(w):
    return _pad_last(w, 0, HEAD_PAD - w.shape[-1]).reshape(w.shape[:-2] + (QKV_PAD,))


def _rot_half_cols(w):
    half = B_ROPE // 2
    return jnp.concatenate([-w[..., half:], w[..., :half]], axis=-1)


def _inproj_consts():
    tri = np.tril(np.ones((TM, TM), np.float32))
    sel = np.full((SUBLANES, LANES), 2, np.int32)
    e = np.zeros((LANES, 2 * LANES), np.float32)
    oneaug = np.zeros((1, 2 * LANES), np.float32)
    for copy in range(F_COPIES):
        for hd in range(N_HEADS):
            lane = copy * N_HEADS + hd
            sel[0, lane], sel[1, lane] = divmod(copy, 3)
            e[lane, hd * AUG_PER_HEAD + copy] = 1.0
            oneaug[0, LANES + hd * AUG_PER_HEAD + copy] = 1.0
            e[lane, LANES + hd * AUG_PER_HEAD + F_COPIES + copy] = -1.0
            oneaug[0, hd * AUG_PER_HEAD + F_COPIES + copy] = 1.0
    sel[2, :] = 0
    sel[2, :3 * N_HEADS] = 1
    onev = np.zeros((1, QKV_PAD), np.float32)
    onev[0, HEAD_DIM::HEAD_PAD] = 1.0
    ones_blk = np.zeros((HEAD_DIM, TM), np.float32)
    ones_blk[0] = 1.0
    return (jnp.asarray(sel), jnp.asarray(tri, BF16), jnp.asarray(e, BF16), jnp.asarray(oneaug),
            jnp.asarray(onev), jnp.asarray(ones_blk))


def _inproj_weights(w_in, w_uq, w_ukv, b_forget):
    o = 0
    cols = {}
    for name, width in (("aq", 256), ("ak", 256), ("av", 256), ("bq", B_Q_LORA), ("bkv", B_KV_LORA),
                        ("bkr", B_ROPE), ("cq", 256), ("ck", 256), ("cv", 256), ("cf", N_HEADS),
                        ("dv", CONV_CH), ("dg", CONV_CH)):
        cols[name] = w_in[..., o:o + width]
        o += width
    scale = HEAD_DIM ** -0.5

    def rope_block(w):
        return _pad_last(w, B_NOPE, HEAD_PAD - B_NOPE - B_ROPE)

    cf = _pad_last(jnp.tile(cols["cf"], (1, 1, F_COPIES)), 0, LANES - F_COPIES * N_HEADS)
    w = jnp.concatenate([cols["aq"] * scale, cols["ak"], cols["av"],
                         cols["bq"], cols["bkv"], rope_block(cols["bkr"]), rope_block(_rot_half_cols(cols["bkr"])), cf,
                         cols["cq"] * scale, cols["ck"], cols["cv"],
                         cols["dv"], cols["dg"]], axis=-1).astype(BF16)

    uq = w_uq.reshape(DEPTH, B_Q_LORA, N_HEADS, B_NOPE + B_ROPE)
    uq_rot = jnp.concatenate([jnp.zeros_like(uq[..., :B_NOPE]), _rot_half_cols(uq[..., B_NOPE:])], axis=-1)
    wuq = jnp.concatenate([_pad_heads(uq), _pad_heads(uq_rot)], axis=-1).astype(BF16)
    ukv = w_ukv.reshape(DEPTH, B_KV_LORA, N_HEADS, B_NOPE + B_V)
    wukv = jnp.concatenate([_pad_heads(ukv[..., :B_NOPE]), _pad_heads(ukv[..., B_NOPE:])], axis=-1).astype(BF16)
    bf = _row(_pad_last(jnp.tile(b_forget, (1, F_COPIES)), 0, LANES - F_COPIES * N_HEADS))
    return w, wuq, wukv, bf


def _inproj(l, x2, g_mix, w, wuq, wukv, bf, g_q_lat, g_kv_lat, cos_t, sin_t, consts):
    assert w.shape == (DEPTH, D_MODEL, COL_END)
    row = lambda w: pl.BlockSpec((TM, w), lambda i: (i, 0))
    layer = lambda a: (a, _layer_spec(a, l))
    ins = [
        (x2, row(D_MODEL)), layer(g_mix), layer(w), (cos_t, row(LANES)), (sin_t, row(LANES)),
        layer(g_q_lat), layer(wuq), layer(g_kv_lat), layer(wukv), layer(bf),
    ] + [(c, None) for c in consts]
    args = [a for a, _ in ins]
    specs = [s if s is not None else _const_spec(a.shape) for a, s in ins]

    def rows(w, dt):
        return jax.ShapeDtypeStruct((N_TOK, w), dt), row(w)

    def slab(nrows):
        return (jax.ShapeDtypeStruct((N_TOK // TM, nrows, TM), BF16),
                pl.BlockSpec((None, nrows, TM), lambda i: (i, 0, 0)))

    outs = [rows(D_MODEL, BF16), slab(QKV_PAD), rows(256, BF16), slab(QKV_PAD),
            slab(QKV_PAD), rows(QKV_PAD, BF16), slab(QKV_PAD),
            slab(2 * QKV_PAD), rows(3 * HEAD_PAD, BF16), slab(QKV_PAD), rows(CONV_CH, F32)]
    return pl.pallas_call(
        _inproj_kernel,
        grid=(N_TOK // TM,),
        in_specs=specs,
        out_specs=[s for _, s in outs],
        out_shape=[a for a, _ in outs],
        scratch_shapes=[pltpu.VMEM((1, LANES), F32)],
        compiler_params=_params("arbitrary"),
        name="inproj",
    )(*args)


def _band_kernel(qt_ref, k0_ref, k1_ref, k2_ref, vt0_ref, vt1_ref, vt2_ref, tbl_ref, o_ref):
    i = pl.program_id(1)
    k_refs = (k0_ref, k1_ref, k2_ref)
    vt_refs = (vt0_ref, vt1_ref, vt2_ref)

    def scores(hd):
        pair = slice((hd // 2) * HEAD_PAD, (hd // 2 + 1) * HEAD_PAD)
        qh = qt_ref[hd * HEAD_PAD:(hd + 1) * HEAD_PAD, :]
        s = []
        for j in range(3):
            sj = _dot(k_refs[j][:, pair], qh) + tbl_ref[j, hd]
            if j < 2:
                sj = jnp.where(i >= 2 - j, sj, NEG)
            s.append(sj)
        return s

    def softmax(s):
        m = jnp.maximum(jnp.maximum(s[0].max(0, keepdims=True), s[1].max(0, keepdims=True)),
                        s[2].max(0, keepdims=True))
        return [jnp.exp2(sj - m).astype(BF16) for sj in s]

    def weighted_sum(hd, p):
        rows = slice(hd * HEAD_PAD, (hd + 1) * HEAD_PAD)
        acc = _dot(vt_refs[0][rows, :], p[0]) + _dot(vt_refs[1][rows, :], p[1]) + _dot(vt_refs[2][rows, :], p[2])
        return acc[:HEAD_DIM, :] / acc[HEAD_DIM:HEAD_DIM + 1, :]

    s, p, outs = {}, {}, {}
    for tick in range(N_HEADS + 2):
        if tick < N_HEADS:
            s[tick] = scores(tick)
        if 0 <= tick - 1 < N_HEADS:
            p[tick - 1] = softmax(s.pop(tick - 1))
        if 0 <= tick - 2 < N_HEADS:
            outs[tick - 2] = weighted_sum(tick - 2, p.pop(tick - 2))
    o_ref[...] = jnp.concatenate([outs[hd] for hd in range(N_HEADS)], axis=0).T.astype(BF16)


def _band_table(rel_bias):
    w = TQ_A
    key = np.arange(w)[:, None]
    qry = np.arange(w)[None, :]
    flat = rel_bias.reshape(DEPTH * N_HEADS, -1)
    tabs = []
    for j in range(3):
        rel = (2 - j) * w + (np.arange(2 * w - 1) - (w - 1))
        idx = np.clip(rel, -A_MAX_REL, A_MAX_REL) + A_MAX_REL
        diag = _pad_last(flat[:, idx], 0, 1)
        skew = jnp.tile(diag, (1, w))[:, :w * (2 * w - 1)].reshape(-1, w, 2 * w - 1)
        bias = skew[:, :, w - 1:].reshape(DEPTH, N_HEADS, w, w)
        cd = (2 - j) * (w // CHUNK) + qry // CHUNK - key // CHUNK
        valid = (cd >= 0) & (cd <= A_LEFT_CHUNKS)
        tabs.append(jnp.where(valid, bias * LOG2E, NEG))
    return jnp.stack(tabs, axis=1).astype(F32)


def _band_attention(l, qat, ka, vat, tbl):
    nt = SEQ // TQ_A
    per_slab = TM // TQ_A

    def block(b, i, j):
        return b * nt + jnp.maximum(i - 2 + j, 0)

    def slab_spec(j):
        return pl.BlockSpec((None, QKV_PAD, TQ_A),
                            lambda b, i: (block(b, i, j) // per_slab, 0, block(b, i, j) % per_slab))

    def k_spec(j):
        return pl.BlockSpec((TQ_A, 256), lambda b, i: (block(b, i, j), 0))

    return pl.pallas_call(
        _band_kernel,
        grid=(BATCH, nt),
        in_specs=([slab_spec(2)] + [k_spec(j) for j in range(3)] + [slab_spec(j) for j in range(3)]
                  + [_layer_spec(tbl, l)]),
        out_specs=pl.BlockSpec((TQ_A, 256), lambda b, i: (b * nt + i, 0)),
        out_shape=jax.ShapeDtypeStruct((N_TOK, 256), BF16),
        compiler_params=_params("parallel", "parallel"),
        name="band_attention",
    )(qat, ka, ka, ka, vat, vat, vat, tbl)


def _flash_kernel(qt_ref, k_ref, vt_ref, o_ref, *, unit, pair_layout):
    i = pl.program_id(1)
    t = T_ATT
    key = lax.broadcasted_iota(jnp.int32, (t, t), 0)
    last_visible = lax.broadcasted_iota(jnp.int32, (1, t), 1) | (unit - 1)
    visible = key <= last_visible
    qrows = 2 * HEAD_PAD if pair_layout else HEAD_PAD

    def k_tile(k_ref, off, hd):
        if not pair_layout:
            return k_ref[pl.ds(off, t), hd * HEAD_PAD:(hd + 1) * HEAD_PAD]
        pair = hd // 2
        return jnp.concatenate([k_ref[pl.ds(off, t), pair * HEAD_PAD:(pair + 1) * HEAD_PAD],
                                k_ref[pl.ds(off, t), 2 * HEAD_PAD:3 * HEAD_PAD]], axis=1)

    def step(tiles, carry):
        heads = [slice(hd * HEAD_PAD, (hd + 1) * HEAD_PAD) for hd in range(N_HEADS)]
        m = [c[0] for c in carry]
        acc = [c[1] for c in carry]

        def scores(j, diag, hd):
            s = _dot(k_tile(k_ref, pl.multiple_of(j * t, t), hd), qt_ref[hd * qrows:(hd + 1) * qrows, :])
            return jnp.where(visible, s, NEG) if diag else s

        def softmax(hd, s):
            m_new = jnp.maximum(m[hd], s.max(0, keepdims=True))
            alpha = jnp.exp2(m[hd] - m_new)
            m[hd] = m_new
            return alpha, jnp.exp2(s - m_new).astype(BF16)

        def update(j, hd, alpha, p):
            acc[hd] = alpha * acc[hd] + _dot(vt_ref[j, heads[hd], :], p)

        units = [(j, diag, hd) for j, diag in tiles for hd in range(N_HEADS)]
        s, sm = {}, {}
        for tick in range(len(units) + 2):
            if tick < len(units):
                s[tick] = scores(*units[tick])
            if 0 <= tick - 1 < len(units):
                sm[tick - 1] = softmax(units[tick - 1][2], s.pop(tick - 1))
            if 0 <= tick - 2 < len(units):
                update(units[tick - 2][0], units[tick - 2][2], *sm.pop(tick - 2))
        return tuple(zip(m, acc))

    init = tuple((jnp.full((1, t), NEG, F32), jnp.zeros((HEAD_PAD, t), F32)) for _ in range(N_HEADS))
    carry = lax.fori_loop(0, i // 2, lambda jj, c: step([(2 * jj, False), (2 * jj + 1, False)], c), init)
    carry = lax.cond(i % 2 == 1,
                     lambda c: step([(i - 1, False), (i, True)], c),
                     lambda c: step([(i, True)], c), carry)
    outs = [acc[:HEAD_DIM, :] / acc[HEAD_DIM:HEAD_DIM + 1, :] for _, acc in carry]
    o_ref[...] = jnp.concatenate(outs, axis=0).T.astype(BF16)


def _flash_attention(qt, k, vt, unit, name):
    nt = SEQ // T_ATT
    pair_layout = k.shape[1] == 3 * HEAD_PAD
    return pl.pallas_call(
        functools.partial(_flash_kernel, unit=unit, pair_layout=pair_layout),
        grid=(BATCH, nt),
        in_specs=[pl.BlockSpec((None, qt.shape[1], T_ATT), lambda b, i: (b * nt + i, 0, 0)),
                  pl.BlockSpec((SEQ, k.shape[1]), lambda b, i: (b, 0)),
                  pl.BlockSpec((nt, QKV_PAD, T_ATT), lambda b, i: (b, 0, 0))],
        out_specs=pl.BlockSpec((T_ATT, 256), lambda b, i: (b * nt + i, 0)),
        out_shape=jax.ShapeDtypeStruct((N_TOK, 256), BF16),
        compiler_params=_params("parallel", "parallel"),
        name=name,
    )(qt, k, vt)


def _conv_kernel(cur_ref, prev_ref, w_ref, b_ref, g_ref, beta_ref, o_ref, ext_ref, shift_ref):
    i = pl.program_id(1)
    ext_ref[0:CONV_HALO, :] = jnp.where(i > 0, prev_ref[...], 0.0)
    ext_ref[CONV_HALO:CONV_HALO + TM, :] = cur_ref[...]
    acc = jnp.zeros((TM, CONV_CH), F32) + b_ref[...]
    first = CONV_HALO - (CONV_K - 1)
    for res in range(SUBLANES):
        taps = [kk for kk in range(CONV_K) if (first + kk) % SUBLANES == res]
        span = max((first + kk) // SUBLANES for kk in taps) * SUBLANES + TM
        shift_ref[0:span, :] = ext_ref[res:res + span, :]
        for kk in taps:
            start = (first + kk) // SUBLANES * SUBLANES
            acc = acc + shift_ref[start:start + TM, :] * w_ref[kk:kk + 1, :]
    mu = jnp.mean(acc, axis=-1, keepdims=True)
    xc = acc - mu
    var = jnp.mean(xc * xc, axis=-1, keepdims=True)
    y = xc * lax.rsqrt(var + NORM_EPS) * g_ref[...] + beta_ref[...]
    o_ref[...] = (y * _sigmoid(y)).astype(BF16)


def _conv_module(l, u, w_dw, b_dw, g_ln, b_ln):
    halo_per_tile = TM // CONV_HALO
    return pl.pallas_call(
        _conv_kernel,
        grid=(BATCH, TILES_PER_SEQ),
        in_specs=[pl.BlockSpec((TM, CONV_CH), lambda b, i: (b * TILES_PER_SEQ + i, 0)),
                  pl.BlockSpec((CONV_HALO, CONV_CH),
                               lambda b, i: (jnp.maximum((b * TILES_PER_SEQ + i) * halo_per_tile - 1, 0), 0))]
                 + [_layer_spec(p, l) for p in (w_dw, b_dw, g_ln, b_ln)],
        out_specs=pl.BlockSpec((TM, CONV_CH), lambda b, i: (b * TILES_PER_SEQ + i, 0)),
        out_shape=jax.ShapeDtypeStruct((N_TOK, CONV_CH), BF16),
        scratch_shapes=[pltpu.VMEM((CONV_HALO + TM, CONV_CH), F32)] * 2,
        compiler_params=_params("parallel", "parallel"),
        name="conv_module",
    )(u, u, w_dw, b_dw, g_ln, b_ln)


def _merge_kernel(x_ref, h_ref, oa_ref, ob_ref, oc_ref, od_ref, wg_ref, bg_ref, wbr_ref, wo_ref, out_ref):
    h = h_ref[...]
    acc = jnp.zeros((TM, D_MODEL), F32)
    for n, o_ref in enumerate((oa_ref, ob_ref, oc_ref, od_ref)):
        sl = slice(n * D_MODEL, (n + 1) * D_MODEL)
        gate = _sigmoid(_dot(h, wg_ref[:, sl]) + bg_ref[:, sl])
        acc = acc + gate * _dot(o_ref[...], wbr_ref[n])
    out_ref[...] = x_ref[...] + _dot(acc.astype(BF16), wo_ref[...])


def _merge(l, x2, h, oa, ob, oc, od, wg, bg, wbr, wo):
    row = lambda w: pl.BlockSpec((TM, w), lambda i: (i, 0))
    return pl.pallas_call(
        _merge_kernel,
        grid=(N_TOK // TM,),
        in_specs=[row(D_MODEL), row(D_MODEL)] + [row(BRANCH_W)] * 4
                 + [_layer_spec(p, l) for p in (wg, bg, wbr, wo)],
        out_specs=row(D_MODEL),
        out_shape=jax.ShapeDtypeStruct((N_TOK, D_MODEL), F32),
        compiler_params=_params("parallel"),
        name="merge",
    )(x2, h, oa, ob, oc, od, wg, bg, wbr, wo)


def _ffn_kernel(x_ref, g_ref, wup_ref, wdn_ref, gfin_ref, out_ref, *, final):
    x = x_ref[...]
    hf = _rms(x, g_ref[...]).astype(BF16)
    acc = x
    for c in range(D_FF // D_MODEL):
        sl = slice(c * D_MODEL, (c + 1) * D_MODEL)
        up = jnp.maximum(_dot(hf, wup_ref[:, sl]), 0.0)
        acc = acc + _dot((up * up).astype(BF16), wdn_ref[sl, :])
    if final:
        acc = _rms(acc, gfin_ref[...])
    out_ref[...] = acc


def _ffn(l, x2, g_ffn, wup, wdn, g_final, final):
    row = pl.BlockSpec((TM, D_MODEL), lambda i: (i, 0))
    return pl.pallas_call(
        functools.partial(_ffn_kernel, final=final),
        grid=(N_TOK // TM,),
        in_specs=[row] + [_layer_spec(p, l) for p in (g_ffn, wup, wdn)] + [_const_spec((1, D_MODEL))],
        out_specs=row,
        out_shape=jax.ShapeDtypeStruct((N_TOK, D_MODEL), F32),
        compiler_params=_params("parallel"),
        name="ffn",
    )(x2, g_ffn, wup, wdn, g_final)


def kernel(x, positions, g_mix, w_in, w_gate, b_gate, rel_bias, g_q_lat, w_uq, g_kv_lat, w_ukv, b_forget,
           w_dw, b_dw, g_conv_ln, b_conv_ln, w_branch, w_o, g_ffn, w_up, w_down, g_final):
    x2 = x.reshape(N_TOK, D_MODEL)
    cos_t, sin_t = _rope_tables(positions)
    consts = _inproj_consts()
    w, wuq, wukv, bf = _inproj_weights(w_in, w_uq, w_ukv, b_forget)
    tbl = _band_table(rel_bias)
    w_taps = jnp.pad(w_dw, ((0, 0), (0, 1), (0, 0)))
    wg, wbr, wo, wup, wdn = (p.astype(BF16) for p in (w_gate, w_branch, w_o, w_up, w_down))
    for l in range(DEPTH):
        (h, qa, ka, va, qb, kb, vb, qc, kc, vc, u) = _inproj(
            l, x2, _row(g_mix), w, wuq, wukv, bf, _row(g_q_lat), _row(g_kv_lat), cos_t, sin_t, consts)
        oa = _band_attention(l, qa, ka, va, tbl)
        ob = _flash_attention(qb, kb, vb, CHUNK, "flash_b")
        oc = _flash_attention(qc, kc, vc, 1, "flash_c")
        od = _conv_module(l, u, w_taps, _row(b_dw), _row(g_conv_ln), _row(b_conv_ln))
        x2 = _merge(l, x2, h, oa, ob, oc, od, wg, _row(b_gate), wbr, wo)
        x2 = _ffn(l, x2, _row(g_ffn), wup, wdn, g_final.reshape(1, -1), final=(l == DEPTH - 1))
    return x2.reshape(BATCH, SEQ, D_MODEL)
```

```python
import functools

import numpy as np
import jax
import jax.numpy as jnp
from jax import lax
from jax.experimental import pallas as pl
from jax.experimental.pallas import tpu as pltpu

D_MODEL = 1024
BATCH = 4
SEQ = 4096
DEPTH = 2
N_TOK = BATCH * SEQ
CHUNK = 64
HEAD_DIM = 64
N_HEADS = 4
NORM_EPS = 1e-6
A_LEFT_CHUNKS = 8
A_MAX_REL = 128
B_Q_LORA = 256
B_KV_LORA = 128
B_NOPE = 64
B_ROPE = 32
B_V = 64
ROPE_THETA = 10000.0
CONV_CH = 256
CONV_K = 31
BRANCH_W = 256
D_FF = 4 * D_MODEL

LANES = 128
SUBLANES = 8
HEAD_PAD = LANES
QKV_PAD = N_HEADS * HEAD_PAD
NEG = -1e30
LOG2E = 1.4426950408889634

TM = 512
TILES_PER_SEQ = SEQ // TM
T_ATT = 512
TQ_A = 256
CONV_HALO = 32
VMEM_LIMIT = 56 * 1024 * 1024

F32 = jnp.float32
BF16 = jnp.bfloat16


def _dot(a, b):
    return jnp.dot(a, b, preferred_element_type=F32)


def _dot_nt(a, b):
    return lax.dot_general(a, b, (((1,), (1,)), ((), ())), preferred_element_type=F32)


def _rms(x, g):
    return x * lax.rsqrt(jnp.mean(x * x, axis=-1, keepdims=True) + NORM_EPS) * g


def _sigmoid(x):
    return 1.0 / (1.0 + jnp.exp(-x))


def _split3(x):
    hi = x.astype(BF16).astype(F32)
    r1 = x - hi
    mid = r1.astype(BF16).astype(F32)
    return hi, mid, r1 - mid


def _const_spec(shape):
    return pl.BlockSpec(shape, lambda *_: (0,) * len(shape))


def _layer_spec(arr, l):
    nd = arr.ndim - 1
    return pl.BlockSpec((None,) + tuple(arr.shape[1:]), lambda *_: (l,) + (0,) * nd)


def _row(p):
    return p.reshape(p.shape[0], 1, p.shape[1])


def _params(*sem):
    return pltpu.CompilerParams(dimension_semantics=sem, vmem_limit_bytes=VMEM_LIMIT)


def _rope_kernel(pos_ref, invf_ref, cos_ref, sin_ref):
    ang = pos_ref[...].astype(F32) * invf_ref[...]
    cos_ref[...] = jnp.cos(ang)
    sin_ref[...] = jnp.sin(ang)


def _rope_tables(positions):
    half = B_ROPE // 2
    inv_freq = 1.0 / (ROPE_THETA ** (jnp.arange(half, dtype=F32) / half))
    row = jnp.zeros((LANES,), F32).at[B_NOPE:B_NOPE + B_ROPE].set(jnp.tile(inv_freq, 2))
    pos = positions.reshape(N_TOK, 1)
    tm = 1024
    return pl.pallas_call(
        _rope_kernel,
        grid=(N_TOK // tm,),
        in_specs=[pl.BlockSpec((tm, 1), lambda i: (i, 0)), _const_spec((1, LANES))],
        out_specs=[pl.BlockSpec((tm, LANES), lambda i: (i, 0))] * 2,
        out_shape=[jax.ShapeDtypeStruct((N_TOK, LANES), F32)] * 2,
        compiler_params=_params("parallel"),
        name="rope_tables",
    )(pos, row.reshape(1, LANES))


COL_A, COL_B, COL_C, COL_D, COL_END = 0, 768, 1536, 2304, 2816
F_COPIES = 9
AUG_PER_HEAD = 2 * F_COPIES


def _pick3(sel, parts):
    return jnp.where(sel == 0, parts[0], jnp.where(sel == 1, parts[1], parts[2]))


def _inproj_kernel(x_ref, gmix_ref, w_ref, cos_ref, sin_ref, gq_ref, wuq_ref, gkv_ref, wukv_ref, bf_ref,
                   wdw_ref, bdw_ref, gln_ref, bln_ref, sel_ref, tri_ref, e_ref, oneaug_ref, onev_ref, onesblk_ref,
                   h_ref, qa_ref, ka_ref, va_ref, qb_ref, kb_ref, vb_ref,
                   qc_ref, kc_ref, vc_ref, od_ref, carry_ref, ext_ref, shift_ref):
    i = pl.program_id(0)

    @pl.when(i % TILES_PER_SEQ == 0)
    def _():
        carry_ref[...] = jnp.zeros_like(carry_ref)
        ext_ref[0:CONV_HALO, :] = jnp.zeros((CONV_HALO, CONV_CH), F32)

    h = _rms(x_ref[...], gmix_ref[...]).astype(BF16)
    h_ref[...] = h

    pd = _dot(h, w_ref[:, COL_D:COL_END])
    ext_ref[CONV_HALO:CONV_HALO + TM, :] = pd[:, 0:CONV_CH] * _sigmoid(pd[:, CONV_CH:2 * CONV_CH])
    od_ref[...] = _conv_ln_swish(ext_ref, shift_ref, wdw_ref, bdw_ref, gln_ref, bln_ref)
    ext_ref[0:CONV_HALO, :] = ext_ref[TM:TM + CONV_HALO, :]

    row = lax.broadcasted_iota(jnp.int32, (HEAD_PAD, TM), 0)
    own_rows = (row < HEAD_DIM, row >= HEAD_DIM)
    ones_blk = onesblk_ref[...]

    def pair_rows(q):
        qt = q.T
        return [jnp.where(own_rows[hd % 2], qt[(hd // 2) * HEAD_PAD:(hd // 2 + 1) * HEAD_PAD], 0.0)
                for hd in range(N_HEADS)]

    def v_rows(v):
        vt = v.T
        blocks = []
        for hd in range(N_HEADS):
            blocks += [vt[hd * HEAD_DIM:(hd + 1) * HEAD_DIM], ones_blk]
        return jnp.concatenate(blocks, axis=0).astype(BF16)

    pa = _dot(h, w_ref[:, COL_A:COL_B])
    ka_ref[...] = pa[:, 256:512].astype(BF16)
    qa_ref[...] = jnp.concatenate(pair_rows(pa[:, 0:256] * LOG2E), axis=0).astype(BF16)
    va_ref[...] = v_rows(pa[:, 512:768])

    pb = _dot(h, w_ref[:, COL_B:COL_C])
    cos = cos_ref[...]
    sin = sin_ref[...]
    cos4 = jnp.concatenate([cos] * N_HEADS, axis=-1)
    sin4 = jnp.concatenate([sin] * N_HEADS, axis=-1)
    qn = _rms(pb[:, 0:B_Q_LORA], gq_ref[...]).astype(BF16)
    qq = _dot(qn, wuq_ref[...])
    qb = qq[:, :QKV_PAD] * cos4 + qq[:, QKV_PAD:] * sin4
    qb_ref[...] = (qb * ((B_NOPE + B_ROPE) ** -0.5 * LOG2E)).T.astype(BF16)
    kvn = _rms(pb[:, B_Q_LORA:B_Q_LORA + B_KV_LORA], gkv_ref[...]).astype(BF16)
    krot = pb[:, 384:512] * cos + pb[:, 512:640] * sin
    kv = _dot(kvn, wukv_ref[...])
    kb_ref[...] = (kv[:, :QKV_PAD] + jnp.concatenate([krot] * N_HEADS, axis=-1)).astype(BF16)
    vb_ref[...] = (kv[:, QKV_PAD:] + onev_ref[...]).T.astype(BF16)

    pc = _dot(h, w_ref[:, COL_C:COL_D])
    piece_a = sel_ref[0:1, :]
    piece_b = sel_ref[1:2, :]
    z = pb[:, 640:768] + bf_ref[...]
    logf = jnp.minimum(z, 0.0) - jnp.log(1.0 + jnp.exp(-jnp.abs(z)))
    part = _dot(tri_ref[...], _pick3(piece_a, _split3(logf)).astype(BF16))
    g = part * LOG2E + carry_ref[...]
    last = g[TM - SUBLANES:TM, :]
    total = last + pltpu.roll(last, LANES - 3 * N_HEADS, 1) + pltpu.roll(last, LANES - 6 * N_HEADS, 1)
    carry_ref[...] = jnp.where(sel_ref[2:3, :] == 1, total[SUBLANES - 1:], 0.0)
    aug = _dot(_pick3(piece_b, _split3(g)).astype(BF16), e_ref[...]) + oneaug_ref[...]
    kc_ref[...] = jnp.concatenate([pc[:, 256:512], aug[:, LANES:]], axis=-1).astype(BF16)
    aug_t = aug[:, :LANES].T
    q_blocks = []
    for hd, pair in enumerate(pair_rows(pc[:, 0:256] * LOG2E)):
        mine = jnp.where(row >= hd * AUG_PER_HEAD, jnp.where(row < (hd + 1) * AUG_PER_HEAD, aug_t, 0.0), 0.0)
        q_blocks += [pair, mine]
    qc_ref[...] = jnp.concatenate(q_blocks, axis=0).astype(BF16)
    vc_ref[...] = v_rows(pc[:, 512:768])


def _pad_last(w, before, after):
    return jnp.pad(w, ((0, 0),) * (w.ndim - 1) + ((before, after),))


def _pad_heads(w):
    return _pad_last(w, 0, HEAD_PAD - w.shape[-1]).reshape(w.shape[:-2] + (QKV_PAD,))


def _rot_half_cols(w):
    half = B_ROPE // 2
    return jnp.concatenate([-w[..., half:], w[..., :half]], axis=-1)


def _inproj_consts():
    tri = np.tril(np.ones((TM, TM), np.float32))
    sel = np.full((SUBLANES, LANES), 2, np.int32)
    e = np.zeros((LANES, 2 * LANES), np.float32)
    oneaug = np.zeros((1, 2 * LANES), np.float32)
    for copy in range(F_COPIES):
        for hd in range(N_HEADS):
            lane = copy * N_HEADS + hd
            sel[0, lane], sel[1, lane] = divmod(copy, 3)
            e[lane, hd * AUG_PER_HEAD + copy] = 1.0
            oneaug[0, LANES + hd * AUG_PER_HEAD + copy] = 1.0
            e[lane, LANES + hd * AUG_PER_HEAD + F_COPIES + copy] = -1.0
            oneaug[0, hd * AUG_PER_HEAD + F_COPIES + copy] = 1.0
    sel[2, :] = 0
    sel[2, :3 * N_HEADS] = 1
    onev = np.zeros((1, QKV_PAD), np.float32)
    onev[0, HEAD_DIM::HEAD_PAD] = 1.0
    ones_blk = np.zeros((HEAD_DIM, TM), np.float32)
    ones_blk[0] = 1.0
    return (jnp.asarray(sel), jnp.asarray(tri, BF16), jnp.asarray(e, BF16), jnp.asarray(oneaug),
            jnp.asarray(onev), jnp.asarray(ones_blk))


def _inproj_weights(w_in, w_uq, w_ukv, b_forget):
    o = 0
    cols = {}
    for name, width in (("aq", 256), ("ak", 256), ("av", 256), ("bq", B_Q_LORA), ("bkv", B_KV_LORA),
                        ("bkr", B_ROPE), ("cq", 256), ("ck", 256), ("cv", 256), ("cf", N_HEADS),
                        ("dv", CONV_CH), ("dg", CONV_CH)):
        cols[name] = w_in[..., o:o + width]
        o += width
    scale = HEAD_DIM ** -0.5

    def rope_block(w):
        return _pad_last(w, B_NOPE, HEAD_PAD - B_NOPE - B_ROPE)

    cf = _pad_last(jnp.tile(cols["cf"], (1, 1, F_COPIES)), 0, LANES - F_COPIES * N_HEADS)
    w = jnp.concatenate([cols["aq"] * scale, cols["ak"], cols["av"],
                         cols["bq"], cols["bkv"], rope_block(cols["bkr"]), rope_block(_rot_half_cols(cols["bkr"])), cf,
                         cols["cq"] * scale, cols["ck"], cols["cv"],
                         cols["dv"], cols["dg"]], axis=-1).astype(BF16)

    uq = w_uq.reshape(DEPTH, B_Q_LORA, N_HEADS, B_NOPE + B_ROPE)
    uq_rot = jnp.concatenate([jnp.zeros_like(uq[..., :B_NOPE]), _rot_half_cols(uq[..., B_NOPE:])], axis=-1)
    wuq = jnp.concatenate([_pad_heads(uq), _pad_heads(uq_rot)], axis=-1).astype(BF16)
    ukv = w_ukv.reshape(DEPTH, B_KV_LORA, N_HEADS, B_NOPE + B_V)
    wukv = jnp.concatenate([_pad_heads(ukv[..., :B_NOPE]), _pad_heads(ukv[..., B_NOPE:])], axis=-1).astype(BF16)
    bf = _row(_pad_last(jnp.tile(b_forget, (1, F_COPIES)), 0, LANES - F_COPIES * N_HEADS))
    return w, wuq, wukv, bf


def _inproj(l, x2, g_mix, w, wuq, wukv, bf, g_q_lat, g_kv_lat, cos_t, sin_t, conv_params, consts):
    assert w.shape == (DEPTH, D_MODEL, COL_END)
    row = lambda w: pl.BlockSpec((TM, w), lambda i: (i, 0))
    layer = lambda a: (a, _layer_spec(a, l))
    ins = [
        (x2, row(D_MODEL)), layer(g_mix), layer(w), (cos_t, row(LANES)), (sin_t, row(LANES)),
        layer(g_q_lat), layer(wuq), layer(g_kv_lat), layer(wukv), layer(bf),
    ] + [layer(p) for p in conv_params] + [(c, None) for c in consts]
    args = [a for a, _ in ins]
    specs = [s if s is not None else _const_spec(a.shape) for a, s in ins]

    def rows(w, dt):
        return jax.ShapeDtypeStruct((N_TOK, w), dt), row(w)

    def slab(nrows):
        return (jax.ShapeDtypeStruct((N_TOK // TM, nrows, TM), BF16),
                pl.BlockSpec((None, nrows, TM), lambda i: (i, 0, 0)))

    outs = [rows(D_MODEL, BF16), slab(QKV_PAD), rows(256, BF16), slab(QKV_PAD),
            slab(QKV_PAD), rows(QKV_PAD, BF16), slab(QKV_PAD),
            slab(2 * QKV_PAD), rows(3 * HEAD_PAD, BF16), slab(QKV_PAD), rows(CONV_CH, BF16)]
    return pl.pallas_call(
        _inproj_kernel,
        grid=(N_TOK // TM,),
        in_specs=specs,
        out_specs=[s for _, s in outs],
        out_shape=[a for a, _ in outs],
        scratch_shapes=[pltpu.VMEM((1, LANES), F32)] + [pltpu.VMEM((CONV_HALO + TM, CONV_CH), F32)] * 2,
        compiler_params=_params("arbitrary"),
        name="inproj",
    )(*args)


def _band_kernel(qt_ref, k0_ref, k1_ref, k2_ref, vt0_ref, vt1_ref, vt2_ref, tbl_ref, o_ref):
    i = pl.program_id(1)
    k_refs = (k0_ref, k1_ref, k2_ref)
    vt_refs = (vt0_ref, vt1_ref, vt2_ref)

    def scores(hd):
        pair = slice((hd // 2) * HEAD_PAD, (hd // 2 + 1) * HEAD_PAD)
        qh = qt_ref[hd * HEAD_PAD:(hd + 1) * HEAD_PAD, :]
        s = []
        for j in range(3):
            sj = _dot(k_refs[j][:, pair], qh) + tbl_ref[j, hd]
            if j < 2:
                sj = jnp.where(i >= 2 - j, sj, NEG)
            s.append(sj)
        return s

    def softmax(s):
        m = jnp.maximum(jnp.maximum(s[0].max(0, keepdims=True), s[1].max(0, keepdims=True)),
                        s[2].max(0, keepdims=True))
        return [jnp.exp2(sj - m).astype(BF16) for sj in s]

    def weighted_sum(hd, p):
        rows = slice(hd * HEAD_PAD, (hd + 1) * HEAD_PAD)
        acc = _dot(vt_refs[0][rows, :], p[0]) + _dot(vt_refs[1][rows, :], p[1]) + _dot(vt_refs[2][rows, :], p[2])
        return acc[:HEAD_DIM, :] / acc[HEAD_DIM:HEAD_DIM + 1, :]

    s, p, outs = {}, {}, {}
    for tick in range(N_HEADS + 2):
        if tick < N_HEADS:
            s[tick] = scores(tick)
        if 0 <= tick - 1 < N_HEADS:
            p[tick - 1] = softmax(s.pop(tick - 1))
        if 0 <= tick - 2 < N_HEADS:
            outs[tick - 2] = weighted_sum(tick - 2, p.pop(tick - 2))
    o_ref[...] = jnp.concatenate([outs[hd] for hd in range(N_HEADS)], axis=0).T.astype(BF16)


def _band_table(rel_bias):
    w = TQ_A
    key = np.arange(w)[:, None]
    qry = np.arange(w)[None, :]
    flat = rel_bias.reshape(DEPTH * N_HEADS, -1)
    tabs = []
    for j in range(3):
        rel = (2 - j) * w + (np.arange(2 * w - 1) - (w - 1))
        idx = np.clip(rel, -A_MAX_REL, A_MAX_REL) + A_MAX_REL
        diag = _pad_last(flat[:, idx], 0, 1)
        skew = jnp.tile(diag, (1, w))[:, :w * (2 * w - 1)].reshape(-1, w, 2 * w - 1)
        bias = skew[:, :, w - 1:].reshape(DEPTH, N_HEADS, w, w)
        cd = (2 - j) * (w // CHUNK) + qry // CHUNK - key // CHUNK
        valid = (cd >= 0) & (cd <= A_LEFT_CHUNKS)
        tabs.append(jnp.where(valid, bias * LOG2E, NEG))
    return jnp.stack(tabs, axis=1).astype(F32)


def _band_attention(l, qat, ka, vat, tbl):
    nt = SEQ // TQ_A
    per_slab = TM // TQ_A

    def block(b, i, j):
        return b * nt + jnp.maximum(i - 2 + j, 0)

    def slab_spec(j):
        return pl.BlockSpec((None, QKV_PAD, TQ_A),
                            lambda b, i: (block(b, i, j) // per_slab, 0, block(b, i, j) % per_slab))

    def k_spec(j):
        return pl.BlockSpec((TQ_A, 256), lambda b, i: (block(b, i, j), 0))

    return pl.pallas_call(
        _band_kernel,
        grid=(BATCH, nt),
        in_specs=([slab_spec(2)] + [k_spec(j) for j in range(3)] + [slab_spec(j) for j in range(3)]
                  + [_layer_spec(tbl, l)]),
        out_specs=pl.BlockSpec((TQ_A, 256), lambda b, i: (b * nt + i, 0)),
        out_shape=jax.ShapeDtypeStruct((N_TOK, 256), BF16),
        compiler_params=_params("parallel", "parallel"),
        name="band_attention",
    )(qat, ka, ka, ka, vat, vat, vat, tbl)


def _flash_kernel(qt_ref, k_ref, vt_ref, o_ref, *, unit, pair_layout):
    i = pl.program_id(1)
    t = T_ATT
    key = lax.broadcasted_iota(jnp.int32, (t, t), 0)
    last_visible = lax.broadcasted_iota(jnp.int32, (1, t), 1) | (unit - 1)
    visible = key <= last_visible
    qrows = 2 * HEAD_PAD if pair_layout else HEAD_PAD

    def k_tile(k_ref, off, hd):
        if not pair_layout:
            return k_ref[pl.ds(off, t), hd * HEAD_PAD:(hd + 1) * HEAD_PAD]
        pair = hd // 2
        return jnp.concatenate([k_ref[pl.ds(off, t), pair * HEAD_PAD:(pair + 1) * HEAD_PAD],
                                k_ref[pl.ds(off, t), 2 * HEAD_PAD:3 * HEAD_PAD]], axis=1)

    def step(tiles, carry):
        heads = [slice(hd * HEAD_PAD, (hd + 1) * HEAD_PAD) for hd in range(N_HEADS)]
        m = [c[0] for c in carry]
        acc = [c[1] for c in carry]

        def scores(j, diag, hd):
            s = _dot(k_tile(k_ref, pl.multiple_of(j * t, t), hd), qt_ref[hd * qrows:(hd + 1) * qrows, :])
            return jnp.where(visible, s, NEG) if diag else s

        def softmax(hd, s):
            m_new = jnp.maximum(m[hd], s.max(0, keepdims=True))
            alpha = jnp.exp2(m[hd] - m_new)
            m[hd] = m_new
            return alpha, jnp.exp2(s - m_new).astype(BF16)

        def update(j, hd, alpha, p):
            acc[hd] = alpha * acc[hd] + _dot(vt_ref[j, heads[hd], :], p)

        units = [(j, diag, hd) for j, diag in tiles for hd in range(N_HEADS)]
        s, sm = {}, {}
        for tick in range(len(units) + 2):
            if tick < len(units):
                s[tick] = scores(*units[tick])
            if 0 <= tick - 1 < len(units):
                sm[tick - 1] = softmax(units[tick - 1][2], s.pop(tick - 1))
            if 0 <= tick - 2 < len(units):
                update(units[tick - 2][0], units[tick - 2][2], *sm.pop(tick - 2))
        return tuple(zip(m, acc))

    init = tuple((jnp.full((1, t), NEG, F32), jnp.zeros((HEAD_PAD, t), F32)) for _ in range(N_HEADS))
    carry = lax.fori_loop(0, i // 2, lambda jj, c: step([(2 * jj, False), (2 * jj + 1, False)], c), init)
    carry = lax.cond(i % 2 == 1,
                     lambda c: step([(i - 1, False), (i, True)], c),
                     lambda c: step([(i, True)], c), carry)
    outs = [acc[:HEAD_DIM, :] / acc[HEAD_DIM:HEAD_DIM + 1, :] for _, acc in carry]
    o_ref[...] = jnp.concatenate(outs, axis=0).T.astype(BF16)


def _flash_attention(qt, k, vt, unit, name):
    nt = SEQ // T_ATT
    pair_layout = k.shape[1] == 3 * HEAD_PAD
    return pl.pallas_call(
        functools.partial(_flash_kernel, unit=unit, pair_layout=pair_layout),
        grid=(BATCH, nt),
        in_specs=[pl.BlockSpec((None, qt.shape[1], T_ATT), lambda b, i: (b * nt + i, 0, 0)),
                  pl.BlockSpec((SEQ, k.shape[1]), lambda b, i: (b, 0)),
                  pl.BlockSpec((nt, QKV_PAD, T_ATT), lambda b, i: (b, 0, 0))],
        out_specs=pl.BlockSpec((T_ATT, 256), lambda b, i: (b * nt + i, 0)),
        out_shape=jax.ShapeDtypeStruct((N_TOK, 256), BF16),
        compiler_params=_params("parallel", "parallel"),
        name=name,
    )(qt, k, vt)


def _conv_ln_swish(ext_ref, shift_ref, w_ref, b_ref, g_ref, beta_ref):
    acc = jnp.zeros((TM, CONV_CH), F32) + b_ref[...]
    first = CONV_HALO - (CONV_K - 1)
    for res in range(SUBLANES):
        taps = [kk for kk in range(CONV_K) if (first + kk) % SUBLANES == res]
        span = max((first + kk) // SUBLANES for kk in taps) * SUBLANES + TM
        shift_ref[0:span, :] = ext_ref[res:res + span, :]
        for kk in taps:
            start = (first + kk) // SUBLANES * SUBLANES
            acc = acc + shift_ref[start:start + TM, :] * w_ref[kk:kk + 1, :]
    mu = jnp.mean(acc, axis=-1, keepdims=True)
    xc = acc - mu
    var = jnp.mean(xc * xc, axis=-1, keepdims=True)
    y = xc * lax.rsqrt(var + NORM_EPS) * g_ref[...] + beta_ref[...]
    return (y * _sigmoid(y)).astype(BF16)


def _merge_kernel(x_ref, h_ref, oa_ref, ob_ref, oc_ref, od_ref, wg_ref, bg_ref, wbr_ref, wo_ref, out_ref):
    h = h_ref[...]
    acc = jnp.zeros((TM, D_MODEL), F32)
    for n, o_ref in enumerate((oa_ref, ob_ref, oc_ref, od_ref)):
        sl = slice(n * D_MODEL, (n + 1) * D_MODEL)
        gate = _sigmoid(_dot(h, wg_ref[:, sl]) + bg_ref[:, sl])
        acc = acc + gate * _dot(o_ref[...], wbr_ref[n])
    out_ref[...] = x_ref[...] + _dot(acc.astype(BF16), wo_ref[...])


def _merge(l, x2, h, oa, ob, oc, od, wg, bg, wbr, wo):
    row = lambda w: pl.BlockSpec((TM, w), lambda i: (i, 0))
    return pl.pallas_call(
        _merge_kernel,
        grid=(N_TOK // TM,),
        in_specs=[row(D_MODEL), row(D_MODEL)] + [row(BRANCH_W)] * 4
                 + [_layer_spec(p, l) for p in (wg, bg, wbr, wo)],
        out_specs=row(D_MODEL),
        out_shape=jax.ShapeDtypeStruct((N_TOK, D_MODEL), F32),
        compiler_params=_params("parallel"),
        name="merge",
    )(x2, h, oa, ob, oc, od, wg, bg, wbr, wo)


def _ffn_kernel(x_ref, g_ref, wup_ref, wdn_ref, gfin_ref, out_ref, *, final):
    x = x_ref[...]
    hf = _rms(x, g_ref[...]).astype(BF16)
    acc = x
    for c in range(D_FF // D_MODEL):
        sl = slice(c * D_MODEL, (c + 1) * D_MODEL)
        up = jnp.maximum(_dot(hf, wup_ref[:, sl]), 0.0)
        acc = acc + _dot((up * up).astype(BF16), wdn_ref[sl, :])
    if final:
        acc = _rms(acc, gfin_ref[...])
    out_ref[...] = acc


def _ffn(l, x2, g_ffn, wup, wdn, g_final, final):
    row = pl.BlockSpec((TM, D_MODEL), lambda i: (i, 0))
    return pl.pallas_call(
        functools.partial(_ffn_kernel, final=final),
        grid=(N_TOK // TM,),
        in_specs=[row] + [_layer_spec(p, l) for p in (g_ffn, wup, wdn)] + [_const_spec((1, D_MODEL))],
        out_specs=row,
        out_shape=jax.ShapeDtypeStruct((N_TOK, D_MODEL), F32),
        compiler_params=_params("parallel"),
        name="ffn",
    )(x2, g_ffn, wup, wdn, g_final)


def kernel(x, positions, g_mix, w_in, w_gate, b_gate, rel_bias, g_q_lat, w_uq, g_kv_lat, w_ukv, b_forget,
           w_dw, b_dw, g_conv_ln, b_conv_ln, w_branch, w_o, g_ffn, w_up, w_down, g_final):
    x2 = x.reshape(N_TOK, D_MODEL)
    cos_t, sin_t = _rope_tables(positions)
    consts = _inproj_consts()
    w, wuq, wukv, bf = _inproj_weights(w_in, w_uq, w_ukv, b_forget)
    tbl = _band_table(rel_bias)
    w_taps = jnp.pad(w_dw, ((0, 0), (0, 1), (0, 0)))
    wg, wbr, wo, wup, wdn = (p.astype(BF16) for p in (w_gate, w_branch, w_o, w_up, w_down))
    conv_params = (w_taps, _row(b_dw), _row(g_conv_ln), _row(b_conv_ln))
    for l in range(DEPTH):
        (h, qa, ka, va, qb, kb, vb, qc, kc, vc, od) = _inproj(
            l, x2, _row(g_mix), w, wuq, wukv, bf, _row(g_q_lat), _row(g_kv_lat), cos_t, sin_t, conv_params, consts)
        oa = _band_attention(l, qa, ka, va, tbl)
        ob = _flash_attention(qb, kb, vb, CHUNK, "flash_b")
        oc = _flash_attention(qc, kc, vc, 1, "flash_c")
        x2 = _merge(l, x2, h, oa, ob, oc, od, wg, _row(b_gate), wbr, wo)
        x2 = _ffn(l, x2, _row(g_ffn), wup, wdn, g_final.reshape(1, -1), final=(l == DEPTH - 1))
    return x2.reshape(BATCH, SEQ, D_MODEL)
```

```python
import functools

import numpy as np
import jax
import jax.numpy as jnp
from jax import lax
from jax.experimental import pallas as pl
from jax.experimental.pallas import tpu as pltpu

D_MODEL = 1024
BATCH = 4
SEQ = 4096
DEPTH = 2
N_TOK = BATCH * SEQ
CHUNK = 64
HEAD_DIM = 64
N_HEADS = 4
NORM_EPS = 1e-6
A_LEFT_CHUNKS = 8
A_MAX_REL = 128
B_Q_LORA = 256
B_KV_LORA = 128
B_NOPE = 64
B_ROPE = 32
B_V = 64
ROPE_THETA = 10000.0
CONV_CH = 256
CONV_K = 31
BRANCH_W = 256
D_FF = 4 * D_MODEL

LANES = 128
SUBLANES = 8
HEAD_PAD = LANES
QKV_PAD = N_HEADS * HEAD_PAD
NEG = -1e30
LOG2E = 1.4426950408889634

TM = 512
TILES_PER_SEQ = SEQ // TM
T_ATT = 512
TQ_A = 256
CONV_HALO = 32
VMEM_LIMIT = 56 * 1024 * 1024

F32 = jnp.float32
BF16 = jnp.bfloat16


def _dot(a, b):
    return jnp.dot(a, b, preferred_element_type=F32)


def _dot_nt(a, b):
    return lax.dot_general(a, b, (((1,), (1,)), ((), ())), preferred_element_type=F32)


def _rms(x, g):
    return x * lax.rsqrt(jnp.mean(x * x, axis=-1, keepdims=True) + NORM_EPS) * g


def _sigmoid(x):
    return 1.0 / (1.0 + jnp.exp(-x))


def _split3(x):
    hi = x.astype(BF16).astype(F32)
    r1 = x - hi
    mid = r1.astype(BF16).astype(F32)
    return hi, mid, r1 - mid


def _const_spec(shape):
    return pl.BlockSpec(shape, lambda *_: (0,) * len(shape))


def _layer_spec(arr, l):
    nd = arr.ndim - 1
    return pl.BlockSpec((None,) + tuple(arr.shape[1:]), lambda *_: (l,) + (0,) * nd)


def _row(p):
    return p.reshape(p.shape[0], 1, p.shape[1])


def _params(*sem):
    return pltpu.CompilerParams(dimension_semantics=sem, vmem_limit_bytes=VMEM_LIMIT)


def _rope_kernel(pos_ref, invf_ref, cos_ref, sin_ref):
    ang = pos_ref[...].astype(F32) * invf_ref[...]
    cos_ref[...] = jnp.cos(ang)
    sin_ref[...] = jnp.sin(ang)


def _rope_tables(positions):
    half = B_ROPE // 2
    inv_freq = 1.0 / (ROPE_THETA ** (jnp.arange(half, dtype=F32) / half))
    row = jnp.zeros((LANES,), F32).at[B_NOPE:B_NOPE + B_ROPE].set(jnp.tile(inv_freq, 2))
    pos = positions.reshape(N_TOK, 1)
    tm = 1024
    return pl.pallas_call(
        _rope_kernel,
        grid=(N_TOK // tm,),
        in_specs=[pl.BlockSpec((tm, 1), lambda i: (i, 0)), _const_spec((1, LANES))],
        out_specs=[pl.BlockSpec((tm, LANES), lambda i: (i, 0))] * 2,
        out_shape=[jax.ShapeDtypeStruct((N_TOK, LANES), F32)] * 2,
        compiler_params=_params("parallel"),
        name="rope_tables",
    )(pos, row.reshape(1, LANES))


COL_A, COL_B, COL_C, COL_D, COL_END = 0, 768, 1536, 2304, 2816
F_COPIES = 9
AUG_PER_HEAD = 2 * F_COPIES


def _pick3(sel, parts):
    return jnp.where(sel == 0, parts[0], jnp.where(sel == 1, parts[1], parts[2]))


def _inproj_kernel(x_ref, gmix_ref, w_ref, cos_ref, sin_ref, gq_ref, wuq_ref, gkv_ref, wukv_ref, bf_ref,
                   wdw_ref, bdw_ref, gln_ref, bln_ref, sel_ref, tri_ref, e_ref, oneaug_ref, onev_ref, onesblk_ref,
                   h_ref, qa_ref, ka_ref, va_ref, qb_ref, kb_ref, vb_ref,
                   qc_ref, kc_ref, vc_ref, od_ref, carry_ref, ext_ref, shift_ref):
    i = pl.program_id(0)

    @pl.when(i % TILES_PER_SEQ == 0)
    def _():
        carry_ref[...] = jnp.zeros_like(carry_ref)
        ext_ref[0:CONV_HALO, :] = jnp.zeros((CONV_HALO, CONV_CH), F32)

    h = _rms(x_ref[...], gmix_ref[...]).astype(BF16)
    h_ref[...] = h

    pd = _dot_nt(h, w_ref[COL_D:COL_END, :])
    ext_ref[CONV_HALO:CONV_HALO + TM, :] = pd[:, 0:CONV_CH] * _sigmoid(pd[:, CONV_CH:2 * CONV_CH])
    od_ref[...] = _conv_ln_swish(ext_ref, shift_ref, wdw_ref, bdw_ref, gln_ref, bln_ref)
    ext_ref[0:CONV_HALO, :] = ext_ref[TM:TM + CONV_HALO, :]

    row = lax.broadcasted_iota(jnp.int32, (HEAD_PAD, TM), 0)
    own_rows = (row < HEAD_DIM, row >= HEAD_DIM)
    ones_blk = onesblk_ref[...]

    def pair_rows(q):
        qt = q.T
        return [jnp.where(own_rows[hd % 2], qt[(hd // 2) * HEAD_PAD:(hd // 2 + 1) * HEAD_PAD], 0.0)
                for hd in range(N_HEADS)]

    def v_rows(v):
        vt = v.T
        blocks = []
        for hd in range(N_HEADS):
            blocks += [vt[hd * HEAD_DIM:(hd + 1) * HEAD_DIM], ones_blk]
        return jnp.concatenate(blocks, axis=0).astype(BF16)

    pa = _dot_nt(h, w_ref[COL_A:COL_B, :])
    ka_ref[...] = pa[:, 256:512].astype(BF16)
    qa_ref[...] = jnp.concatenate(pair_rows(pa[:, 0:256] * LOG2E), axis=0).astype(BF16)
    va_ref[...] = v_rows(pa[:, 512:768])

    pb = _dot_nt(h, w_ref[COL_B:COL_C, :])
    cos = cos_ref[...]
    sin = sin_ref[...]
    cos4 = jnp.concatenate([cos] * N_HEADS, axis=-1)
    sin4 = jnp.concatenate([sin] * N_HEADS, axis=-1)
    qn = _rms(pb[:, 0:B_Q_LORA], gq_ref[...]).astype(BF16)
    qq = _dot(qn, wuq_ref[...])
    qb = qq[:, :QKV_PAD] * cos4 + qq[:, QKV_PAD:] * sin4
    qb_ref[...] = (qb * ((B_NOPE + B_ROPE) ** -0.5 * LOG2E)).T.astype(BF16)
    kvn = _rms(pb[:, B_Q_LORA:B_Q_LORA + B_KV_LORA], gkv_ref[...]).astype(BF16)
    krot = pb[:, 384:512] * cos + pb[:, 512:640] * sin
    kv = _dot(kvn, wukv_ref[...])
    kb_ref[...] = (kv[:, :QKV_PAD] + jnp.concatenate([krot] * N_HEADS, axis=-1)).astype(BF16)
    vb_ref[...] = (kv[:, QKV_PAD:] + onev_ref[...]).T.astype(BF16)

    pc = _dot_nt(h, w_ref[COL_C:COL_D, :])
    piece_a = sel_ref[0:1, :]
    piece_b = sel_ref[1:2, :]
    z = pb[:, 640:768] + bf_ref[...]
    logf = jnp.minimum(z, 0.0) - jnp.log(1.0 + jnp.exp(-jnp.abs(z)))
    part = _dot(tri_ref[...], _pick3(piece_a, _split3(logf)).astype(BF16))
    g = part * LOG2E + carry_ref[...]
    last = g[TM - SUBLANES:TM, :]
    total = last + pltpu.roll(last, LANES - 3 * N_HEADS, 1) + pltpu.roll(last, LANES - 6 * N_HEADS, 1)
    carry_ref[...] = jnp.where(sel_ref[2:3, :] == 1, total[SUBLANES - 1:], 0.0)
    aug = _dot(_pick3(piece_b, _split3(g)).astype(BF16), e_ref[...]) + oneaug_ref[...]
    kc_ref[...] = jnp.concatenate([pc[:, 256:512], aug[:, LANES:]], axis=-1).astype(BF16)
    aug_t = aug[:, :LANES].T
    q_blocks = []
    for hd, pair in enumerate(pair_rows(pc[:, 0:256] * LOG2E)):
        mine = jnp.where(row >= hd * AUG_PER_HEAD, jnp.where(row < (hd + 1) * AUG_PER_HEAD, aug_t, 0.0), 0.0)
        q_blocks += [pair, mine]
    qc_ref[...] = jnp.concatenate(q_blocks, axis=0).astype(BF16)
    vc_ref[...] = v_rows(pc[:, 512:768])


def _pad_last(w, before, after):
    return jnp.pad(w, ((0, 0),) * (w.ndim - 1) + ((before, after),))


def _pad_heads(w):
    return _pad_last(w, 0, HEAD_PAD - w.shape[-1]).reshape(w.shape[:-2] + (QKV_PAD,))


def _rot_half_cols(w):
    half = B_ROPE // 2
    return jnp.concatenate([-w[..., half:], w[..., :half]], axis=-1)


def _inproj_consts():
    tri = np.tril(np.ones((TM, TM), np.float32))
    sel = np.full((SUBLANES, LANES), 2, np.int32)
    e = np.zeros((LANES, 2 * LANES), np.float32)
    oneaug = np.zeros((1, 2 * LANES), np.float32)
    for copy in range(F_COPIES):
        for hd in range(N_HEADS):
            lane = copy * N_HEADS + hd
            sel[0, lane], sel[1, lane] = divmod(copy, 3)
            e[lane, hd * AUG_PER_HEAD + copy] = 1.0
            oneaug[0, LANES + hd * AUG_PER_HEAD + copy] = 1.0
            e[lane, LANES + hd * AUG_PER_HEAD + F_COPIES + copy] = -1.0
            oneaug[0, hd * AUG_PER_HEAD + F_COPIES + copy] = 1.0
    sel[2, :] = 0
    sel[2, :3 * N_HEADS] = 1
    onev = np.zeros((1, QKV_PAD), np.float32)
    onev[0, HEAD_DIM::HEAD_PAD] = 1.0
    ones_blk = np.zeros((HEAD_DIM, TM), np.float32)
    ones_blk[0] = 1.0
    return (jnp.asarray(sel), jnp.asarray(tri, BF16), jnp.asarray(e, BF16), jnp.asarray(oneaug),
            jnp.asarray(onev), jnp.asarray(ones_blk))


def _inproj_weights(w_in, w_uq, w_ukv, b_forget):
    wt = jnp.swapaxes(w_in, 1, 2).astype(BF16)
    o = 0
    cols = {}
    for name, width in (("aq", 256), ("ak", 256), ("av", 256), ("bq", B_Q_LORA), ("bkv", B_KV_LORA),
                        ("bkr", B_ROPE), ("cq", 256), ("ck", 256), ("cv", 256), ("cf", N_HEADS),
                        ("dv", CONV_CH), ("dg", CONV_CH)):
        cols[name] = wt[:, o:o + width, :]
        o += width
    scale = HEAD_DIM ** -0.5

    def pad_rows(w, before, after):
        return jnp.pad(w, ((0, 0), (before, after), (0, 0)))

    def rope_block(w):
        return pad_rows(w, B_NOPE, HEAD_PAD - B_NOPE - B_ROPE)

    half = B_ROPE // 2
    bkr_rot = jnp.concatenate([-cols["bkr"][:, half:], cols["bkr"][:, :half]], axis=1)
    cf = pad_rows(jnp.tile(cols["cf"], (1, F_COPIES, 1)), 0, LANES - F_COPIES * N_HEADS)
    w = jnp.concatenate([cols["aq"] * scale, cols["ak"], cols["av"],
                         cols["bq"], cols["bkv"], rope_block(cols["bkr"]), rope_block(bkr_rot), cf,
                         cols["cq"] * scale, cols["ck"], cols["cv"],
                         cols["dv"], cols["dg"]], axis=1)

    uq = w_uq.reshape(DEPTH, B_Q_LORA, N_HEADS, B_NOPE + B_ROPE)
    uq_rot = jnp.concatenate([jnp.zeros_like(uq[..., :B_NOPE]), _rot_half_cols(uq[..., B_NOPE:])], axis=-1)
    wuq = jnp.concatenate([_pad_heads(uq), _pad_heads(uq_rot)], axis=-1).astype(BF16)
    ukv = w_ukv.reshape(DEPTH, B_KV_LORA, N_HEADS, B_NOPE + B_V)
    wukv = jnp.concatenate([_pad_heads(ukv[..., :B_NOPE]), _pad_heads(ukv[..., B_NOPE:])], axis=-1).astype(BF16)
    bf = _row(_pad_last(jnp.tile(b_forget, (1, F_COPIES)), 0, LANES - F_COPIES * N_HEADS))
    return w, wuq, wukv, bf


def _inproj(l, x2, g_mix, w, wuq, wukv, bf, g_q_lat, g_kv_lat, cos_t, sin_t, conv_params, consts):
    assert w.shape == (DEPTH, COL_END, D_MODEL)
    row = lambda w: pl.BlockSpec((TM, w), lambda i: (i, 0))
    layer = lambda a: (a, _layer_spec(a, l))
    ins = [
        (x2, row(D_MODEL)), layer(g_mix), layer(w), (cos_t, row(LANES)), (sin_t, row(LANES)),
        layer(g_q_lat), layer(wuq), layer(g_kv_lat), layer(wukv), layer(bf),
    ] + [layer(p) for p in conv_params] + [(c, None) for c in consts]
    args = [a for a, _ in ins]
    specs = [s if s is not None else _const_spec(a.shape) for a, s in ins]

    def rows(w, dt):
        return jax.ShapeDtypeStruct((N_TOK, w), dt), row(w)

    def slab(nrows):
        return (jax.ShapeDtypeStruct((N_TOK // TM, nrows, TM), BF16),
                pl.BlockSpec((None, nrows, TM), lambda i: (i, 0, 0)))

    outs = [rows(D_MODEL, BF16), slab(QKV_PAD), rows(256, BF16), slab(QKV_PAD),
            slab(QKV_PAD), rows(QKV_PAD, BF16), slab(QKV_PAD),
            slab(2 * QKV_PAD), rows(3 * HEAD_PAD, BF16), slab(QKV_PAD), rows(CONV_CH, BF16)]
    return pl.pallas_call(
        _inproj_kernel,
        grid=(N_TOK // TM,),
        in_specs=specs,
        out_specs=[s for _, s in outs],
        out_shape=[a for a, _ in outs],
        scratch_shapes=[pltpu.VMEM((1, LANES), F32)] + [pltpu.VMEM((CONV_HALO + TM, CONV_CH), F32)] * 2,
        compiler_params=_params("arbitrary"),
        name="inproj",
    )(*args)


def _band_kernel(qt_ref, k0_ref, k1_ref, k2_ref, vt0_ref, vt1_ref, vt2_ref, tbl_ref, o_ref):
    i = pl.program_id(1)
    k_refs = (k0_ref, k1_ref, k2_ref)
    vt_refs = (vt0_ref, vt1_ref, vt2_ref)

    def scores(hd):
        pair = slice((hd // 2) * HEAD_PAD, (hd // 2 + 1) * HEAD_PAD)
        qh = qt_ref[hd * HEAD_PAD:(hd + 1) * HEAD_PAD, :]
        s = []
        for j in range(3):
            sj = _dot(k_refs[j][:, pair], qh) + tbl_ref[j, hd]
            if j < 2:
                sj = jnp.where(i >= 2 - j, sj, NEG)
            s.append(sj)
        return s

    def softmax(s):
        m = jnp.maximum(jnp.maximum(s[0].max(0, keepdims=True), s[1].max(0, keepdims=True)),
                        s[2].max(0, keepdims=True))
        return [jnp.exp2(sj - m).astype(BF16) for sj in s]

    def weighted_sum(hd, p):
        rows = slice(hd * HEAD_PAD, (hd + 1) * HEAD_PAD)
        acc = _dot(vt_refs[0][rows, :], p[0]) + _dot(vt_refs[1][rows, :], p[1]) + _dot(vt_refs[2][rows, :], p[2])
        return acc[:HEAD_DIM, :] / acc[HEAD_DIM:HEAD_DIM + 1, :]

    s, p, outs = {}, {}, {}
    for tick in range(N_HEADS + 2):
        if tick < N_HEADS:
            s[tick] = scores(tick)
        if 0 <= tick - 1 < N_HEADS:
            p[tick - 1] = softmax(s.pop(tick - 1))
        if 0 <= tick - 2 < N_HEADS:
            outs[tick - 2] = weighted_sum(tick - 2, p.pop(tick - 2))
    o_ref[...] = jnp.concatenate([outs[hd] for hd in range(N_HEADS)], axis=0).T.astype(BF16)


def _band_table(rel_bias):
    w = TQ_A
    key = np.arange(w)[:, None]
    qry = np.arange(w)[None, :]
    diags, valid = [], []
    for j in range(3):
        rel = (2 - j) * w + (np.arange(2 * w) - (w - 1))
        diags.append(rel_bias[..., np.clip(rel, -A_MAX_REL, A_MAX_REL) + A_MAX_REL])
        cd = (2 - j) * (w // CHUNK) + qry // CHUNK - key // CHUNK
        valid.append((cd >= 0) & (cd <= A_LEFT_CHUNKS))
    diags = jnp.stack(diags, axis=1)
    valid = jnp.asarray(np.stack(valid), F32)
    return pl.pallas_call(
        _band_table_kernel,
        grid=(DEPTH, 3),
        in_specs=[pl.BlockSpec((None, None, N_HEADS, 2 * w), lambda l, j: (l, j, 0, 0)),
                  pl.BlockSpec((None, w, w), lambda l, j: (j, 0, 0))],
        out_specs=pl.BlockSpec((None, None, N_HEADS, w, w), lambda l, j: (l, j, 0, 0, 0)),
        out_shape=jax.ShapeDtypeStruct((DEPTH, 3, N_HEADS, w, w), F32),
        compiler_params=_params("parallel", "parallel"),
        name="band_table",
    )(diags, valid)


def _band_table_kernel(diag_ref, valid_ref, o_ref):
    for hd in range(N_HEADS):
        rows = jnp.broadcast_to(diag_ref[hd:hd + 1, :], (TQ_A, 2 * TQ_A))
        toeplitz = pltpu.roll(rows, TQ_A + 1, 1, stride=1, stride_axis=0)[:, :TQ_A]
        o_ref[hd] = jnp.where(valid_ref[...] > 0.0, toeplitz * LOG2E, NEG)


def _band_attention(l, qat, ka, vat, tbl):
    nt = SEQ // TQ_A
    per_slab = TM // TQ_A

    def block(b, i, j):
        return b * nt + jnp.maximum(i - 2 + j, 0)

    def slab_spec(j):
        return pl.BlockSpec((None, QKV_PAD, TQ_A),
                            lambda b, i: (block(b, i, j) // per_slab, 0, block(b, i, j) % per_slab))

    def k_spec(j):
        return pl.BlockSpec((TQ_A, 256), lambda b, i: (block(b, i, j), 0))

    return pl.pallas_call(
        _band_kernel,
        grid=(BATCH, nt),
        in_specs=([slab_spec(2)] + [k_spec(j) for j in range(3)] + [slab_spec(j) for j in range(3)]
                  + [_layer_spec(tbl, l)]),
        out_specs=pl.BlockSpec((TQ_A, 256), lambda b, i: (b * nt + i, 0)),
        out_shape=jax.ShapeDtypeStruct((N_TOK, 256), BF16),
        compiler_params=_params("parallel", "parallel"),
        name="band_attention",
    )(qat, ka, ka, ka, vat, vat, vat, tbl)


def _flash_kernel(qt_ref, k_ref, vt_ref, o_ref, *, unit, pair_layout):
    i = pl.program_id(1)
    t = T_ATT
    key = lax.broadcasted_iota(jnp.int32, (t, t), 0)
    last_visible = lax.broadcasted_iota(jnp.int32, (1, t), 1) | (unit - 1)
    visible = key <= last_visible
    qrows = 2 * HEAD_PAD if pair_layout else HEAD_PAD

    def k_tile(k_ref, off, hd):
        if not pair_layout:
            return k_ref[pl.ds(off, t), hd * HEAD_PAD:(hd + 1) * HEAD_PAD]
        pair = hd // 2
        return jnp.concatenate([k_ref[pl.ds(off, t), pair * HEAD_PAD:(pair + 1) * HEAD_PAD],
                                k_ref[pl.ds(off, t), 2 * HEAD_PAD:3 * HEAD_PAD]], axis=1)

    def step(tiles, carry):
        heads = [slice(hd * HEAD_PAD, (hd + 1) * HEAD_PAD) for hd in range(N_HEADS)]
        m = [c[0] for c in carry]
        acc = [c[1] for c in carry]

        def scores(j, diag, hd):
            s = _dot(k_tile(k_ref, pl.multiple_of(j * t, t), hd), qt_ref[hd * qrows:(hd + 1) * qrows, :])
            return jnp.where(visible, s, NEG) if diag else s

        def softmax(hd, s):
            m_new = jnp.maximum(m[hd], s.max(0, keepdims=True))
            alpha = jnp.exp2(m[hd] - m_new)
            m[hd] = m_new
            return alpha, jnp.exp2(s - m_new).astype(BF16)

        def update(j, hd, alpha, p):
            acc[hd] = alpha * acc[hd] + _dot(vt_ref[j, heads[hd], :], p)

        units = [(j, diag, hd) for j, diag in tiles for hd in range(N_HEADS)]
        s, sm = {}, {}
        for tick in range(len(units) + 2):
            if tick < len(units):
                s[tick] = scores(*units[tick])
            if 0 <= tick - 1 < len(units):
                sm[tick - 1] = softmax(units[tick - 1][2], s.pop(tick - 1))
            if 0 <= tick - 2 < len(units):
                update(units[tick - 2][0], units[tick - 2][2], *sm.pop(tick - 2))
        return tuple(zip(m, acc))

    init = tuple((jnp.full((1, t), NEG, F32), jnp.zeros((HEAD_PAD, t), F32)) for _ in range(N_HEADS))
    carry = lax.fori_loop(0, i // 2, lambda jj, c: step([(2 * jj, False), (2 * jj + 1, False)], c), init)
    carry = lax.cond(i % 2 == 1,
                     lambda c: step([(i - 1, False), (i, True)], c),
                     lambda c: step([(i, True)], c), carry)
    outs = [acc[:HEAD_DIM, :] / acc[HEAD_DIM:HEAD_DIM + 1, :] for _, acc in carry]
    o_ref[...] = jnp.concatenate(outs, axis=0).T.astype(BF16)


def _flash_attention(qt, k, vt, unit, name):
    nt = SEQ // T_ATT
    pair_layout = k.shape[1] == 3 * HEAD_PAD
    return pl.pallas_call(
        functools.partial(_flash_kernel, unit=unit, pair_layout=pair_layout),
        grid=(BATCH, nt),
        in_specs=[pl.BlockSpec((None, qt.shape[1], T_ATT), lambda b, i: (b * nt + i, 0, 0)),
                  pl.BlockSpec((SEQ, k.shape[1]), lambda b, i: (b, 0)),
                  pl.BlockSpec((nt, QKV_PAD, T_ATT), lambda b, i: (b, 0, 0))],
        out_specs=pl.BlockSpec((T_ATT, 256), lambda b, i: (b * nt + i, 0)),
        out_shape=jax.ShapeDtypeStruct((N_TOK, 256), BF16),
        compiler_params=_params("parallel", "parallel"),
        name=name,
    )(qt, k, vt)


def _conv_ln_swish(ext_ref, shift_ref, w_ref, b_ref, g_ref, beta_ref):
    acc = jnp.zeros((TM, CONV_CH), F32) + b_ref[...]
    first = CONV_HALO - (CONV_K - 1)
    for res in range(SUBLANES):
        taps = [kk for kk in range(CONV_K) if (first + kk) % SUBLANES == res]
        span = max((first + kk) // SUBLANES for kk in taps) * SUBLANES + TM
        shift_ref[0:span, :] = ext_ref[res:res + span, :]
        for kk in taps:
            start = (first + kk) // SUBLANES * SUBLANES
            acc = acc + shift_ref[start:start + TM, :] * w_ref[kk:kk + 1, :]
    mu = jnp.mean(acc, axis=-1, keepdims=True)
    xc = acc - mu
    var = jnp.mean(xc * xc, axis=-1, keepdims=True)
    y = xc * lax.rsqrt(var + NORM_EPS) * g_ref[...] + beta_ref[...]
    return (y * _sigmoid(y)).astype(BF16)


def _merge_kernel(x_ref, h_ref, oa_ref, ob_ref, oc_ref, od_ref, wg_ref, bg_ref, wbr_ref, wo_ref, out_ref):
    h = h_ref[...]
    acc = jnp.zeros((TM, D_MODEL), F32)
    for n, o_ref in enumerate((oa_ref, ob_ref, oc_ref, od_ref)):
        sl = slice(n * D_MODEL, (n + 1) * D_MODEL)
        gate = _sigmoid(_dot(h, wg_ref[:, sl]) + bg_ref[:, sl])
        acc = acc + gate * _dot(o_ref[...], wbr_ref[n])
    out_ref[...] = x_ref[...] + _dot(acc.astype(BF16), wo_ref[...])


def _merge(l, x2, h, oa, ob, oc, od, wg, bg, wbr, wo):
    row = lambda w: pl.BlockSpec((TM, w), lambda i: (i, 0))
    return pl.pallas_call(
        _merge_kernel,
        grid=(N_TOK // TM,),
        in_specs=[row(D_MODEL), row(D_MODEL)] + [row(BRANCH_W)] * 4
                 + [_layer_spec(p, l) for p in (wg, bg, wbr, wo)],
        out_specs=row(D_MODEL),
        out_shape=jax.ShapeDtypeStruct((N_TOK, D_MODEL), F32),
        compiler_params=_params("parallel"),
        name="merge",
    )(x2, h, oa, ob, oc, od, wg, bg, wbr, wo)


def _ffn_kernel(x_ref, g_ref, wup_ref, wdn_ref, gfin_ref, out_ref, *, final):
    x = x_ref[...]
    hf = _rms(x, g_ref[...]).astype(BF16)
    acc = x
    for c in range(D_FF // D_MODEL):
        sl = slice(c * D_MODEL, (c + 1) * D_MODEL)
        up = jnp.maximum(_dot(hf, wup_ref[:, sl]), 0.0)
        acc = acc + _dot((up * up).astype(BF16), wdn_ref[sl, :])
    if final:
        acc = _rms(acc, gfin_ref[...])
    out_ref[...] = acc


def _ffn(l, x2, g_ffn, wup, wdn, g_final, final):
    row = pl.BlockSpec((TM, D_MODEL), lambda i: (i, 0))
    return pl.pallas_call(
        functools.partial(_ffn_kernel, final=final),
        grid=(N_TOK // TM,),
        in_specs=[row] + [_layer_spec(p, l) for p in (g_ffn, wup, wdn)] + [_const_spec((1, D_MODEL))],
        out_specs=row,
        out_shape=jax.ShapeDtypeStruct((N_TOK, D_MODEL), F32),
        compiler_params=_params("parallel"),
        name="ffn",
    )(x2, g_ffn, wup, wdn, g_final)


def kernel(x, positions, g_mix, w_in, w_gate, b_gate, rel_bias, g_q_lat, w_uq, g_kv_lat, w_ukv, b_forget,
           w_dw, b_dw, g_conv_ln, b_conv_ln, w_branch, w_o, g_ffn, w_up, w_down, g_final):
    x2 = x.reshape(N_TOK, D_MODEL)
    cos_t, sin_t = _rope_tables(positions)
    consts = _inproj_consts()
    w, wuq, wukv, bf = _inproj_weights(w_in, w_uq, w_ukv, b_forget)
    tbl = _band_table(rel_bias)
    w_taps = jnp.pad(w_dw, ((0, 0), (0, 1), (0, 0)))
    wg, wbr, wo, wup, wdn = (p.astype(BF16) for p in (w_gate, w_branch, w_o, w_up, w_down))
    conv_params = (w_taps, _row(b_dw), _row(g_conv_ln), _row(b_conv_ln))
    for l in range(DEPTH):
        (h, qa, ka, va, qb, kb, vb, qc, kc, vc, od) = _inproj(
            l, x2, _row(g_mix), w, wuq, wukv, bf, _row(g_q_lat), _row(g_kv_lat), cos_t, sin_t, conv_params, consts)
        oa = _band_attention(l, qa, ka, va, tbl)
        ob = _flash_attention(qb, kb, vb, CHUNK, "flash_b")
        oc = _flash_attention(qc, kc, vc, 1, "flash_c")
        x2 = _merge(l, x2, h, oa, ob, oc, od, wg, _row(b_gate), wbr, wo)
        x2 = _ffn(l, x2, _row(g_ffn), wup, wdn, g_final.reshape(1, -1), final=(l == DEPTH - 1))
    return x2.reshape(BATCH, SEQ, D_MODEL)
```

```python
import functools

import numpy as np
import jax
import jax.numpy as jnp
from jax import lax
from jax.experimental import pallas as pl
from jax.experimental.pallas import tpu as pltpu

D_MODEL = 1024
BATCH = 4
SEQ = 4096
DEPTH = 2
N_TOK = BATCH * SEQ
CHUNK = 64
HEAD_DIM = 64
N_HEADS = 4
NORM_EPS = 1e-6
A_LEFT_CHUNKS = 8
A_MAX_REL = 128
B_Q_LORA = 256
B_KV_LORA = 128
B_NOPE = 64
B_ROPE = 32
B_V = 64
ROPE_THETA = 10000.0
CONV_CH = 256
CONV_K = 31
BRANCH_W = 256
D_FF = 4 * D_MODEL

LANES = 128
SUBLANES = 8
HEAD_PAD = LANES
QKV_PAD = N_HEADS * HEAD_PAD
NEG = -1e30
LOG2E = 1.4426950408889634

TM = 512
TILES_PER_SEQ = SEQ // TM
T_ATT = 512
TQ_A = 256
CONV_HALO = 32
VMEM_LIMIT = 56 * 1024 * 1024

F32 = jnp.float32
BF16 = jnp.bfloat16


def _dot(a, b):
    return jnp.dot(a, b, preferred_element_type=F32)


def _dot_nt(a, b):
    return lax.dot_general(a, b, (((1,), (1,)), ((), ())), preferred_element_type=F32)


def _rms(x, g):
    return x * lax.rsqrt(jnp.mean(x * x, axis=-1, keepdims=True) + NORM_EPS) * g


def _sigmoid(x):
    return 1.0 / (1.0 + jnp.exp(-x))


def _split3(x):
    hi = x.astype(BF16).astype(F32)
    r1 = x - hi
    mid = r1.astype(BF16).astype(F32)
    return hi, mid, r1 - mid


def _const_spec(shape):
    return pl.BlockSpec(shape, lambda *_: (0,) * len(shape))


def _layer_spec(arr, l):
    nd = arr.ndim - 1
    return pl.BlockSpec((None,) + tuple(arr.shape[1:]), lambda *_: (l,) + (0,) * nd)


def _row(p):
    return p.reshape(p.shape[0], 1, p.shape[1])


def _params(*sem):
    return pltpu.CompilerParams(dimension_semantics=sem, vmem_limit_bytes=VMEM_LIMIT)


def _rope_kernel(pos_ref, invf_ref, cos_ref, sin_ref):
    half = B_ROPE // 2
    ang = invf_ref[...] * pos_ref[...].astype(F32)
    for ref, table, rest in ((cos_ref, jnp.cos(ang), 1.0), (sin_ref, jnp.sin(ang), 0.0)):
        ref[0:B_NOPE, :] = jnp.full((B_NOPE, ang.shape[1]), rest, F32)
        ref[B_NOPE:B_NOPE + half, :] = table
        ref[B_NOPE + half:B_NOPE + B_ROPE, :] = table
        ref[B_NOPE + B_ROPE:HEAD_PAD, :] = jnp.full((HEAD_PAD - B_NOPE - B_ROPE, ang.shape[1]), rest, F32)


def _rope_tables(positions):
    half = B_ROPE // 2
    inv_freq = 1.0 / (ROPE_THETA ** (jnp.arange(half, dtype=F32) / half))
    tm = 2048
    return pl.pallas_call(
        _rope_kernel,
        grid=(N_TOK // tm,),
        in_specs=[pl.BlockSpec((1, tm), lambda i: (0, i)), _const_spec((half, 1))],
        out_specs=[pl.BlockSpec((HEAD_PAD, tm), lambda i: (0, i))] * 2,
        out_shape=[jax.ShapeDtypeStruct((HEAD_PAD, N_TOK), F32)] * 2,
        compiler_params=_params("parallel"),
        name="rope_tables",
    )(positions.reshape(1, N_TOK), inv_freq.reshape(half, 1))


COL_A, COL_B, COL_C, COL_D, COL_END = 0, 768, 1536, 2304, 2816
F_COPIES = 9
AUG_PER_HEAD = 2 * F_COPIES


def _pick3(sel, parts):
    return jnp.where(sel == 0, parts[0], jnp.where(sel == 1, parts[1], parts[2]))


def _inproj_kernel(x_ref, gmix_ref, w_ref, cos_ref, sin_ref, gq_ref, wuq_ref, gkv_ref, wukv_ref, bf_ref,
                   wdw_ref, bdw_ref, gln_ref, bln_ref, sel_ref, tri_ref, e_ref, oneaug_ref, onev_ref, onesblk_ref,
                   h_ref, qa_ref, ka_ref, va_ref, qb_ref, kb_ref, vb_ref,
                   qc_ref, kc_ref, vc_ref, od_ref, carry_ref, ext_ref, shift_ref):
    i = pl.program_id(0)

    @pl.when(i % TILES_PER_SEQ == 0)
    def _():
        carry_ref[...] = jnp.zeros_like(carry_ref)
        ext_ref[0:CONV_HALO, :] = jnp.zeros((CONV_HALO, CONV_CH), F32)

    h = _rms(x_ref[...], gmix_ref[...]).astype(BF16)
    h_ref[...] = h

    pd = _dot_nt(h, w_ref[COL_D:COL_END, :])
    ext_ref[CONV_HALO:CONV_HALO + TM, :] = pd[:, 0:CONV_CH] * _sigmoid(pd[:, CONV_CH:2 * CONV_CH])
    od_ref[...] = _conv_ln_swish(ext_ref, shift_ref, wdw_ref, bdw_ref, gln_ref, bln_ref)
    ext_ref[0:CONV_HALO, :] = ext_ref[TM:TM + CONV_HALO, :]

    row = lax.broadcasted_iota(jnp.int32, (HEAD_PAD, TM), 0)
    own_rows = (row < HEAD_DIM, row >= HEAD_DIM)
    ones_blk = onesblk_ref[...]

    def pair_rows(q):
        qt = q.T
        return [jnp.where(own_rows[hd % 2], qt[(hd // 2) * HEAD_PAD:(hd // 2 + 1) * HEAD_PAD], 0.0)
                for hd in range(N_HEADS)]

    def v_rows(v):
        vt = v.T
        blocks = []
        for hd in range(N_HEADS):
            blocks += [vt[hd * HEAD_DIM:(hd + 1) * HEAD_DIM], ones_blk]
        return jnp.concatenate(blocks, axis=0).astype(BF16)

    pa = _dot_nt(h, w_ref[COL_A:COL_B, :])
    ka_ref[...] = pa[:, 256:512].astype(BF16)
    qa_ref[...] = jnp.concatenate(pair_rows(pa[:, 0:256] * LOG2E), axis=0).astype(BF16)
    va_ref[...] = v_rows(pa[:, 512:768])

    pb = _dot_nt(h, w_ref[COL_B:COL_C, :])
    cos = cos_ref[...].T
    sin = sin_ref[...].T
    cos4 = jnp.concatenate([cos] * N_HEADS, axis=-1)
    sin4 = jnp.concatenate([sin] * N_HEADS, axis=-1)
    qn = _rms(pb[:, 0:B_Q_LORA], gq_ref[...]).astype(BF16)
    qq = _dot(qn, wuq_ref[...])
    qb = qq[:, :QKV_PAD] * cos4 + qq[:, QKV_PAD:] * sin4
    qb_ref[...] = (qb * ((B_NOPE + B_ROPE) ** -0.5 * LOG2E)).T.astype(BF16)
    kvn = _rms(pb[:, B_Q_LORA:B_Q_LORA + B_KV_LORA], gkv_ref[...]).astype(BF16)
    krot = pb[:, 384:512] * cos + pb[:, 512:640] * sin
    kv = _dot(kvn, wukv_ref[...])
    kb_ref[...] = (kv[:, :QKV_PAD] + jnp.concatenate([krot] * N_HEADS, axis=-1)).astype(BF16)
    vb_ref[...] = (kv[:, QKV_PAD:] + onev_ref[...]).T.astype(BF16)

    pc = _dot_nt(h, w_ref[COL_C:COL_D, :])
    piece_a = sel_ref[0:1, :]
    piece_b = sel_ref[1:2, :]
    z = pb[:, 640:768] + bf_ref[...]
    logf = jnp.minimum(z, 0.0) - jnp.log(1.0 + jnp.exp(-jnp.abs(z)))
    part = _dot(tri_ref[...], _pick3(piece_a, _split3(logf)).astype(BF16))
    g = part * LOG2E + carry_ref[...]
    last = g[TM - SUBLANES:TM, :]
    total = last + pltpu.roll(last, LANES - 3 * N_HEADS, 1) + pltpu.roll(last, LANES - 6 * N_HEADS, 1)
    carry_ref[...] = jnp.where(sel_ref[2:3, :] == 1, total[SUBLANES - 1:], 0.0)
    aug = _dot(_pick3(piece_b, _split3(g)).astype(BF16), e_ref[...]) + oneaug_ref[...]
    kc_ref[...] = jnp.concatenate([pc[:, 256:512], aug[:, LANES:]], axis=-1).astype(BF16)
    aug_t = aug[:, :LANES].T
    q_blocks = []
    for hd, pair in enumerate(pair_rows(pc[:, 0:256] * LOG2E)):
        mine = jnp.where(row >= hd * AUG_PER_HEAD, jnp.where(row < (hd + 1) * AUG_PER_HEAD, aug_t, 0.0), 0.0)
        q_blocks += [pair, mine]
    qc_ref[...] = jnp.concatenate(q_blocks, axis=0).astype(BF16)
    vc_ref[...] = v_rows(pc[:, 512:768])


def _pad_last(w, before, after):
    return jnp.pad(w, ((0, 0),) * (w.ndim - 1) + ((before, after),))


def _pad_heads(w):
    return _pad_last(w, 0, HEAD_PAD - w.shape[-1]).reshape(w.shape[:-2] + (QKV_PAD,))


def _rot_half_cols(w):
    half = B_ROPE // 2
    return jnp.concatenate([-w[..., half:], w[..., :half]], axis=-1)


def _inproj_consts():
    tri = np.tril(np.ones((TM, TM), np.float32))
    sel = np.full((SUBLANES, LANES), 2, np.int32)
    e = np.zeros((LANES, 2 * LANES), np.float32)
    oneaug = np.zeros((1, 2 * LANES), np.float32)
    for copy in range(F_COPIES):
        for hd in range(N_HEADS):
            lane = copy * N_HEADS + hd
            sel[0, lane], sel[1, lane] = divmod(copy, 3)
            e[lane, hd * AUG_PER_HEAD + copy] = 1.0
            oneaug[0, LANES + hd * AUG_PER_HEAD + copy] = 1.0
            e[lane, LANES + hd * AUG_PER_HEAD + F_COPIES + copy] = -1.0
            oneaug[0, hd * AUG_PER_HEAD + F_COPIES + copy] = 1.0
    sel[2, :] = 0
    sel[2, :3 * N_HEADS] = 1
    onev = np.zeros((1, QKV_PAD), np.float32)
    onev[0, HEAD_DIM::HEAD_PAD] = 1.0
    ones_blk = np.zeros((HEAD_DIM, TM), np.float32)
    ones_blk[0] = 1.0
    return (jnp.asarray(sel), jnp.asarray(tri, BF16), jnp.asarray(e, BF16), jnp.asarray(oneaug),
            jnp.asarray(onev), jnp.asarray(ones_blk))


def _inproj_weights(w_in, w_uq, w_ukv, b_forget):
    wt = jnp.swapaxes(w_in, 1, 2).astype(BF16)
    o = 0
    cols = {}
    for name, width in (("aq", 256), ("ak", 256), ("av", 256), ("bq", B_Q_LORA), ("bkv", B_KV_LORA),
                        ("bkr", B_ROPE), ("cq", 256), ("ck", 256), ("cv", 256), ("cf", N_HEADS),
                        ("dv", CONV_CH), ("dg", CONV_CH)):
        cols[name] = wt[:, o:o + width, :]
        o += width
    scale = HEAD_DIM ** -0.5

    def pad_rows(w, before, after):
        return jnp.pad(w, ((0, 0), (before, after), (0, 0)))

    def rope_block(w):
        return pad_rows(w, B_NOPE, HEAD_PAD - B_NOPE - B_ROPE)

    half = B_ROPE // 2
    bkr_rot = jnp.concatenate([-cols["bkr"][:, half:], cols["bkr"][:, :half]], axis=1)
    cf = pad_rows(jnp.tile(cols["cf"], (1, F_COPIES, 1)), 0, LANES - F_COPIES * N_HEADS)
    w = jnp.concatenate([cols["aq"] * scale, cols["ak"], cols["av"],
                         cols["bq"], cols["bkv"], rope_block(cols["bkr"]), rope_block(bkr_rot), cf,
                         cols["cq"] * scale, cols["ck"], cols["cv"],
                         cols["dv"], cols["dg"]], axis=1)

    uq = w_uq.reshape(DEPTH, B_Q_LORA, N_HEADS, B_NOPE + B_ROPE)
    uq_rot = jnp.concatenate([jnp.zeros_like(uq[..., :B_NOPE]), _rot_half_cols(uq[..., B_NOPE:])], axis=-1)
    wuq = jnp.concatenate([_pad_heads(uq), _pad_heads(uq_rot)], axis=-1).astype(BF16)
    ukv = w_ukv.reshape(DEPTH, B_KV_LORA, N_HEADS, B_NOPE + B_V)
    wukv = jnp.concatenate([_pad_heads(ukv[..., :B_NOPE]), _pad_heads(ukv[..., B_NOPE:])], axis=-1).astype(BF16)
    bf = _row(_pad_last(jnp.tile(b_forget, (1, F_COPIES)), 0, LANES - F_COPIES * N_HEADS))
    return w, wuq, wukv, bf


def _inproj(l, x2, g_mix, w, wuq, wukv, bf, g_q_lat, g_kv_lat, cos_t, sin_t, conv_params, consts):
    assert w.shape == (DEPTH, COL_END, D_MODEL)
    row = lambda w: pl.BlockSpec((TM, w), lambda i: (i, 0))
    layer = lambda a: (a, _layer_spec(a, l))
    rope_spec = pl.BlockSpec((HEAD_PAD, TM), lambda i: (0, i))
    ins = [
        (x2, row(D_MODEL)), layer(g_mix), layer(w), (cos_t, rope_spec), (sin_t, rope_spec),
        layer(g_q_lat), layer(wuq), layer(g_kv_lat), layer(wukv), layer(bf),
    ] + [layer(p) for p in conv_params] + [(c, None) for c in consts]
    args = [a for a, _ in ins]
    specs = [s if s is not None else _const_spec(a.shape) for a, s in ins]

    def rows(w, dt):
        return jax.ShapeDtypeStruct((N_TOK, w), dt), row(w)

    def slab(nrows):
        return (jax.ShapeDtypeStruct((N_TOK // TM, nrows, TM), BF16),
                pl.BlockSpec((None, nrows, TM), lambda i: (i, 0, 0)))

    outs = [rows(D_MODEL, BF16), slab(QKV_PAD), rows(256, BF16), slab(QKV_PAD),
            slab(QKV_PAD), rows(QKV_PAD, BF16), slab(QKV_PAD),
            slab(2 * QKV_PAD), rows(3 * HEAD_PAD, BF16), slab(QKV_PAD), rows(CONV_CH, BF16)]
    return pl.pallas_call(
        _inproj_kernel,
        grid=(N_TOK // TM,),
        in_specs=specs,
        out_specs=[s for _, s in outs],
        out_shape=[a for a, _ in outs],
        scratch_shapes=[pltpu.VMEM((1, LANES), F32)] + [pltpu.VMEM((CONV_HALO + TM, CONV_CH), F32)] * 2,
        compiler_params=_params("arbitrary"),
        name="inproj",
    )(*args)


def _band_kernel(qt_ref, kp_ref, kc_ref, vtp_ref, vtc_ref, tbl_ref, o_ref):
    i = pl.program_id(1)
    w = TQ_A
    k_blocks = [(kp_ref, 0), (kp_ref, 1), (kc_ref, 0), (kc_ref, 1)]
    vt_blocks = [(vtp_ref, 0), (vtp_ref, 1), (vtc_ref, 0), (vtc_ref, 1)]

    def scores(t, hd):
        pair = slice((hd // 2) * HEAD_PAD, (hd // 2 + 1) * HEAD_PAD)
        qh = qt_ref[hd * HEAD_PAD:(hd + 1) * HEAD_PAD, t * w:(t + 1) * w]
        s = []
        for j in range(3):
            ref, half = k_blocks[t + j]
            sj = _dot(ref[half * w:(half + 1) * w, pair], qh) + tbl_ref[j, hd]
            if t + j < 2:
                sj = jnp.where(i > 0, sj, NEG)
            s.append(sj)
        return s

    def softmax(s):
        m = jnp.maximum(jnp.maximum(s[0].max(0, keepdims=True), s[1].max(0, keepdims=True)),
                        s[2].max(0, keepdims=True))
        return [jnp.exp2(sj - m).astype(BF16) for sj in s]

    def weighted_sum(t, hd, p):
        rows = slice(hd * HEAD_PAD, (hd + 1) * HEAD_PAD)
        acc = None
        for j in range(3):
            ref, half = vt_blocks[t + j]
            part = _dot(ref[rows, half * w:(half + 1) * w], p[j])
            acc = part if acc is None else acc + part
        return acc[:HEAD_DIM, :] / acc[HEAD_DIM:HEAD_DIM + 1, :]

    units = [(t, hd) for t in range(TM // w) for hd in range(N_HEADS)]
    s, p, outs = {}, {}, {}
    for tick in range(len(units) + 2):
        if tick < len(units):
            s[tick] = scores(*units[tick])
        if 0 <= tick - 1 < len(units):
            p[tick - 1] = softmax(s.pop(tick - 1))
        if 0 <= tick - 2 < len(units):
            outs[units[tick - 2]] = weighted_sum(*units[tick - 2], p.pop(tick - 2))
    for t in range(TM // w):
        o_ref[t * w:(t + 1) * w, :] = jnp.concatenate(
            [outs[(t, hd)] for hd in range(N_HEADS)], axis=0).T.astype(BF16)


def _band_table(rel_bias):
    w = TQ_A
    key = np.arange(w)[:, None]
    qry = np.arange(w)[None, :]
    diags, valid = [], []
    for j in range(3):
        rel = (2 - j) * w + (np.arange(2 * w) - (w - 1))
        diags.append(rel_bias[..., np.clip(rel, -A_MAX_REL, A_MAX_REL) + A_MAX_REL])
        cd = (2 - j) * (w // CHUNK) + qry // CHUNK - key // CHUNK
        valid.append((cd >= 0) & (cd <= A_LEFT_CHUNKS))
    diags = jnp.stack(diags, axis=1)
    valid = jnp.asarray(np.stack(valid), F32)
    return pl.pallas_call(
        _band_table_kernel,
        grid=(DEPTH, 3),
        in_specs=[pl.BlockSpec((None, None, N_HEADS, 2 * w), lambda l, j: (l, j, 0, 0)),
                  pl.BlockSpec((None, w, w), lambda l, j: (j, 0, 0))],
        out_specs=pl.BlockSpec((None, None, N_HEADS, w, w), lambda l, j: (l, j, 0, 0, 0)),
        out_shape=jax.ShapeDtypeStruct((DEPTH, 3, N_HEADS, w, w), F32),
        compiler_params=_params("parallel", "parallel"),
        name="band_table",
    )(diags, valid)


def _band_table_kernel(diag_ref, valid_ref, o_ref):
    for hd in range(N_HEADS):
        rows = jnp.broadcast_to(diag_ref[hd:hd + 1, :], (TQ_A, 2 * TQ_A))
        toeplitz = pltpu.roll(rows, TQ_A + 1, 1, stride=1, stride_axis=0)[:, :TQ_A]
        o_ref[hd] = jnp.where(valid_ref[...] > 0.0, toeplitz * LOG2E, NEG)


def _band_attention(l, qat, ka, vat, tbl):
    assert TM == 2 * TQ_A
    nt = TILES_PER_SEQ

    def cur(b, i):
        return b * nt + i

    def prev(b, i):
        return b * nt + jnp.maximum(i - 1, 0)

    def slab_spec(which):
        return pl.BlockSpec((None, QKV_PAD, TM), lambda b, i: (which(b, i), 0, 0))

    def k_spec(which):
        return pl.BlockSpec((TM, 256), lambda b, i: (which(b, i), 0))

    return pl.pallas_call(
        _band_kernel,
        grid=(BATCH, nt),
        in_specs=[slab_spec(cur), k_spec(prev), k_spec(cur), slab_spec(prev), slab_spec(cur), _layer_spec(tbl, l)],
        out_specs=pl.BlockSpec((TM, 256), lambda b, i: (b * nt + i, 0)),
        out_shape=jax.ShapeDtypeStruct((N_TOK, 256), BF16),
        compiler_params=_params("parallel", "parallel"),
        name="band_attention",
    )(qat, ka, ka, vat, vat, tbl)


def _flash_kernel(qt_ref, k_ref, vt_ref, o_ref, *, unit, pair_layout):
    i = pl.program_id(1)
    t = T_ATT
    key = lax.broadcasted_iota(jnp.int32, (t, t), 0)
    last_visible = lax.broadcasted_iota(jnp.int32, (1, t), 1) | (unit - 1)
    visible = key <= last_visible
    qrows = 2 * HEAD_PAD if pair_layout else HEAD_PAD

    def k_tile(k_ref, off, hd):
        if not pair_layout:
            return k_ref[pl.ds(off, t), hd * HEAD_PAD:(hd + 1) * HEAD_PAD]
        pair = hd // 2
        return jnp.concatenate([k_ref[pl.ds(off, t), pair * HEAD_PAD:(pair + 1) * HEAD_PAD],
                                k_ref[pl.ds(off, t), 2 * HEAD_PAD:3 * HEAD_PAD]], axis=1)

    def step(tiles, carry):
        heads = [slice(hd * HEAD_PAD, (hd + 1) * HEAD_PAD) for hd in range(N_HEADS)]
        m = [c[0] for c in carry]
        acc = [c[1] for c in carry]

        def scores(j, diag, hd):
            s = _dot(k_tile(k_ref, pl.multiple_of(j * t, t), hd), qt_ref[hd * qrows:(hd + 1) * qrows, :])
            return jnp.where(visible, s, NEG) if diag else s

        def softmax(hd, s):
            m_new = jnp.maximum(m[hd], s.max(0, keepdims=True))
            alpha = jnp.exp2(m[hd] - m_new)
            m[hd] = m_new
            return alpha, jnp.exp2(s - m_new).astype(BF16)

        def update(j, hd, alpha, p):
            acc[hd] = alpha * acc[hd] + _dot(vt_ref[j, heads[hd], :], p)

        units = [(j, diag, hd) for j, diag in tiles for hd in range(N_HEADS)]
        s, sm = {}, {}
        for tick in range(len(units) + 2):
            if tick < len(units):
                s[tick] = scores(*units[tick])
            if 0 <= tick - 1 < len(units):
                sm[tick - 1] = softmax(units[tick - 1][2], s.pop(tick - 1))
            if 0 <= tick - 2 < len(units):
                update(units[tick - 2][0], units[tick - 2][2], *sm.pop(tick - 2))
        return tuple(zip(m, acc))

    init = tuple((jnp.full((1, t), NEG, F32), jnp.zeros((HEAD_PAD, t), F32)) for _ in range(N_HEADS))
    carry = lax.fori_loop(0, i // 2, lambda jj, c: step([(2 * jj, False), (2 * jj + 1, False)], c), init)
    carry = lax.cond(i % 2 == 1,
                     lambda c: step([(i - 1, False), (i, True)], c),
                     lambda c: step([(i, True)], c), carry)
    outs = [acc[:HEAD_DIM, :] / acc[HEAD_DIM:HEAD_DIM + 1, :] for _, acc in carry]
    o_ref[...] = jnp.concatenate(outs, axis=0).T.astype(BF16)


def _flash_attention(qt, k, vt, unit, name):
    nt = SEQ // T_ATT
    pair_layout = k.shape[1] == 3 * HEAD_PAD
    return pl.pallas_call(
        functools.partial(_flash_kernel, unit=unit, pair_layout=pair_layout),
        grid=(BATCH, nt),
        in_specs=[pl.BlockSpec((None, qt.shape[1], T_ATT), lambda b, i: (b * nt + i, 0, 0)),
                  pl.BlockSpec((SEQ, k.shape[1]), lambda b, i: (b, 0)),
                  pl.BlockSpec((nt, QKV_PAD, T_ATT), lambda b, i: (b, 0, 0))],
        out_specs=pl.BlockSpec((T_ATT, 256), lambda b, i: (b * nt + i, 0)),
        out_shape=jax.ShapeDtypeStruct((N_TOK, 256), BF16),
        compiler_params=_params("parallel", "parallel"),
        name=name,
    )(qt, k, vt)


def _conv_ln_swish(ext_ref, shift_ref, w_ref, b_ref, g_ref, beta_ref):
    acc = jnp.zeros((TM, CONV_CH), F32) + b_ref[...]
    first = CONV_HALO - (CONV_K - 1)
    for res in range(SUBLANES):
        taps = [kk for kk in range(CONV_K) if (first + kk) % SUBLANES == res]
        span = max((first + kk) // SUBLANES for kk in taps) * SUBLANES + TM
        shift_ref[0:span, :] = ext_ref[res:res + span, :]
        for kk in taps:
            start = (first + kk) // SUBLANES * SUBLANES
            acc = acc + shift_ref[start:start + TM, :] * w_ref[kk:kk + 1, :]
    mu = jnp.mean(acc, axis=-1, keepdims=True)
    xc = acc - mu
    var = jnp.mean(xc * xc, axis=-1, keepdims=True)
    y = xc * lax.rsqrt(var + NORM_EPS) * g_ref[...] + beta_ref[...]
    return (y * _sigmoid(y)).astype(BF16)


def _merge_kernel(x_ref, h_ref, oa_ref, ob_ref, oc_ref, od_ref, wg_ref, bg_ref, wbr_ref, wo_ref, out_ref):
    h = h_ref[...]
    acc = jnp.zeros((TM, D_MODEL), F32)
    for n, o_ref in enumerate((oa_ref, ob_ref, oc_ref, od_ref)):
        sl = slice(n * D_MODEL, (n + 1) * D_MODEL)
        gate = _sigmoid(_dot(h, wg_ref[:, sl]) + bg_ref[:, sl])
        acc = acc + gate * _dot(o_ref[...], wbr_ref[n])
    out_ref[...] = x_ref[...] + _dot(acc.astype(BF16), wo_ref[...])


def _merge(l, x2, h, oa, ob, oc, od, wg, bg, wbr, wo):
    row = lambda w: pl.BlockSpec((TM, w), lambda i: (i, 0))
    return pl.pallas_call(
        _merge_kernel,
        grid=(N_TOK // TM,),
        in_specs=[row(D_MODEL), row(D_MODEL)] + [row(BRANCH_W)] * 4
                 + [_layer_spec(p, l) for p in (wg, bg, wbr, wo)],
        out_specs=row(D_MODEL),
        out_shape=jax.ShapeDtypeStruct((N_TOK, D_MODEL), F32),
        compiler_params=_params("parallel"),
        name="merge",
    )(x2, h, oa, ob, oc, od, wg, bg, wbr, wo)


def _ffn_kernel(x_ref, g_ref, wup_ref, wdn_ref, gfin_ref, out_ref, *, final):
    x = x_ref[...]
    hf = _rms(x, g_ref[...]).astype(BF16)
    acc = x
    for c in range(D_FF // D_MODEL):
        sl = slice(c * D_MODEL, (c + 1) * D_MODEL)
        up = jnp.maximum(_dot(hf, wup_ref[:, sl]), 0.0)
        acc = acc + _dot((up * up).astype(BF16), wdn_ref[sl, :])
    if final:
        acc = _rms(acc, gfin_ref[...])
    out_ref[...] = acc


def _ffn(l, x2, g_ffn, wup, wdn, g_final, final):
    row = pl.BlockSpec((TM, D_MODEL), lambda i: (i, 0))
    return pl.pallas_call(
        functools.partial(_ffn_kernel, final=final),
        grid=(N_TOK // TM,),
        in_specs=[row] + [_layer_spec(p, l) for p in (g_ffn, wup, wdn)] + [_const_spec((1, D_MODEL))],
        out_specs=row,
        out_shape=jax.ShapeDtypeStruct((N_TOK, D_MODEL), F32),
        compiler_params=_params("parallel"),
        name="ffn",
    )(x2, g_ffn, wup, wdn, g_final)


def kernel(x, positions, g_mix, w_in, w_gate, b_gate, rel_bias, g_q_lat, w_uq, g_kv_lat, w_ukv, b_forget,
           w_dw, b_dw, g_conv_ln, b_conv_ln, w_branch, w_o, g_ffn, w_up, w_down, g_final):
    x2 = x.reshape(N_TOK, D_MODEL)
    cos_t, sin_t = _rope_tables(positions)
    consts = _inproj_consts()
    w, wuq, wukv, bf = _inproj_weights(w_in, w_uq, w_ukv, b_forget)
    tbl = _band_table(rel_bias)
    w_taps = jnp.pad(w_dw, ((0, 0), (0, 1), (0, 0)))
    wg, wbr, wo, wup, wdn = (p.astype(BF16) for p in (w_gate, w_branch, w_o, w_up, w_down))
    conv_params = (w_taps, _row(b_dw), _row(g_conv_ln), _row(b_conv_ln))
    for l in range(DEPTH):
        (h, qa, ka, va, qb, kb, vb, qc, kc, vc, od) = _inproj(
            l, x2, _row(g_mix), w, wuq, wukv, bf, _row(g_q_lat), _row(g_kv_lat), cos_t, sin_t, conv_params, consts)
        oa = _band_attention(l, qa, ka, va, tbl)
        ob = _flash_attention(qb, kb, vb, CHUNK, "flash_b")
        oc = _flash_attention(qc, kc, vc, 1, "flash_c")
        x2 = _merge(l, x2, h, oa, ob, oc, od, wg, _row(b_gate), wbr, wo)
        x2 = _ffn(l, x2, _row(g_ffn), wup, wdn, g_final.reshape(1, -1), final=(l == DEPTH - 1))
    return x2.reshape(BATCH, SEQ, D_MODEL)
```

```python
import functools

import numpy as np
import jax
import jax.numpy as jnp
from jax import lax
from jax.experimental import pallas as pl
from jax.experimental.pallas import tpu as pltpu

D_MODEL = 1024
BATCH = 4
SEQ = 4096
DEPTH = 2
N_TOK = BATCH * SEQ
CHUNK = 64
HEAD_DIM = 64
N_HEADS = 4
NORM_EPS = 1e-6
A_LEFT_CHUNKS = 8
A_MAX_REL = 128
B_Q_LORA = 256
B_KV_LORA = 128
B_NOPE = 64
B_ROPE = 32
B_V = 64
ROPE_THETA = 10000.0
CONV_CH = 256
CONV_K = 31
BRANCH_W = 256
D_FF = 4 * D_MODEL

LANES = 128
SUBLANES = 8
HEAD_PAD = LANES
QKV_PAD = N_HEADS * HEAD_PAD
NEG = -1e30
LOG2E = 1.4426950408889634

TM = 512
TILES_PER_SEQ = SEQ // TM
T_ATT = 512
TQ_A = 256
CONV_HALO = 32
VMEM_LIMIT = 56 * 1024 * 1024

F32 = jnp.float32
BF16 = jnp.bfloat16


def _dot(a, b):
    return jnp.dot(a, b, preferred_element_type=F32)


def _dot_nt(a, b):
    return lax.dot_general(a, b, (((1,), (1,)), ((), ())), preferred_element_type=F32)


def _rms(x, g):
    return x * lax.rsqrt(jnp.mean(x * x, axis=-1, keepdims=True) + NORM_EPS) * g


def _sigmoid(x):
    return 1.0 / (1.0 + jnp.exp(-x))


def _split3(x):
    hi = x.astype(BF16).astype(F32)
    r1 = x - hi
    mid = r1.astype(BF16).astype(F32)
    return hi, mid, r1 - mid


def _emit_pipeline(n, lags, stage_a, stage_b, stage_c):
    for tick in range(n + lags[1]):
        if tick < n:
            stage_a(tick)
        if 0 <= tick - lags[0] < n:
            stage_b(tick - lags[0])
        if 0 <= tick - lags[1] < n:
            stage_c(tick - lags[1])


def _const_spec(shape):
    return pl.BlockSpec(shape, lambda *_: (0,) * len(shape))


def _layer_spec(arr, l):
    nd = arr.ndim - 1
    return pl.BlockSpec((None,) + tuple(arr.shape[1:]), lambda *_: (l,) + (0,) * nd,
                        pipeline_mode=pl.Buffered(1))


def _row(p):
    return p.reshape(p.shape[0], 1, p.shape[1])


def _params(*sem):
    return pltpu.CompilerParams(dimension_semantics=sem, vmem_limit_bytes=VMEM_LIMIT)


def _rope_kernel(pos_ref, invf_ref, cos_ref, sin_ref):
    half = B_ROPE // 2
    ang = invf_ref[...] * pos_ref[...].astype(F32)
    for ref, table, rest in ((cos_ref, jnp.cos(ang), 1.0), (sin_ref, jnp.sin(ang), 0.0)):
        ref[0:B_NOPE, :] = jnp.full((B_NOPE, ang.shape[1]), rest, F32)
        ref[B_NOPE:B_NOPE + half, :] = table
        ref[B_NOPE + half:B_NOPE + B_ROPE, :] = table
        ref[B_NOPE + B_ROPE:HEAD_PAD, :] = jnp.full((HEAD_PAD - B_NOPE - B_ROPE, ang.shape[1]), rest, F32)


def _rope_tables(positions):
    half = B_ROPE // 2
    inv_freq = 1.0 / (ROPE_THETA ** (jnp.arange(half, dtype=F32) / half))
    tm = 2048
    return pl.pallas_call(
        _rope_kernel,
        grid=(N_TOK // tm,),
        in_specs=[pl.BlockSpec((1, tm), lambda i: (0, i)), _const_spec((half, 1))],
        out_specs=[pl.BlockSpec((HEAD_PAD, tm), lambda i: (0, i))] * 2,
        out_shape=[jax.ShapeDtypeStruct((HEAD_PAD, N_TOK), F32)] * 2,
        compiler_params=_params("parallel"),
        name="rope_tables",
    )(positions.reshape(1, N_TOK), inv_freq.reshape(half, 1))


COL_A, COL_B, COL_C, COL_D, COL_END = 0, 768, 1536, 2304, 2816
F_COPIES = 9
AUG_PER_HEAD = 2 * F_COPIES


def _pick3(sel, parts):
    return jnp.where(sel == 0, parts[0], jnp.where(sel == 1, parts[1], parts[2]))


def _inproj_kernel(x_ref, gmix_ref, w_ref, cos_ref, sin_ref, gq_ref, wuq_ref, gkv_ref, wukv_ref, bf_ref,
                   wdw_ref, bdw_ref, gln_ref, bln_ref, sel_ref, tri_ref, e_ref, oneaug_ref, onev_ref, onesblk_ref,
                   h_ref, qa_ref, ka_ref, va_ref, qb_ref, kb_ref, vb_ref,
                   qc_ref, kc_ref, vc_ref, od_ref, carry_ref, ext_ref, shift_ref):
    i = pl.program_id(0)

    @pl.when(i % TILES_PER_SEQ == 0)
    def _():
        carry_ref[...] = jnp.zeros_like(carry_ref)
        ext_ref[0:CONV_HALO, :] = jnp.zeros((CONV_HALO, CONV_CH), F32)

    h = _rms(x_ref[...], gmix_ref[...]).astype(BF16)
    h_ref[...] = h

    pd = _dot_nt(h, w_ref[COL_D:COL_END, :])
    ext_ref[CONV_HALO:CONV_HALO + TM, :] = pd[:, 0:CONV_CH] * _sigmoid(pd[:, CONV_CH:2 * CONV_CH])
    od_ref[...] = _conv_ln_swish(ext_ref, shift_ref, wdw_ref, bdw_ref, gln_ref, bln_ref)
    ext_ref[0:CONV_HALO, :] = ext_ref[TM:TM + CONV_HALO, :]

    row = lax.broadcasted_iota(jnp.int32, (HEAD_PAD, TM), 0)
    own_rows = (row < HEAD_DIM, row >= HEAD_DIM)
    ones_blk = onesblk_ref[...]

    def pair_rows(q):
        qt = q.T
        return [jnp.where(own_rows[hd % 2], qt[(hd // 2) * HEAD_PAD:(hd // 2 + 1) * HEAD_PAD], 0.0)
                for hd in range(N_HEADS)]

    def v_rows(v):
        vt = v.T
        blocks = []
        for hd in range(N_HEADS):
            blocks += [vt[hd * HEAD_DIM:(hd + 1) * HEAD_DIM], ones_blk]
        return jnp.concatenate(blocks, axis=0).astype(BF16)

    pa = _dot_nt(h, w_ref[COL_A:COL_B, :])
    ka_ref[...] = pa[:, 256:512].astype(BF16)
    qa_ref[...] = jnp.concatenate(pair_rows(pa[:, 0:256] * LOG2E), axis=0).astype(BF16)
    va_ref[...] = v_rows(pa[:, 512:768])

    pb = _dot_nt(h, w_ref[COL_B:COL_C, :])
    cos = cos_ref[...].T
    sin = sin_ref[...].T
    cos4 = jnp.concatenate([cos] * N_HEADS, axis=-1)
    sin4 = jnp.concatenate([sin] * N_HEADS, axis=-1)
    qn = _rms(pb[:, 0:B_Q_LORA], gq_ref[...]).astype(BF16)
    qq = _dot(qn, wuq_ref[...])
    qb = qq[:, :QKV_PAD] * cos4 + qq[:, QKV_PAD:] * sin4
    qb_ref[...] = (qb * ((B_NOPE + B_ROPE) ** -0.5 * LOG2E)).T.astype(BF16)
    kvn = _rms(pb[:, B_Q_LORA:B_Q_LORA + B_KV_LORA], gkv_ref[...]).astype(BF16)
    krot = pb[:, 384:512] * cos + pb[:, 512:640] * sin
    kv = _dot(kvn, wukv_ref[...])
    kb_ref[...] = (kv[:, :QKV_PAD] + jnp.concatenate([krot] * N_HEADS, axis=-1)).astype(BF16)
    vb_ref[...] = (kv[:, QKV_PAD:] + onev_ref[...]).T.astype(BF16)

    pc = _dot_nt(h, w_ref[COL_C:COL_D, :])
    piece_a = sel_ref[0:1, :]
    piece_b = sel_ref[1:2, :]
    z = pb[:, 640:768] + bf_ref[...]
    logf = jnp.minimum(z, 0.0) - jnp.log(1.0 + jnp.exp(-jnp.abs(z)))
    part = _dot(tri_ref[...], _pick3(piece_a, _split3(logf)).astype(BF16))
    g = part * LOG2E + carry_ref[...]
    last = g[TM - SUBLANES:TM, :]
    total = last + pltpu.roll(last, LANES - 3 * N_HEADS, 1) + pltpu.roll(last, LANES - 6 * N_HEADS, 1)
    carry_ref[...] = jnp.where(sel_ref[2:3, :] == 1, total[SUBLANES - 1:], 0.0)
    aug = _dot(_pick3(piece_b, _split3(g)).astype(BF16), e_ref[...]) + oneaug_ref[...]
    kc_ref[...] = jnp.concatenate([pc[:, 256:512], aug[:, LANES:]], axis=-1).astype(BF16)
    aug_t = aug[:, :LANES].T
    q_blocks = []
    for hd, pair in enumerate(pair_rows(pc[:, 0:256] * LOG2E)):
        mine = jnp.where(row >= hd * AUG_PER_HEAD, jnp.where(row < (hd + 1) * AUG_PER_HEAD, aug_t, 0.0), 0.0)
        q_blocks += [pair, mine]
    qc_ref[...] = jnp.concatenate(q_blocks, axis=0).astype(BF16)
    vc_ref[...] = v_rows(pc[:, 512:768])


def _pad_last(w, before, after):
    return jnp.pad(w, ((0, 0),) * (w.ndim - 1) + ((before, after),))


def _pad_heads(w):
    return _pad_last(w, 0, HEAD_PAD - w.shape[-1]).reshape(w.shape[:-2] + (QKV_PAD,))


def _rot_half_cols(w):
    half = B_ROPE // 2
    return jnp.concatenate([-w[..., half:], w[..., :half]], axis=-1)


def _inproj_consts():
    tri = np.tril(np.ones((TM, TM), np.float32))
    sel = np.full((SUBLANES, LANES), 2, np.int32)
    e = np.zeros((LANES, 2 * LANES), np.float32)
    oneaug = np.zeros((1, 2 * LANES), np.float32)
    for copy in range(F_COPIES):
        for hd in range(N_HEADS):
            lane = copy * N_HEADS + hd
            sel[0, lane], sel[1, lane] = divmod(copy, 3)
            e[lane, hd * AUG_PER_HEAD + copy] = 1.0
            oneaug[0, LANES + hd * AUG_PER_HEAD + copy] = 1.0
            e[lane, LANES + hd * AUG_PER_HEAD + F_COPIES + copy] = -1.0
            oneaug[0, hd * AUG_PER_HEAD + F_COPIES + copy] = 1.0
    sel[2, :] = 0
    sel[2, :3 * N_HEADS] = 1
    onev = np.zeros((1, QKV_PAD), np.float32)
    onev[0, HEAD_DIM::HEAD_PAD] = 1.0
    ones_blk = np.zeros((HEAD_DIM, TM), np.float32)
    ones_blk[0] = 1.0
    return (jnp.asarray(sel), jnp.asarray(tri, BF16), jnp.asarray(e, BF16), jnp.asarray(oneaug),
            jnp.asarray(onev), jnp.asarray(ones_blk))


def _inproj_weights(w_in, w_uq, w_ukv, b_forget):
    wt = jnp.swapaxes(w_in, 1, 2).astype(BF16)
    o = 0
    cols = {}
    for name, width in (("aq", 256), ("ak", 256), ("av", 256), ("bq", B_Q_LORA), ("bkv", B_KV_LORA),
                        ("bkr", B_ROPE), ("cq", 256), ("ck", 256), ("cv", 256), ("cf", N_HEADS),
                        ("dv", CONV_CH), ("dg", CONV_CH)):
        cols[name] = wt[:, o:o + width, :]
        o += width
    scale = HEAD_DIM ** -0.5

    def pad_rows(w, before, after):
        return jnp.pad(w, ((0, 0), (before, after), (0, 0)))

    def rope_block(w):
        return pad_rows(w, B_NOPE, HEAD_PAD - B_NOPE - B_ROPE)

    half = B_ROPE // 2
    bkr_rot = jnp.concatenate([-cols["bkr"][:, half:], cols["bkr"][:, :half]], axis=1)
    cf = pad_rows(jnp.tile(cols["cf"], (1, F_COPIES, 1)), 0, LANES - F_COPIES * N_HEADS)
    w = jnp.concatenate([cols["aq"] * scale, cols["ak"], cols["av"],
                         cols["bq"], cols["bkv"], rope_block(cols["bkr"]), rope_block(bkr_rot), cf,
                         cols["cq"] * scale, cols["ck"], cols["cv"],
                         cols["dv"], cols["dg"]], axis=1)

    uq = w_uq.reshape(DEPTH, B_Q_LORA, N_HEADS, B_NOPE + B_ROPE)
    uq_rot = jnp.concatenate([jnp.zeros_like(uq[..., :B_NOPE]), _rot_half_cols(uq[..., B_NOPE:])], axis=-1)
    wuq = jnp.concatenate([_pad_heads(uq), _pad_heads(uq_rot)], axis=-1).astype(BF16)
    ukv = w_ukv.reshape(DEPTH, B_KV_LORA, N_HEADS, B_NOPE + B_V)
    wukv = jnp.concatenate([_pad_heads(ukv[..., :B_NOPE]), _pad_heads(ukv[..., B_NOPE:])], axis=-1).astype(BF16)
    bf = _row(_pad_last(jnp.tile(b_forget, (1, F_COPIES)), 0, LANES - F_COPIES * N_HEADS))
    return w, wuq, wukv, bf


def _inproj(l, x2, g_mix, w, wuq, wukv, bf, g_q_lat, g_kv_lat, cos_t, sin_t, conv_params, consts):
    assert w.shape == (DEPTH, COL_END, D_MODEL)
    row = lambda w: pl.BlockSpec((TM, w), lambda i: (i, 0))
    layer = lambda a: (a, _layer_spec(a, l))
    rope_spec = pl.BlockSpec((HEAD_PAD, TM), lambda i: (0, i))
    ins = [
        (x2, row(D_MODEL)), layer(g_mix), layer(w), (cos_t, rope_spec), (sin_t, rope_spec),
        layer(g_q_lat), layer(wuq), layer(g_kv_lat), layer(wukv), layer(bf),
    ] + [layer(p) for p in conv_params] + [(c, None) for c in consts]
    args = [a for a, _ in ins]
    specs = [s if s is not None else _const_spec(a.shape) for a, s in ins]

    def rows(w, dt):
        return jax.ShapeDtypeStruct((N_TOK, w), dt), row(w)

    def slab(nrows):
        return (jax.ShapeDtypeStruct((N_TOK // TM, nrows, TM), BF16),
                pl.BlockSpec((None, nrows, TM), lambda i: (i, 0, 0)))

    outs = [rows(D_MODEL, BF16), slab(QKV_PAD), rows(256, BF16), slab(QKV_PAD),
            slab(QKV_PAD), rows(QKV_PAD, BF16), slab(QKV_PAD),
            slab(2 * QKV_PAD), rows(3 * HEAD_PAD, BF16), slab(QKV_PAD), rows(CONV_CH, BF16)]
    return pl.pallas_call(
        _inproj_kernel,
        grid=(N_TOK // TM,),
        in_specs=specs,
        out_specs=[s for _, s in outs],
        out_shape=[a for a, _ in outs],
        scratch_shapes=[pltpu.VMEM((1, LANES), F32)] + [pltpu.VMEM((CONV_HALO + TM, CONV_CH), F32)] * 2,
        compiler_params=_params("arbitrary"),
        name="inproj",
    )(*args)


def _band_kernel(qt_ref, kp_ref, kc_ref, vtp_ref, vtc_ref, tbl_ref, o_ref):
    i = pl.program_id(1)
    w = TQ_A
    k_blocks = [(kp_ref, 0), (kp_ref, 1), (kc_ref, 0), (kc_ref, 1)]
    vt_blocks = [(vtp_ref, 0), (vtp_ref, 1), (vtc_ref, 0), (vtc_ref, 1)]

    def scores(t, hd):
        pair = slice((hd // 2) * HEAD_PAD, (hd // 2 + 1) * HEAD_PAD)
        qh = qt_ref[hd * HEAD_PAD:(hd + 1) * HEAD_PAD, t * w:(t + 1) * w]
        s = []
        for j in range(3):
            ref, half = k_blocks[t + j]
            sj = _dot(ref[half * w:(half + 1) * w, pair], qh) + tbl_ref[j, hd]
            if t + j < 2:
                sj = jnp.where(i > 0, sj, NEG)
            s.append(sj)
        return s

    def softmax(s):
        m = jnp.maximum(jnp.maximum(s[0].max(0, keepdims=True), s[1].max(0, keepdims=True)),
                        s[2].max(0, keepdims=True))
        return [jnp.exp2(sj - m).astype(BF16) for sj in s]

    def weighted_sum(t, hd, p):
        rows = slice(hd * HEAD_PAD, (hd + 1) * HEAD_PAD)
        acc = None
        for j in range(3):
            ref, half = vt_blocks[t + j]
            part = _dot(ref[rows, half * w:(half + 1) * w], p[j])
            acc = part if acc is None else acc + part
        return acc[:HEAD_DIM, :] / acc[HEAD_DIM:HEAD_DIM + 1, :]

    units = [(t, hd) for t in range(TM // w) for hd in range(N_HEADS)]
    s, p, outs = {}, {}, {}
    _emit_pipeline(len(units), (1, 2),
                   lambda u: s.__setitem__(u, scores(*units[u])),
                   lambda u: p.__setitem__(u, softmax(s.pop(u))),
                   lambda u: outs.__setitem__(units[u], weighted_sum(*units[u], p.pop(u))))
    for t in range(TM // w):
        o_ref[t * w:(t + 1) * w, :] = jnp.concatenate(
            [outs[(t, hd)] for hd in range(N_HEADS)], axis=0).T.astype(BF16)


def _band_table(rel_bias):
    w = TQ_A
    key = np.arange(w)[:, None]
    qry = np.arange(w)[None, :]
    diags, valid = [], []
    for j in range(3):
        rel = (2 - j) * w + (np.arange(2 * w) - (w - 1))
        diags.append(rel_bias[..., np.clip(rel, -A_MAX_REL, A_MAX_REL) + A_MAX_REL])
        cd = (2 - j) * (w // CHUNK) + qry // CHUNK - key // CHUNK
        valid.append((cd >= 0) & (cd <= A_LEFT_CHUNKS))
    diags = jnp.stack(diags, axis=1)
    valid = jnp.asarray(np.stack(valid), F32)
    return pl.pallas_call(
        _band_table_kernel,
        grid=(DEPTH, 3),
        in_specs=[pl.BlockSpec((None, None, N_HEADS, 2 * w), lambda l, j: (l, j, 0, 0)),
                  pl.BlockSpec((None, w, w), lambda l, j: (j, 0, 0))],
        out_specs=pl.BlockSpec((None, None, N_HEADS, w, w), lambda l, j: (l, j, 0, 0, 0)),
        out_shape=jax.ShapeDtypeStruct((DEPTH, 3, N_HEADS, w, w), F32),
        compiler_params=_params("parallel", "parallel"),
        name="band_table",
    )(diags, valid)


def _band_table_kernel(diag_ref, valid_ref, o_ref):
    for hd in range(N_HEADS):
        rows = jnp.broadcast_to(diag_ref[hd:hd + 1, :], (TQ_A, 2 * TQ_A))
        toeplitz = pltpu.roll(rows, TQ_A + 1, 1, stride=1, stride_axis=0)[:, :TQ_A]
        o_ref[hd] = jnp.where(valid_ref[...] > 0.0, toeplitz * LOG2E, NEG)


def _band_attention(l, qat, ka, vat, tbl):
    assert TM == 2 * TQ_A
    nt = TILES_PER_SEQ

    def cur(b, i):
        return b * nt + i

    def prev(b, i):
        return b * nt + jnp.maximum(i - 1, 0)

    def slab_spec(which):
        return pl.BlockSpec((None, QKV_PAD, TM), lambda b, i: (which(b, i), 0, 0))

    def k_spec(which):
        return pl.BlockSpec((TM, 256), lambda b, i: (which(b, i), 0))

    return pl.pallas_call(
        _band_kernel,
        grid=(BATCH, nt),
        in_specs=[slab_spec(cur), k_spec(prev), k_spec(cur), slab_spec(prev), slab_spec(cur), _layer_spec(tbl, l)],
        out_specs=pl.BlockSpec((TM, 256), lambda b, i: (b * nt + i, 0)),
        out_shape=jax.ShapeDtypeStruct((N_TOK, 256), BF16),
        compiler_params=_params("parallel", "parallel"),
        name="band_attention",
    )(qat, ka, ka, vat, vat, tbl)


def _flash_kernel(qt_ref, k_ref, vt_ref, o_ref, *, unit, pair_layout, lags):
    i = pl.program_id(1)
    t = T_ATT
    key = lax.broadcasted_iota(jnp.int32, (t, t), 0)
    last_visible = lax.broadcasted_iota(jnp.int32, (1, t), 1) | (unit - 1)
    visible = key <= last_visible
    qrows = 2 * HEAD_PAD if pair_layout else HEAD_PAD

    def k_tile(k_ref, off, hd):
        if not pair_layout:
            return k_ref[pl.ds(off, t), hd * HEAD_PAD:(hd + 1) * HEAD_PAD]
        pair = hd // 2
        return jnp.concatenate([k_ref[pl.ds(off, t), pair * HEAD_PAD:(pair + 1) * HEAD_PAD],
                                k_ref[pl.ds(off, t), 2 * HEAD_PAD:3 * HEAD_PAD]], axis=1)

    def step(tiles, carry):
        heads = [slice(hd * HEAD_PAD, (hd + 1) * HEAD_PAD) for hd in range(N_HEADS)]
        m = [c[0] for c in carry]
        acc = [c[1] for c in carry]

        def scores(j, diag, hd):
            s = _dot(k_tile(k_ref, pl.multiple_of(j * t, t), hd), qt_ref[hd * qrows:(hd + 1) * qrows, :])
            return jnp.where(visible, s, NEG) if diag else s

        def softmax(hd, s):
            m_new = jnp.maximum(m[hd], s.max(0, keepdims=True))
            alpha = jnp.exp2(m[hd] - m_new)
            m[hd] = m_new
            return alpha, jnp.exp2(s - m_new).astype(BF16)

        def update(j, hd, alpha, p):
            acc[hd] = alpha * acc[hd] + _dot(vt_ref[j, heads[hd], :], p)

        units = [(j, diag, hd) for j, diag in tiles for hd in range(N_HEADS)]
        s, sm = {}, {}
        _emit_pipeline(len(units), lags,
                       lambda u: s.__setitem__(u, scores(*units[u])),
                       lambda u: sm.__setitem__(u, softmax(units[u][2], s.pop(u))),
                       lambda u: update(units[u][0], units[u][2], *sm.pop(u)))
        return tuple(zip(m, acc))

    init = tuple((jnp.full((1, t), NEG, F32), jnp.zeros((HEAD_PAD, t), F32)) for _ in range(N_HEADS))
    carry = lax.fori_loop(0, i // 2, lambda jj, c: step([(2 * jj, False), (2 * jj + 1, False)], c), init)
    carry = lax.cond(i % 2 == 1,
                     lambda c: step([(i - 1, False), (i, True)], c),
                     lambda c: step([(i, True)], c), carry)
    outs = [acc[:HEAD_DIM, :] / acc[HEAD_DIM:HEAD_DIM + 1, :] for _, acc in carry]
    o_ref[...] = jnp.concatenate(outs, axis=0).T.astype(BF16)


def _flash_attention(qt, k, vt, unit, name):
    nt = SEQ // T_ATT
    pair_layout = k.shape[1] == 3 * HEAD_PAD
    return pl.pallas_call(
        functools.partial(_flash_kernel, unit=unit, pair_layout=pair_layout,
                          lags=(2, 4) if pair_layout else (1, 2)),
        grid=(BATCH, nt),
        in_specs=[pl.BlockSpec((None, qt.shape[1], T_ATT), lambda b, i: (b * nt + i, 0, 0)),
                  pl.BlockSpec((SEQ, k.shape[1]), lambda b, i: (b, 0)),
                  pl.BlockSpec((nt, QKV_PAD, T_ATT), lambda b, i: (b, 0, 0))],
        out_specs=pl.BlockSpec((T_ATT, 256), lambda b, i: (b * nt + i, 0)),
        out_shape=jax.ShapeDtypeStruct((N_TOK, 256), BF16),
        compiler_params=_params("parallel", "parallel"),
        name=name,
    )(qt, k, vt)


def _conv_ln_swish(ext_ref, shift_ref, w_ref, b_ref, g_ref, beta_ref):
    acc = jnp.zeros((TM, CONV_CH), F32) + b_ref[...]
    first = CONV_HALO - (CONV_K - 1)
    for res in range(SUBLANES):
        taps = [kk for kk in range(CONV_K) if (first + kk) % SUBLANES == res]
        span = max((first + kk) // SUBLANES for kk in taps) * SUBLANES + TM
        shift_ref[0:span, :] = ext_ref[res:res + span, :]
        for kk in taps:
            start = (first + kk) // SUBLANES * SUBLANES
            acc = acc + shift_ref[start:start + TM, :] * w_ref[kk:kk + 1, :]
    mu = jnp.mean(acc, axis=-1, keepdims=True)
    xc = acc - mu
    var = jnp.mean(xc * xc, axis=-1, keepdims=True)
    y = xc * lax.rsqrt(var + NORM_EPS) * g_ref[...] + beta_ref[...]
    return (y * _sigmoid(y)).astype(BF16)


def _merge_kernel(x_ref, h_ref, oa_ref, ob_ref, oc_ref, od_ref, wg_ref, bg_ref, wbr_ref, wo_ref, out_ref):
    h = h_ref[...]
    acc = jnp.zeros((TM, D_MODEL), F32)
    for n, o_ref in enumerate((oa_ref, ob_ref, oc_ref, od_ref)):
        sl = slice(n * D_MODEL, (n + 1) * D_MODEL)
        gate = _sigmoid(_dot(h, wg_ref[:, sl]) + bg_ref[:, sl])
        acc = acc + gate * _dot(o_ref[...], wbr_ref[n])
    out_ref[...] = x_ref[...] + _dot(acc.astype(BF16), wo_ref[...])


def _merge(l, x2, h, oa, ob, oc, od, wg, bg, wbr, wo):
    row = lambda w: pl.BlockSpec((TM, w), lambda i: (i, 0))
    return pl.pallas_call(
        _merge_kernel,
        grid=(N_TOK // TM,),
        in_specs=[row(D_MODEL), row(D_MODEL)] + [row(BRANCH_W)] * 4
                 + [_layer_spec(p, l) for p in (wg, bg, wbr, wo)],
        out_specs=row(D_MODEL),
        out_shape=jax.ShapeDtypeStruct((N_TOK, D_MODEL), F32),
        compiler_params=_params("parallel"),
        name="merge",
    )(x2, h, oa, ob, oc, od, wg, bg, wbr, wo)


def _ffn_kernel(x_ref, g_ref, wup_ref, wdn_ref, gfin_ref, out_ref, *, final):
    x = x_ref[...]
    hf = _rms(x, g_ref[...]).astype(BF16)
    acc = x
    for c in range(D_FF // D_MODEL):
        sl = slice(c * D_MODEL, (c + 1) * D_MODEL)
        up = jnp.maximum(_dot(hf, wup_ref[:, sl]), 0.0)
        acc = acc + _dot((up * up).astype(BF16), wdn_ref[sl, :])
    if final:
        acc = _rms(acc, gfin_ref[...])
    out_ref[...] = acc


def _ffn(l, x2, g_ffn, wup, wdn, g_final, final):
    row = pl.BlockSpec((TM, D_MODEL), lambda i: (i, 0))
    return pl.pallas_call(
        functools.partial(_ffn_kernel, final=final),
        grid=(N_TOK // TM,),
        in_specs=[row] + [_layer_spec(p, l) for p in (g_ffn, wup, wdn)] + [_const_spec((1, D_MODEL))],
        out_specs=row,
        out_shape=jax.ShapeDtypeStruct((N_TOK, D_MODEL), F32),
        compiler_params=_params("parallel"),
        name="ffn",
    )(x2, g_ffn, wup, wdn, g_final)


def kernel(x, positions, g_mix, w_in, w_gate, b_gate, rel_bias, g_q_lat, w_uq, g_kv_lat, w_ukv, b_forget,
           w_dw, b_dw, g_conv_ln, b_conv_ln, w_branch, w_o, g_ffn, w_up, w_down, g_final):
    x2 = x.reshape(N_TOK, D_MODEL)
    cos_t, sin_t = _rope_tables(positions)
    consts = _inproj_consts()
    w, wuq, wukv, bf = _inproj_weights(w_in, w_uq, w_ukv, b_forget)
    tbl = _band_table(rel_bias)
    w_taps = jnp.pad(w_dw, ((0, 0), (0, 1), (0, 0)))
    wg, wbr, wo, wup, wdn = (p.astype(BF16) for p in (w_gate, w_branch, w_o, w_up, w_down))
    conv_params = (w_taps, _row(b_dw), _row(g_conv_ln), _row(b_conv_ln))
    for l in range(DEPTH):
        (h, qa, ka, va, qb, kb, vb, qc, kc, vc, od) = _inproj(
            l, x2, _row(g_mix), w, wuq, wukv, bf, _row(g_q_lat), _row(g_kv_lat), cos_t, sin_t, conv_params, consts)
        oa = _band_attention(l, qa, ka, va, tbl)
        ob = _flash_attention(qb, kb, vb, CHUNK, "flash_b")
        oc = _flash_attention(qc, kc, vc, 1, "flash_c")
        x2 = _merge(l, x2, h, oa, ob, oc, od, wg, _row(b_gate), wbr, wo)
        x2 = _ffn(l, x2, _row(g_ffn), wup, wdn, g_final.reshape(1, -1), final=(l == DEPTH - 1))
    return x2.reshape(BATCH, SEQ, D_MODEL)
```

```python
import functools

import numpy as np
import jax
import jax.numpy as jnp
from jax import lax
from jax.experimental import pallas as pl
from jax.experimental.pallas import tpu as pltpu

D_MODEL = 1024
BATCH = 4
SEQ = 4096
DEPTH = 2
N_TOK = BATCH * SEQ
CHUNK = 64
HEAD_DIM = 64
N_HEADS = 4
NORM_EPS = 1e-6
A_LEFT_CHUNKS = 8
A_MAX_REL = 128
B_Q_LORA = 256
B_KV_LORA = 128
B_NOPE = 64
B_ROPE = 32
B_V = 64
ROPE_THETA = 10000.0
CONV_CH = 256
CONV_K = 31
BRANCH_W = 256
D_FF = 4 * D_MODEL

LANES = 128
SUBLANES = 8
HEAD_PAD = LANES
QKV_PAD = N_HEADS * HEAD_PAD
NEG = -1e30
LOG2E = 1.4426950408889634

TM = 512
TILES_PER_SEQ = SEQ // TM
T_ATT = 512
TQ_A = 256
CONV_HALO = 32
MERGE_ROWS = 2 * TM
VMEM_LIMIT = 56 * 1024 * 1024

F32 = jnp.float32
BF16 = jnp.bfloat16


def _dot(a, b):
    return jnp.dot(a, b, preferred_element_type=F32)


def _dot_nt(a, b):
    return lax.dot_general(a, b, (((1,), (1,)), ((), ())), preferred_element_type=F32)


def _rms(x, g):
    return x * lax.rsqrt(jnp.mean(x * x, axis=-1, keepdims=True) + NORM_EPS) * g


def _sigmoid(x):
    return 1.0 / (1.0 + jnp.exp(-x))


def _split3(x):
    hi = x.astype(BF16).astype(F32)
    r1 = x - hi
    mid = r1.astype(BF16).astype(F32)
    return hi, mid, r1 - mid


def _emit_pipeline(n, lags, stage_a, stage_b, stage_c):
    for tick in range(n + lags[1]):
        if tick < n:
            stage_a(tick)
        if 0 <= tick - lags[0] < n:
            stage_b(tick - lags[0])
        if 0 <= tick - lags[1] < n:
            stage_c(tick - lags[1])


def _const_spec(shape):
    return pl.BlockSpec(shape, lambda *_: (0,) * len(shape))


def _layer_spec(arr, l):
    nd = arr.ndim - 1
    return pl.BlockSpec((None,) + tuple(arr.shape[1:]), lambda *_: (l,) + (0,) * nd,
                        pipeline_mode=pl.Buffered(1))


def _row(p):
    return p.reshape(p.shape[0], 1, p.shape[1])


def _params(*sem):
    return pltpu.CompilerParams(dimension_semantics=sem, vmem_limit_bytes=VMEM_LIMIT)


def _rope_kernel(pos_ref, invf_ref, cos_ref, sin_ref):
    half = B_ROPE // 2
    ang = invf_ref[...] * pos_ref[...].astype(F32)
    for ref, table, rest in ((cos_ref, jnp.cos(ang), 1.0), (sin_ref, jnp.sin(ang), 0.0)):
        ref[0:B_NOPE, :] = jnp.full((B_NOPE, ang.shape[1]), rest, F32)
        ref[B_NOPE:B_NOPE + half, :] = table
        ref[B_NOPE + half:B_NOPE + B_ROPE, :] = table
        ref[B_NOPE + B_ROPE:HEAD_PAD, :] = jnp.full((HEAD_PAD - B_NOPE - B_ROPE, ang.shape[1]), rest, F32)


def _rope_tables(positions):
    half = B_ROPE // 2
    inv_freq = 1.0 / (ROPE_THETA ** (jnp.arange(half, dtype=F32) / half))
    tm = 2048
    return pl.pallas_call(
        _rope_kernel,
        grid=(N_TOK // tm,),
        in_specs=[pl.BlockSpec((1, tm), lambda i: (0, i)), _const_spec((half, 1))],
        out_specs=[pl.BlockSpec((HEAD_PAD, tm), lambda i: (0, i))] * 2,
        out_shape=[jax.ShapeDtypeStruct((HEAD_PAD, N_TOK), F32)] * 2,
        compiler_params=_params("parallel"),
        name="rope_tables",
    )(positions.reshape(1, N_TOK), inv_freq.reshape(half, 1))


COL_A, COL_B, COL_C, COL_D, COL_END = 0, 768, 1536, 2304, 2816
F_COPIES = 9
AUG_PER_HEAD = 2 * F_COPIES


def _pick3(sel, parts):
    return jnp.where(sel == 0, parts[0], jnp.where(sel == 1, parts[1], parts[2]))


def _inproj_kernel(x_ref, gmix_ref, w_ref, cos_ref, sin_ref, gq_ref, wuq_ref, gkv_ref, wukv_ref, bf_ref,
                   wdw_ref, bdw_ref, gln_ref, bln_ref, sel_ref, tri_ref, e_ref, oneaug_ref, onev_ref, onesblk_ref,
                   h_ref, qa_ref, ka_ref, va_ref, qb_ref, kb_ref, vb_ref,
                   qc_ref, kc_ref, vc_ref, od_ref, carry_ref, ext_ref, shift_ref):
    i = pl.program_id(0)

    @pl.when(i % TILES_PER_SEQ == 0)
    def _():
        carry_ref[...] = jnp.zeros_like(carry_ref)
        ext_ref[0:CONV_HALO, :] = jnp.zeros((CONV_HALO, CONV_CH), F32)

    h = _rms(x_ref[...], gmix_ref[...]).astype(BF16)
    h_ref[...] = h

    pd = _dot_nt(h, w_ref[COL_D:COL_END, :])
    ext_ref[CONV_HALO:CONV_HALO + TM, :] = pd[:, 0:CONV_CH] * _sigmoid(pd[:, CONV_CH:2 * CONV_CH])
    od_ref[...] = _conv_ln_swish(ext_ref, shift_ref, wdw_ref, bdw_ref, gln_ref, bln_ref)
    ext_ref[0:CONV_HALO, :] = ext_ref[TM:TM + CONV_HALO, :]

    row = lax.broadcasted_iota(jnp.int32, (HEAD_PAD, TM), 0)
    own_rows = (row < HEAD_DIM, row >= HEAD_DIM)
    ones_blk = onesblk_ref[...]

    def pair_rows(q):
        qt = q.T
        return [jnp.where(own_rows[hd % 2], qt[(hd // 2) * HEAD_PAD:(hd // 2 + 1) * HEAD_PAD], 0.0)
                for hd in range(N_HEADS)]

    def v_rows(v):
        vt = v.T
        blocks = []
        for hd in range(N_HEADS):
            blocks += [vt[hd * HEAD_DIM:(hd + 1) * HEAD_DIM], ones_blk]
        return jnp.concatenate(blocks, axis=0).astype(BF16)

    pa = _dot_nt(h, w_ref[COL_A:COL_B, :])
    ka_ref[...] = pa[:, 256:512].astype(BF16)
    qa_ref[...] = jnp.concatenate(pair_rows(pa[:, 0:256] * LOG2E), axis=0).astype(BF16)
    va_ref[...] = v_rows(pa[:, 512:768])

    pb = _dot_nt(h, w_ref[COL_B:COL_C, :])
    cos = cos_ref[...].T
    sin = sin_ref[...].T
    cos4 = jnp.concatenate([cos] * N_HEADS, axis=-1)
    sin4 = jnp.concatenate([sin] * N_HEADS, axis=-1)
    qn = _rms(pb[:, 0:B_Q_LORA], gq_ref[...]).astype(BF16)
    qq = _dot(qn, wuq_ref[...])
    qb = qq[:, :QKV_PAD] * cos4 + qq[:, QKV_PAD:] * sin4
    qb_ref[...] = (qb * ((B_NOPE + B_ROPE) ** -0.5 * LOG2E)).T.astype(BF16)
    kvn = _rms(pb[:, B_Q_LORA:B_Q_LORA + B_KV_LORA], gkv_ref[...]).astype(BF16)
    krot = pb[:, 384:512] * cos + pb[:, 512:640] * sin
    kv = _dot(kvn, wukv_ref[...])
    kb_ref[...] = (kv[:, :QKV_PAD] + jnp.concatenate([krot] * N_HEADS, axis=-1)).astype(BF16)
    vb_ref[...] = (kv[:, QKV_PAD:] + onev_ref[...]).T.astype(BF16)

    pc = _dot_nt(h, w_ref[COL_C:COL_D, :])
    piece_a = sel_ref[0:1, :]
    piece_b = sel_ref[1:2, :]
    z = pb[:, 640:768] + bf_ref[...]
    logf = jnp.minimum(z, 0.0) - jnp.log(1.0 + jnp.exp(-jnp.abs(z)))
    part = _dot(tri_ref[...], _pick3(piece_a, _split3(logf)).astype(BF16))
    g = part * LOG2E + carry_ref[...]
    last = g[TM - SUBLANES:TM, :]
    total = last + pltpu.roll(last, LANES - 3 * N_HEADS, 1) + pltpu.roll(last, LANES - 6 * N_HEADS, 1)
    carry_ref[...] = jnp.where(sel_ref[2:3, :] == 1, total[SUBLANES - 1:], 0.0)
    aug = _dot(_pick3(piece_b, _split3(g)).astype(BF16), e_ref[...]) + oneaug_ref[...]
    kc_ref[...] = jnp.concatenate([pc[:, 256:512], aug[:, LANES:]], axis=-1).astype(BF16)
    aug_t = aug[:, :LANES].T
    q_blocks = []
    for hd, pair in enumerate(pair_rows(pc[:, 0:256] * LOG2E)):
        mine = jnp.where(row >= hd * AUG_PER_HEAD, jnp.where(row < (hd + 1) * AUG_PER_HEAD, aug_t, 0.0), 0.0)
        q_blocks += [pair, mine]
    qc_ref[...] = jnp.concatenate(q_blocks, axis=0).astype(BF16)
    vc_ref[...] = v_rows(pc[:, 512:768])


def _pad_last(w, before, after):
    return jnp.pad(w, ((0, 0),) * (w.ndim - 1) + ((before, after),))


def _pad_heads(w):
    return _pad_last(w, 0, HEAD_PAD - w.shape[-1]).reshape(w.shape[:-2] + (QKV_PAD,))


def _rot_half_cols(w):
    half = B_ROPE // 2
    return jnp.concatenate([-w[..., half:], w[..., :half]], axis=-1)


def _inproj_consts():
    tri = np.tril(np.ones((TM, TM), np.float32))
    sel = np.full((SUBLANES, LANES), 2, np.int32)
    e = np.zeros((LANES, 2 * LANES), np.float32)
    oneaug = np.zeros((1, 2 * LANES), np.float32)
    for copy in range(F_COPIES):
        for hd in range(N_HEADS):
            lane = copy * N_HEADS + hd
            sel[0, lane], sel[1, lane] = divmod(copy, 3)
            e[lane, hd * AUG_PER_HEAD + copy] = 1.0
            oneaug[0, LANES + hd * AUG_PER_HEAD + copy] = 1.0
            e[lane, LANES + hd * AUG_PER_HEAD + F_COPIES + copy] = -1.0
            oneaug[0, hd * AUG_PER_HEAD + F_COPIES + copy] = 1.0
    sel[2, :] = 0
    sel[2, :3 * N_HEADS] = 1
    onev = np.zeros((1, QKV_PAD), np.float32)
    onev[0, HEAD_DIM::HEAD_PAD] = 1.0
    ones_blk = np.zeros((HEAD_DIM, TM), np.float32)
    ones_blk[0] = 1.0
    return (jnp.asarray(sel), jnp.asarray(tri, BF16), jnp.asarray(e, BF16), jnp.asarray(oneaug),
            jnp.asarray(onev), jnp.asarray(ones_blk))


def _inproj_weights(w_in, w_uq, w_ukv, b_forget):
    wt = jnp.swapaxes(w_in, 1, 2).astype(BF16)
    o = 0
    cols = {}
    for name, width in (("aq", 256), ("ak", 256), ("av", 256), ("bq", B_Q_LORA), ("bkv", B_KV_LORA),
                        ("bkr", B_ROPE), ("cq", 256), ("ck", 256), ("cv", 256), ("cf", N_HEADS),
                        ("dv", CONV_CH), ("dg", CONV_CH)):
        cols[name] = wt[:, o:o + width, :]
        o += width
    scale = HEAD_DIM ** -0.5

    def pad_rows(w, before, after):
        return jnp.pad(w, ((0, 0), (before, after), (0, 0)))

    def rope_block(w):
        return pad_rows(w, B_NOPE, HEAD_PAD - B_NOPE - B_ROPE)

    half = B_ROPE // 2
    bkr_rot = jnp.concatenate([-cols["bkr"][:, half:], cols["bkr"][:, :half]], axis=1)
    cf = pad_rows(jnp.tile(cols["cf"], (1, F_COPIES, 1)), 0, LANES - F_COPIES * N_HEADS)
    w = jnp.concatenate([cols["aq"] * scale, cols["ak"], cols["av"],
                         cols["bq"], cols["bkv"], rope_block(cols["bkr"]), rope_block(bkr_rot), cf,
                         cols["cq"] * scale, cols["ck"], cols["cv"],
                         cols["dv"], cols["dg"]], axis=1)

    uq = w_uq.reshape(DEPTH, B_Q_LORA, N_HEADS, B_NOPE + B_ROPE)
    uq_rot = jnp.concatenate([jnp.zeros_like(uq[..., :B_NOPE]), _rot_half_cols(uq[..., B_NOPE:])], axis=-1)
    wuq = jnp.concatenate([_pad_heads(uq), _pad_heads(uq_rot)], axis=-1).astype(BF16)
    ukv = w_ukv.reshape(DEPTH, B_KV_LORA, N_HEADS, B_NOPE + B_V)
    wukv = jnp.concatenate([_pad_heads(ukv[..., :B_NOPE]), _pad_heads(ukv[..., B_NOPE:])], axis=-1).astype(BF16)
    bf = _row(_pad_last(jnp.tile(b_forget, (1, F_COPIES)), 0, LANES - F_COPIES * N_HEADS))
    return w, wuq, wukv, bf


def _inproj(l, x2, g_mix, w, wuq, wukv, bf, g_q_lat, g_kv_lat, cos_t, sin_t, conv_params, consts):
    assert w.shape == (DEPTH, COL_END, D_MODEL)
    row = lambda w: pl.BlockSpec((TM, w), lambda i: (i, 0))
    layer = lambda a: (a, _layer_spec(a, l))
    rope_spec = pl.BlockSpec((HEAD_PAD, TM), lambda i: (0, i))
    ins = [
        (x2, row(D_MODEL)), layer(g_mix), layer(w), (cos_t, rope_spec), (sin_t, rope_spec),
        layer(g_q_lat), layer(wuq), layer(g_kv_lat), layer(wukv), layer(bf),
    ] + [layer(p) for p in conv_params] + [(c, None) for c in consts]
    args = [a for a, _ in ins]
    specs = [s if s is not None else _const_spec(a.shape) for a, s in ins]

    def rows(w, dt):
        return jax.ShapeDtypeStruct((N_TOK, w), dt), row(w)

    def slab(nrows):
        return (jax.ShapeDtypeStruct((N_TOK // TM, nrows, TM), BF16),
                pl.BlockSpec((None, nrows, TM), lambda i: (i, 0, 0)))

    outs = [rows(D_MODEL, BF16), slab(QKV_PAD), rows(256, BF16), slab(QKV_PAD),
            slab(QKV_PAD), rows(QKV_PAD, BF16), slab(QKV_PAD),
            slab(2 * QKV_PAD), rows(3 * HEAD_PAD, BF16), slab(QKV_PAD), rows(CONV_CH, BF16)]
    return pl.pallas_call(
        _inproj_kernel,
        grid=(N_TOK // TM,),
        in_specs=specs,
        out_specs=[s for _, s in outs],
        out_shape=[a for a, _ in outs],
        scratch_shapes=[pltpu.VMEM((1, LANES), F32)] + [pltpu.VMEM((CONV_HALO + TM, CONV_CH), F32)] * 2,
        compiler_params=_params("arbitrary"),
        name="inproj",
    )(*args)


def _band_kernel(qt_ref, kp_ref, kc_ref, vtp_ref, vtc_ref, tbl_ref, o_ref):
    i = pl.program_id(1)
    w = TQ_A
    k_blocks = [(kp_ref, 0), (kp_ref, 1), (kc_ref, 0), (kc_ref, 1)]
    vt_blocks = [(vtp_ref, 0), (vtp_ref, 1), (vtc_ref, 0), (vtc_ref, 1)]

    def scores(t, hd):
        pair = slice((hd // 2) * HEAD_PAD, (hd // 2 + 1) * HEAD_PAD)
        qh = qt_ref[hd * HEAD_PAD:(hd + 1) * HEAD_PAD, t * w:(t + 1) * w]
        s = []
        for j in range(3):
            ref, half = k_blocks[t + j]
            sj = _dot(ref[half * w:(half + 1) * w, pair], qh) + tbl_ref[j, hd]
            if t + j < 2:
                sj = jnp.where(i > 0, sj, NEG)
            s.append(sj)
        return s

    def softmax(s):
        m = jnp.maximum(jnp.maximum(s[0].max(0, keepdims=True), s[1].max(0, keepdims=True)),
                        s[2].max(0, keepdims=True))
        return [jnp.exp2(sj - m).astype(BF16) for sj in s]

    def weighted_sum(t, hd, p):
        rows = slice(hd * HEAD_PAD, (hd + 1) * HEAD_PAD)
        acc = None
        for j in range(3):
            ref, half = vt_blocks[t + j]
            part = _dot(ref[rows, half * w:(half + 1) * w], p[j])
            acc = part if acc is None else acc + part
        return acc[:HEAD_DIM, :] / acc[HEAD_DIM:HEAD_DIM + 1, :]

    units = [(t, hd) for t in range(TM // w) for hd in range(N_HEADS)]
    s, p, outs = {}, {}, {}
    _emit_pipeline(len(units), (1, 2),
                   lambda u: s.__setitem__(u, scores(*units[u])),
                   lambda u: p.__setitem__(u, softmax(s.pop(u))),
                   lambda u: outs.__setitem__(units[u], weighted_sum(*units[u], p.pop(u))))
    for t in range(TM // w):
        o_ref[t * w:(t + 1) * w, :] = jnp.concatenate(
            [outs[(t, hd)] for hd in range(N_HEADS)], axis=0).T.astype(BF16)


def _band_table(rel_bias):
    w = TQ_A
    key = np.arange(w)[:, None]
    qry = np.arange(w)[None, :]
    diags, valid = [], []
    for j in range(3):
        rel = (2 - j) * w + (np.arange(2 * w) - (w - 1))
        diags.append(rel_bias[..., np.clip(rel, -A_MAX_REL, A_MAX_REL) + A_MAX_REL])
        cd = (2 - j) * (w // CHUNK) + qry // CHUNK - key // CHUNK
        valid.append((cd >= 0) & (cd <= A_LEFT_CHUNKS))
    diags = jnp.stack(diags, axis=1)
    valid = jnp.asarray(np.stack(valid), F32)
    return pl.pallas_call(
        _band_table_kernel,
        grid=(DEPTH, 3),
        in_specs=[pl.BlockSpec((None, None, N_HEADS, 2 * w), lambda l, j: (l, j, 0, 0)),
                  pl.BlockSpec((None, w, w), lambda l, j: (j, 0, 0))],
        out_specs=pl.BlockSpec((None, None, N_HEADS, w, w), lambda l, j: (l, j, 0, 0, 0)),
        out_shape=jax.ShapeDtypeStruct((DEPTH, 3, N_HEADS, w, w), F32),
        compiler_params=_params("parallel", "parallel"),
        name="band_table",
    )(diags, valid)


def _band_table_kernel(diag_ref, valid_ref, o_ref):
    for hd in range(N_HEADS):
        rows = jnp.broadcast_to(diag_ref[hd:hd + 1, :], (TQ_A, 2 * TQ_A))
        toeplitz = pltpu.roll(rows, TQ_A + 1, 1, stride=1, stride_axis=0)[:, :TQ_A]
        o_ref[hd] = jnp.where(valid_ref[...] > 0.0, toeplitz * LOG2E, NEG)


def _band_attention(l, qat, ka, vat, tbl):
    assert TM == 2 * TQ_A
    nt = TILES_PER_SEQ

    def cur(b, i):
        return b * nt + i

    def prev(b, i):
        return b * nt + jnp.maximum(i - 1, 0)

    def slab_spec(which):
        return pl.BlockSpec((None, QKV_PAD, TM), lambda b, i: (which(b, i), 0, 0))

    def k_spec(which):
        return pl.BlockSpec((TM, 256), lambda b, i: (which(b, i), 0))

    return pl.pallas_call(
        _band_kernel,
        grid=(BATCH, nt),
        in_specs=[slab_spec(cur), k_spec(prev), k_spec(cur), slab_spec(prev), slab_spec(cur), _layer_spec(tbl, l)],
        out_specs=pl.BlockSpec((TM, 256), lambda b, i: (b * nt + i, 0)),
        out_shape=jax.ShapeDtypeStruct((N_TOK, 256), BF16),
        compiler_params=_params("parallel", "parallel"),
        name="band_attention",
    )(qat, ka, ka, vat, vat, tbl)


def _flash_kernel(qt_ref, k_ref, vt_ref, o_ref, *, unit, pair_layout, lags):
    i = pl.program_id(1)
    t = T_ATT
    key = lax.broadcasted_iota(jnp.int32, (t, t), 0)
    last_visible = lax.broadcasted_iota(jnp.int32, (1, t), 1) | (unit - 1)
    visible = key <= last_visible
    qrows = 2 * HEAD_PAD if pair_layout else HEAD_PAD

    def k_tile(k_ref, off, hd):
        if not pair_layout:
            return k_ref[pl.ds(off, t), hd * HEAD_PAD:(hd + 1) * HEAD_PAD]
        pair = hd // 2
        return jnp.concatenate([k_ref[pl.ds(off, t), pair * HEAD_PAD:(pair + 1) * HEAD_PAD],
                                k_ref[pl.ds(off, t), 2 * HEAD_PAD:3 * HEAD_PAD]], axis=1)

    def step(tiles, carry):
        heads = [slice(hd * HEAD_PAD, (hd + 1) * HEAD_PAD) for hd in range(N_HEADS)]
        m = [c[0] for c in carry]
        acc = [c[1] for c in carry]

        def scores(j, diag, hd):
            s = _dot(k_tile(k_ref, pl.multiple_of(j * t, t), hd), qt_ref[hd * qrows:(hd + 1) * qrows, :])
            return jnp.where(visible, s, NEG) if diag else s

        def softmax(hd, s):
            m_new = jnp.maximum(m[hd], s.max(0, keepdims=True))
            alpha = jnp.exp2(m[hd] - m_new)
            m[hd] = m_new
            return alpha, jnp.exp2(s - m_new).astype(BF16)

        def update(j, hd, alpha, p):
            acc[hd] = alpha * acc[hd] + _dot(vt_ref[j, heads[hd], :], p)

        units = [(j, diag, hd) for j, diag in tiles for hd in range(N_HEADS)]
        s, sm = {}, {}
        _emit_pipeline(len(units), lags,
                       lambda u: s.__setitem__(u, scores(*units[u])),
                       lambda u: sm.__setitem__(u, softmax(units[u][2], s.pop(u))),
                       lambda u: update(units[u][0], units[u][2], *sm.pop(u)))
        return tuple(zip(m, acc))

    init = tuple((jnp.full((1, t), NEG, F32), jnp.zeros((HEAD_PAD, t), F32)) for _ in range(N_HEADS))
    carry = lax.fori_loop(0, i // 2, lambda jj, c: step([(2 * jj, False), (2 * jj + 1, False)], c), init)
    carry = lax.cond(i % 2 == 1,
                     lambda c: step([(i - 1, False), (i, True)], c),
                     lambda c: step([(i, True)], c), carry)
    outs = [acc[:HEAD_DIM, :] / acc[HEAD_DIM:HEAD_DIM + 1, :] for _, acc in carry]
    o_ref[...] = jnp.concatenate(outs, axis=0).T.astype(BF16)


def _flash_attention(qt, k, vt, unit, name):
    nt = SEQ // T_ATT
    pair_layout = k.shape[1] == 3 * HEAD_PAD
    return pl.pallas_call(
        functools.partial(_flash_kernel, unit=unit, pair_layout=pair_layout,
                          lags=(2, 4) if pair_layout else (1, 3)),
        grid=(BATCH, nt),
        in_specs=[pl.BlockSpec((None, qt.shape[1], T_ATT), lambda b, i: (b * nt + i, 0, 0)),
                  pl.BlockSpec((SEQ, k.shape[1]), lambda b, i: (b, 0)),
                  pl.BlockSpec((nt, QKV_PAD, T_ATT), lambda b, i: (b, 0, 0))],
        out_specs=pl.BlockSpec((T_ATT, 256), lambda b, i: (b * nt + i, 0)),
        out_shape=jax.ShapeDtypeStruct((N_TOK, 256), BF16),
        compiler_params=_params("parallel", "parallel"),
        name=name,
    )(qt, k, vt)


def _conv_ln_swish(ext_ref, shift_ref, w_ref, b_ref, g_ref, beta_ref):
    acc = jnp.zeros((TM, CONV_CH), F32) + b_ref[...]
    first = CONV_HALO - (CONV_K - 1)
    for res in range(SUBLANES):
        taps = [kk for kk in range(CONV_K) if (first + kk) % SUBLANES == res]
        span = max((first + kk) // SUBLANES for kk in taps) * SUBLANES + TM
        shift_ref[0:span, :] = ext_ref[res:res + span, :]
        for kk in taps:
            start = (first + kk) // SUBLANES * SUBLANES
            acc = acc + shift_ref[start:start + TM, :] * w_ref[kk:kk + 1, :]
    mu = jnp.mean(acc, axis=-1, keepdims=True)
    xc = acc - mu
    var = jnp.mean(xc * xc, axis=-1, keepdims=True)
    y = xc * lax.rsqrt(var + NORM_EPS) * g_ref[...] + beta_ref[...]
    return (y * _sigmoid(y)).astype(BF16)


def _merge_kernel(x_ref, h_ref, oa_ref, ob_ref, oc_ref, od_ref, wg_ref, bg_ref, wbr_ref, wo_ref, out_ref):
    halves = [slice(r * TM, (r + 1) * TM) for r in range(MERGE_ROWS // TM)]
    acc = [jnp.zeros((TM, D_MODEL), F32) for _ in halves]
    for n, o_ref in enumerate((oa_ref, ob_ref, oc_ref, od_ref)):
        sl = slice(n * D_MODEL, (n + 1) * D_MODEL)
        for r, rows in enumerate(halves):
            gate = _sigmoid(_dot(h_ref[rows, :], wg_ref[:, sl]) + bg_ref[:, sl])
            acc[r] = acc[r] + gate * _dot(o_ref[rows, :], wbr_ref[n])
    for r, rows in enumerate(halves):
        out_ref[rows, :] = x_ref[rows, :] + _dot(acc[r].astype(BF16), wo_ref[...])


def _merge(l, x2, h, oa, ob, oc, od, wg, bg, wbr, wo):
    row = lambda w: pl.BlockSpec((MERGE_ROWS, w), lambda i: (i, 0))
    return pl.pallas_call(
        _merge_kernel,
        grid=(N_TOK // MERGE_ROWS,),
        in_specs=[row(D_MODEL), row(D_MODEL)] + [row(BRANCH_W)] * 4
                 + [_layer_spec(p, l) for p in (wg, bg, wbr, wo)],
        out_specs=row(D_MODEL),
        out_shape=jax.ShapeDtypeStruct((N_TOK, D_MODEL), F32),
        compiler_params=_params("parallel"),
        name="merge",
    )(x2, h, oa, ob, oc, od, wg, bg, wbr, wo)


def _ffn_kernel(x_ref, g_ref, wup_ref, wdn_ref, gfin_ref, out_ref, *, final):
    halves = [slice(r * TM, (r + 1) * TM) for r in range(MERGE_ROWS // TM)]
    acc = [x_ref[rows, :] for rows in halves]
    hf = [_rms(a, g_ref[...]).astype(BF16) for a in acc]
    for c in range(D_FF // D_MODEL):
        sl = slice(c * D_MODEL, (c + 1) * D_MODEL)
        for r in range(len(halves)):
            up = jnp.maximum(_dot(hf[r], wup_ref[:, sl]), 0.0)
            acc[r] = acc[r] + _dot((up * up).astype(BF16), wdn_ref[sl, :])
    for r, rows in enumerate(halves):
        out_ref[rows, :] = _rms(acc[r], gfin_ref[...]) if final else acc[r]


def _ffn(l, x2, g_ffn, wup, wdn, g_final, final):
    row = pl.BlockSpec((MERGE_ROWS, D_MODEL), lambda i: (i, 0))
    return pl.pallas_call(
        functools.partial(_ffn_kernel, final=final),
        grid=(N_TOK // MERGE_ROWS,),
        in_specs=[row] + [_layer_spec(p, l) for p in (g_ffn, wup, wdn)] + [_const_spec((1, D_MODEL))],
        out_specs=row,
        out_shape=jax.ShapeDtypeStruct((N_TOK, D_MODEL), F32),
        compiler_params=_params("parallel"),
        name="ffn",
    )(x2, g_ffn, wup, wdn, g_final)


def kernel(x, positions, g_mix, w_in, w_gate, b_gate, rel_bias, g_q_lat, w_uq, g_kv_lat, w_ukv, b_forget,
           w_dw, b_dw, g_conv_ln, b_conv_ln, w_branch, w_o, g_ffn, w_up, w_down, g_final):
    x2 = x.reshape(N_TOK, D_MODEL)
    cos_t, sin_t = _rope_tables(positions)
    consts = _inproj_consts()
    w, wuq, wukv, bf = _inproj_weights(w_in, w_uq, w_ukv, b_forget)
    tbl = _band_table(rel_bias)
    w_taps = jnp.pad(w_dw, ((0, 0), (0, 1), (0, 0)))
    wg, wbr, wo, wup, wdn = (p.astype(BF16) for p in (w_gate, w_branch, w_o, w_up, w_down))
    conv_params = (w_taps, _row(b_dw), _row(g_conv_ln), _row(b_conv_ln))
    for l in range(DEPTH):
        (h, qa, ka, va, qb, kb, vb, qc, kc, vc, od) = _inproj(
            l, x2, _row(g_mix), w, wuq, wukv, bf, _row(g_q_lat), _row(g_kv_lat), cos_t, sin_t, conv_params, consts)
        oa = _band_attention(l, qa, ka, va, tbl)
        ob = _flash_attention(qb, kb, vb, CHUNK, "flash_b")
        oc = _flash_attention(qc, kc, vc, 1, "flash_c")
        x2 = _merge(l, x2, h, oa, ob, oc, od, wg, _row(b_gate), wbr, wo)
        x2 = _ffn(l, x2, _row(g_ffn), wup, wdn, g_final.reshape(1, -1), final=(l == DEPTH - 1))
    return x2.reshape(BATCH, SEQ, D_MODEL)
```

```python
import functools

import numpy as np
import jax
import jax.numpy as jnp
from jax import lax
from jax.experimental import pallas as pl
from jax.experimental.pallas import tpu as pltpu

D_MODEL = 1024
BATCH = 4
SEQ = 4096
DEPTH = 2
N_TOK = BATCH * SEQ
CHUNK = 64
HEAD_DIM = 64
N_HEADS = 4
NORM_EPS = 1e-6
A_LEFT_CHUNKS = 8
A_MAX_REL = 128
B_Q_LORA = 256
B_KV_LORA = 128
B_NOPE = 64
B_ROPE = 32
B_V = 64
ROPE_THETA = 10000.0
CONV_CH = 256
CONV_K = 31
BRANCH_W = 256
D_FF = 4 * D_MODEL

LANES = 128
SUBLANES = 8
HEAD_PAD = LANES
QKV_PAD = N_HEADS * HEAD_PAD
NEG = -1e30
LOG2E = 1.4426950408889634

TM = 512
TILES_PER_SEQ = SEQ // TM
T_ATT = 512
TQ_A = 256
CONV_HALO = 32
FLASH_KEY_SPLIT = 2
MERGE_ROWS = 2 * TM
VMEM_LIMIT = 56 * 1024 * 1024

F32 = jnp.float32
BF16 = jnp.bfloat16


def _dot(a, b):
    return jnp.dot(a, b, preferred_element_type=F32)


def _dot_nt(a, b):
    return lax.dot_general(a, b, (((1,), (1,)), ((), ())), preferred_element_type=F32)


def _rms(x, g):
    return x * lax.rsqrt(jnp.mean(x * x, axis=-1, keepdims=True) + NORM_EPS) * g


def _sigmoid(x):
    return 1.0 / (1.0 + jnp.exp(-x))


def _split3(x):
    hi = x.astype(BF16).astype(F32)
    r1 = x - hi
    mid = r1.astype(BF16).astype(F32)
    return hi, mid, r1 - mid


def _emit_pipeline(n, lags, stage_a, stage_b, stage_c):
    for tick in range(n + lags[1]):
        if tick < n:
            stage_a(tick)
        if 0 <= tick - lags[0] < n:
            stage_b(tick - lags[0])
        if 0 <= tick - lags[1] < n:
            stage_c(tick - lags[1])


def _const_spec(shape):
    return pl.BlockSpec(shape, lambda *_: (0,) * len(shape))


def _layer_spec(arr, l):
    nd = arr.ndim - 1
    return pl.BlockSpec((None,) + tuple(arr.shape[1:]), lambda *_: (l,) + (0,) * nd,
                        pipeline_mode=pl.Buffered(1))


def _row(p):
    return p.reshape(p.shape[0], 1, p.shape[1])


def _params(*sem):
    return pltpu.CompilerParams(dimension_semantics=sem, vmem_limit_bytes=VMEM_LIMIT)


def _rope_kernel(pos_ref, invf_ref, cos_ref, sin_ref):
    half = B_ROPE // 2
    ang = invf_ref[...] * pos_ref[...].astype(F32)
    for ref, table, rest in ((cos_ref, jnp.cos(ang), 1.0), (sin_ref, jnp.sin(ang), 0.0)):
        ref[0:B_NOPE, :] = jnp.full((B_NOPE, ang.shape[1]), rest, F32)
        ref[B_NOPE:B_NOPE + half, :] = table
        ref[B_NOPE + half:B_NOPE + B_ROPE, :] = table
        ref[B_NOPE + B_ROPE:HEAD_PAD, :] = jnp.full((HEAD_PAD - B_NOPE - B_ROPE, ang.shape[1]), rest, F32)


def _rope_tables(positions):
    half = B_ROPE // 2
    inv_freq = 1.0 / (ROPE_THETA ** (jnp.arange(half, dtype=F32) / half))
    tm = 2048
    return pl.pallas_call(
        _rope_kernel,
        grid=(N_TOK // tm,),
        in_specs=[pl.BlockSpec((1, tm), lambda i: (0, i)), _const_spec((half, 1))],
        out_specs=[pl.BlockSpec((HEAD_PAD, tm), lambda i: (0, i))] * 2,
        out_shape=[jax.ShapeDtypeStruct((HEAD_PAD, N_TOK), F32)] * 2,
        compiler_params=_params("parallel"),
        name="rope_tables",
    )(positions.reshape(1, N_TOK), inv_freq.reshape(half, 1))


COL_A, COL_B, COL_C, COL_D, COL_END = 0, 768, 1536, 2304, 2816
F_COPIES = 9
AUG_PER_HEAD = 2 * F_COPIES


def _pick3(sel, parts):
    return jnp.where(sel == 0, parts[0], jnp.where(sel == 1, parts[1], parts[2]))


def _inproj_kernel(x_ref, gmix_ref, w_ref, cos_ref, sin_ref, gq_ref, wuq_ref, gkv_ref, wukv_ref, bf_ref,
                   wdw_ref, bdw_ref, gln_ref, bln_ref, sel_ref, tri_ref, e_ref, oneaug_ref, onev_ref, onesblk_ref,
                   h_ref, qa_ref, ka_ref, va_ref, qb_ref, kb_ref, vb_ref,
                   qc_ref, kc_ref, vc_ref, od_ref, carry_ref, ext_ref, shift_ref):
    i = pl.program_id(0)

    @pl.when(i % TILES_PER_SEQ == 0)
    def _():
        carry_ref[...] = jnp.zeros_like(carry_ref)
        ext_ref[0:CONV_HALO, :] = jnp.zeros((CONV_HALO, CONV_CH), F32)

    h = _rms(x_ref[...], gmix_ref[...]).astype(BF16)
    h_ref[...] = h

    pd = _dot_nt(h, w_ref[COL_D:COL_END, :])
    ext_ref[CONV_HALO:CONV_HALO + TM, :] = pd[:, 0:CONV_CH] * _sigmoid(pd[:, CONV_CH:2 * CONV_CH])
    od_ref[...] = _conv_ln_swish(ext_ref, shift_ref, wdw_ref, bdw_ref, gln_ref, bln_ref)
    ext_ref[0:CONV_HALO, :] = ext_ref[TM:TM + CONV_HALO, :]

    row = lax.broadcasted_iota(jnp.int32, (HEAD_PAD, TM), 0)
    own_rows = (row < HEAD_DIM, row >= HEAD_DIM)
    ones_blk = onesblk_ref[...]

    def pair_rows(q):
        qt = q.T
        return [jnp.where(own_rows[hd % 2], qt[(hd // 2) * HEAD_PAD:(hd // 2 + 1) * HEAD_PAD], 0.0)
                for hd in range(N_HEADS)]

    def v_rows(v):
        vt = v.T
        blocks = []
        for hd in range(N_HEADS):
            blocks += [vt[hd * HEAD_DIM:(hd + 1) * HEAD_DIM], ones_blk]
        return jnp.concatenate(blocks, axis=0).astype(BF16)

    pa = _dot_nt(h, w_ref[COL_A:COL_B, :])
    ka_ref[...] = pa[:, 256:512].astype(BF16)
    qa_ref[...] = jnp.concatenate(pair_rows(pa[:, 0:256] * LOG2E), axis=0).astype(BF16)
    va_ref[...] = v_rows(pa[:, 512:768])

    pb = _dot_nt(h, w_ref[COL_B:COL_C, :])
    cos = cos_ref[...].T
    sin = sin_ref[...].T
    cos4 = jnp.concatenate([cos] * N_HEADS, axis=-1)
    sin4 = jnp.concatenate([sin] * N_HEADS, axis=-1)
    qn = _rms(pb[:, 0:B_Q_LORA], gq_ref[...]).astype(BF16)
    qq = _dot(qn, wuq_ref[...])
    qb = qq[:, :QKV_PAD] * cos4 + qq[:, QKV_PAD:] * sin4
    qb_ref[...] = (qb * ((B_NOPE + B_ROPE) ** -0.5 * LOG2E)).T.astype(BF16)
    kvn = _rms(pb[:, B_Q_LORA:B_Q_LORA + B_KV_LORA], gkv_ref[...]).astype(BF16)
    krot = pb[:, 384:512] * cos + pb[:, 512:640] * sin
    kv = _dot(kvn, wukv_ref[...])
    kb_ref[...] = (kv[:, :QKV_PAD] + jnp.concatenate([krot] * N_HEADS, axis=-1)).astype(BF16)
    vb_ref[...] = (kv[:, QKV_PAD:] + onev_ref[...]).T.astype(BF16)

    pc = _dot_nt(h, w_ref[COL_C:COL_D, :])
    piece_a = sel_ref[0:1, :]
    piece_b = sel_ref[1:2, :]
    z = pb[:, 640:768] + bf_ref[...]
    logf = jnp.minimum(z, 0.0) - jnp.log(1.0 + jnp.exp(-jnp.abs(z)))
    part = _dot(tri_ref[...], _pick3(piece_a, _split3(logf)).astype(BF16))
    g = part * LOG2E + carry_ref[...]
    last = g[TM - SUBLANES:TM, :]
    total = last + pltpu.roll(last, LANES - 3 * N_HEADS, 1) + pltpu.roll(last, LANES - 6 * N_HEADS, 1)
    carry_ref[...] = jnp.where(sel_ref[2:3, :] == 1, total[SUBLANES - 1:], 0.0)
    aug = _dot(_pick3(piece_b, _split3(g)).astype(BF16), e_ref[...]) + oneaug_ref[...]
    kc_ref[...] = jnp.concatenate([pc[:, 256:512], aug[:, LANES:]], axis=-1).astype(BF16)
    aug_t = aug[:, :LANES].T
    q_blocks = []
    for hd, pair in enumerate(pair_rows(pc[:, 0:256] * LOG2E)):
        mine = jnp.where(row >= hd * AUG_PER_HEAD, jnp.where(row < (hd + 1) * AUG_PER_HEAD, aug_t, 0.0), 0.0)
        q_blocks += [pair, mine]
    qc_ref[...] = jnp.concatenate(q_blocks, axis=0).astype(BF16)
    vc_ref[...] = v_rows(pc[:, 512:768])


def _pad_last(w, before, after):
    return jnp.pad(w, ((0, 0),) * (w.ndim - 1) + ((before, after),))


def _pad_heads(w):
    return _pad_last(w, 0, HEAD_PAD - w.shape[-1]).reshape(w.shape[:-2] + (QKV_PAD,))


def _rot_half_cols(w):
    half = B_ROPE // 2
    return jnp.concatenate([-w[..., half:], w[..., :half]], axis=-1)


def _inproj_consts():
    tri = np.tril(np.ones((TM, TM), np.float32))
    sel = np.full((SUBLANES, LANES), 2, np.int32)
    e = np.zeros((LANES, 2 * LANES), np.float32)
    oneaug = np.zeros((1, 2 * LANES), np.float32)
    for copy in range(F_COPIES):
        for hd in range(N_HEADS):
            lane = copy * N_HEADS + hd
            sel[0, lane], sel[1, lane] = divmod(copy, 3)
            e[lane, hd * AUG_PER_HEAD + copy] = 1.0
            oneaug[0, LANES + hd * AUG_PER_HEAD + copy] = 1.0
            e[lane, LANES + hd * AUG_PER_HEAD + F_COPIES + copy] = -1.0
            oneaug[0, hd * AUG_PER_HEAD + F_COPIES + copy] = 1.0
    sel[2, :] = 0
    sel[2, :3 * N_HEADS] = 1
    onev = np.zeros((1, QKV_PAD), np.float32)
    onev[0, HEAD_DIM::HEAD_PAD] = 1.0
    ones_blk = np.zeros((HEAD_DIM, TM), np.float32)
    ones_blk[0] = 1.0
    return (jnp.asarray(sel), jnp.asarray(tri, BF16), jnp.asarray(e, BF16), jnp.asarray(oneaug),
            jnp.asarray(onev), jnp.asarray(ones_blk))


def _inproj_weights(w_in, w_uq, w_ukv, b_forget):
    wt = jnp.swapaxes(w_in, 1, 2).astype(BF16)
    o = 0
    cols = {}
    for name, width in (("aq", 256), ("ak", 256), ("av", 256), ("bq", B_Q_LORA), ("bkv", B_KV_LORA),
                        ("bkr", B_ROPE), ("cq", 256), ("ck", 256), ("cv", 256), ("cf", N_HEADS),
                        ("dv", CONV_CH), ("dg", CONV_CH)):
        cols[name] = wt[:, o:o + width, :]
        o += width
    scale = HEAD_DIM ** -0.5

    def pad_rows(w, before, after):
        return jnp.pad(w, ((0, 0), (before, after), (0, 0)))

    def rope_block(w):
        return pad_rows(w, B_NOPE, HEAD_PAD - B_NOPE - B_ROPE)

    half = B_ROPE // 2
    bkr_rot = jnp.concatenate([-cols["bkr"][:, half:], cols["bkr"][:, :half]], axis=1)
    cf = pad_rows(jnp.tile(cols["cf"], (1, F_COPIES, 1)), 0, LANES - F_COPIES * N_HEADS)
    w = jnp.concatenate([cols["aq"] * scale, cols["ak"], cols["av"],
                         cols["bq"], cols["bkv"], rope_block(cols["bkr"]), rope_block(bkr_rot), cf,
                         cols["cq"] * scale, cols["ck"], cols["cv"],
                         cols["dv"], cols["dg"]], axis=1)

    uq = w_uq.reshape(DEPTH, B_Q_LORA, N_HEADS, B_NOPE + B_ROPE)
    uq_rot = jnp.concatenate([jnp.zeros_like(uq[..., :B_NOPE]), _rot_half_cols(uq[..., B_NOPE:])], axis=-1)
    wuq = jnp.concatenate([_pad_heads(uq), _pad_heads(uq_rot)], axis=-1).astype(BF16)
    ukv = w_ukv.reshape(DEPTH, B_KV_LORA, N_HEADS, B_NOPE + B_V)
    wukv = jnp.concatenate([_pad_heads(ukv[..., :B_NOPE]), _pad_heads(ukv[..., B_NOPE:])], axis=-1).astype(BF16)
    bf = _row(_pad_last(jnp.tile(b_forget, (1, F_COPIES)), 0, LANES - F_COPIES * N_HEADS))
    return w, wuq, wukv, bf


def _inproj(l, x2, g_mix, w, wuq, wukv, bf, g_q_lat, g_kv_lat, cos_t, sin_t, conv_params, consts):
    assert w.shape == (DEPTH, COL_END, D_MODEL)
    row = lambda w: pl.BlockSpec((TM, w), lambda i: (i, 0))
    layer = lambda a: (a, _layer_spec(a, l))
    rope_spec = pl.BlockSpec((HEAD_PAD, TM), lambda i: (0, i))
    ins = [
        (x2, row(D_MODEL)), layer(g_mix), layer(w), (cos_t, rope_spec), (sin_t, rope_spec),
        layer(g_q_lat), layer(wuq), layer(g_kv_lat), layer(wukv), layer(bf),
    ] + [layer(p) for p in conv_params] + [(c, None) for c in consts]
    args = [a for a, _ in ins]
    specs = [s if s is not None else _const_spec(a.shape) for a, s in ins]

    def rows(w, dt):
        return jax.ShapeDtypeStruct((N_TOK, w), dt), row(w)

    def slab(nrows):
        return (jax.ShapeDtypeStruct((N_TOK // TM, nrows, TM), BF16),
                pl.BlockSpec((None, nrows, TM), lambda i: (i, 0, 0)))

    outs = [rows(D_MODEL, BF16), slab(QKV_PAD), rows(256, BF16), slab(QKV_PAD),
            slab(QKV_PAD), rows(QKV_PAD, BF16), slab(QKV_PAD),
            slab(2 * QKV_PAD), rows(3 * HEAD_PAD, BF16), slab(QKV_PAD), rows(CONV_CH, BF16)]
    return pl.pallas_call(
        _inproj_kernel,
        grid=(N_TOK // TM,),
        in_specs=specs,
        out_specs=[s for _, s in outs],
        out_shape=[a for a, _ in outs],
        scratch_shapes=[pltpu.VMEM((1, LANES), F32)] + [pltpu.VMEM((CONV_HALO + TM, CONV_CH), F32)] * 2,
        compiler_params=_params("arbitrary"),
        name="inproj",
    )(*args)


def _band_kernel(qt_ref, kp_ref, kc_ref, vtp_ref, vtc_ref, tbl_ref, o_ref):
    i = pl.program_id(1)
    w = TQ_A
    k_blocks = [(kp_ref, 0), (kp_ref, 1), (kc_ref, 0), (kc_ref, 1)]
    vt_blocks = [(vtp_ref, 0), (vtp_ref, 1), (vtc_ref, 0), (vtc_ref, 1)]

    def scores(t, hd):
        pair = slice((hd // 2) * HEAD_PAD, (hd // 2 + 1) * HEAD_PAD)
        qh = qt_ref[hd * HEAD_PAD:(hd + 1) * HEAD_PAD, t * w:(t + 1) * w]
        s = []
        for j in range(3):
            ref, half = k_blocks[t + j]
            sj = _dot(ref[half * w:(half + 1) * w, pair], qh) + tbl_ref[j, hd]
            if t + j < 2:
                sj = jnp.where(i > 0, sj, NEG)
            s.append(sj)
        return s

    def softmax(s):
        m = jnp.maximum(jnp.maximum(s[0].max(0, keepdims=True), s[1].max(0, keepdims=True)),
                        s[2].max(0, keepdims=True))
        return [jnp.exp2(sj - m).astype(BF16) for sj in s]

    def weighted_sum(t, hd, p):
        rows = slice(hd * HEAD_PAD, (hd + 1) * HEAD_PAD)
        acc = None
        for j in range(3):
            ref, half = vt_blocks[t + j]
            part = _dot(ref[rows, half * w:(half + 1) * w], p[j])
            acc = part if acc is None else acc + part
        return acc[:HEAD_DIM, :] / acc[HEAD_DIM:HEAD_DIM + 1, :]

    units = [(t, hd) for t in range(TM // w) for hd in range(N_HEADS)]
    s, p, outs = {}, {}, {}
    _emit_pipeline(len(units), (1, 2),
                   lambda u: s.__setitem__(u, scores(*units[u])),
                   lambda u: p.__setitem__(u, softmax(s.pop(u))),
                   lambda u: outs.__setitem__(units[u], weighted_sum(*units[u], p.pop(u))))
    for t in range(TM // w):
        o_ref[t * w:(t + 1) * w, :] = jnp.concatenate(
            [outs[(t, hd)] for hd in range(N_HEADS)], axis=0).T.astype(BF16)


def _band_table(rel_bias):
    w = TQ_A
    key = np.arange(w)[:, None]
    qry = np.arange(w)[None, :]
    diags, valid = [], []
    for j in range(3):
        rel = (2 - j) * w + (np.arange(2 * w) - (w - 1))
        diags.append(rel_bias[..., np.clip(rel, -A_MAX_REL, A_MAX_REL) + A_MAX_REL])
        cd = (2 - j) * (w // CHUNK) + qry // CHUNK - key // CHUNK
        valid.append((cd >= 0) & (cd <= A_LEFT_CHUNKS))
    diags = jnp.stack(diags, axis=1)
    valid = jnp.asarray(np.stack(valid), F32)
    return pl.pallas_call(
        _band_table_kernel,
        grid=(DEPTH, 3),
        in_specs=[pl.BlockSpec((None, None, N_HEADS, 2 * w), lambda l, j: (l, j, 0, 0)),
                  pl.BlockSpec((None, w, w), lambda l, j: (j, 0, 0))],
        out_specs=pl.BlockSpec((None, None, N_HEADS, w, w), lambda l, j: (l, j, 0, 0, 0)),
        out_shape=jax.ShapeDtypeStruct((DEPTH, 3, N_HEADS, w, w), F32),
        compiler_params=_params("parallel", "parallel"),
        name="band_table",
    )(diags, valid)


def _band_table_kernel(diag_ref, valid_ref, o_ref):
    for hd in range(N_HEADS):
        rows = jnp.broadcast_to(diag_ref[hd:hd + 1, :], (TQ_A, 2 * TQ_A))
        toeplitz = pltpu.roll(rows, TQ_A + 1, 1, stride=1, stride_axis=0)[:, :TQ_A]
        o_ref[hd] = jnp.where(valid_ref[...] > 0.0, toeplitz * LOG2E, NEG)


def _band_attention(l, qat, ka, vat, tbl):
    assert TM == 2 * TQ_A
    nt = TILES_PER_SEQ

    def cur(b, i):
        return b * nt + i

    def prev(b, i):
        return b * nt + jnp.maximum(i - 1, 0)

    def slab_spec(which):
        return pl.BlockSpec((None, QKV_PAD, TM), lambda b, i: (which(b, i), 0, 0))

    def k_spec(which):
        return pl.BlockSpec((TM, 256), lambda b, i: (which(b, i), 0))

    return pl.pallas_call(
        _band_kernel,
        grid=(BATCH, nt),
        in_specs=[slab_spec(cur), k_spec(prev), k_spec(cur), slab_spec(prev), slab_spec(cur), _layer_spec(tbl, l)],
        out_specs=pl.BlockSpec((TM, 256), lambda b, i: (b * nt + i, 0)),
        out_shape=jax.ShapeDtypeStruct((N_TOK, 256), BF16),
        compiler_params=_params("parallel", "parallel"),
        name="band_attention",
    )(qat, ka, ka, vat, vat, tbl)


def _flash_kernel(qt_ref, k_ref, vt_ref, o_ref, *, unit, pair_layout, lags):
    i = pl.program_id(1)
    t = T_ATT
    key = lax.broadcasted_iota(jnp.int32, (t, t), 0)
    last_visible = lax.broadcasted_iota(jnp.int32, (1, t), 1) | (unit - 1)
    visible = key <= last_visible
    qrows = 2 * HEAD_PAD if pair_layout else HEAD_PAD

    def k_tile(k_ref, off, hd, rows):
        if not pair_layout:
            return k_ref[pl.ds(off, rows), hd * HEAD_PAD:(hd + 1) * HEAD_PAD]
        pair = hd // 2
        return jnp.concatenate([k_ref[pl.ds(off, rows), pair * HEAD_PAD:(pair + 1) * HEAD_PAD],
                                k_ref[pl.ds(off, rows), 2 * HEAD_PAD:3 * HEAD_PAD]], axis=1)

    def step(tiles, carry):
        heads = [slice(hd * HEAD_PAD, (hd + 1) * HEAD_PAD) for hd in range(N_HEADS)]
        m = [c[0] for c in carry]
        acc = [c[1] for c in carry]

        tk = t // FLASH_KEY_SPLIT

        def scores(j, sub, diag, hd):
            off = pl.multiple_of(j * t, t) + sub * tk
            s = _dot(k_tile(k_ref, off, hd, tk), qt_ref[hd * qrows:(hd + 1) * qrows, :])
            return jnp.where(visible[sub * tk:(sub + 1) * tk], s, NEG) if diag else s

        def softmax(hd, s):
            m_new = jnp.maximum(m[hd], s.max(0, keepdims=True))
            alpha = jnp.exp2(m[hd] - m_new)
            m[hd] = m_new
            return alpha, jnp.exp2(s - m_new).astype(BF16)

        def update(j, sub, hd, alpha, p):
            acc[hd] = alpha * acc[hd] + _dot(vt_ref[j, heads[hd], sub * tk:(sub + 1) * tk], p)

        units = [(j, sub, diag, hd) for j, diag in tiles for sub in range(FLASH_KEY_SPLIT)
                 for hd in range(N_HEADS)]
        s, sm = {}, {}
        _emit_pipeline(len(units), lags,
                       lambda u: s.__setitem__(u, scores(*units[u])),
                       lambda u: sm.__setitem__(u, softmax(units[u][3], s.pop(u))),
                       lambda u: update(units[u][0], units[u][1], units[u][3], *sm.pop(u)))
        return tuple(zip(m, acc))

    init = tuple((jnp.full((1, t), NEG, F32), jnp.zeros((HEAD_PAD, t), F32)) for _ in range(N_HEADS))
    carry = lax.fori_loop(0, i // 2, lambda jj, c: step([(2 * jj, False), (2 * jj + 1, False)], c), init)
    carry = lax.cond(i % 2 == 1,
                     lambda c: step([(i - 1, False), (i, True)], c),
                     lambda c: step([(i, True)], c), carry)
    outs = [acc[:HEAD_DIM, :] / acc[HEAD_DIM:HEAD_DIM + 1, :] for _, acc in carry]
    o_ref[...] = jnp.concatenate(outs, axis=0).T.astype(BF16)


def _flash_attention(qt, k, vt, unit, name):
    nt = SEQ // T_ATT
    pair_layout = k.shape[1] == 3 * HEAD_PAD
    return pl.pallas_call(
        functools.partial(_flash_kernel, unit=unit, pair_layout=pair_layout,
                          lags=(2, 4) if pair_layout else (2, 6)),
        grid=(BATCH, nt),
        in_specs=[pl.BlockSpec((None, qt.shape[1], T_ATT), lambda b, i: (b * nt + i, 0, 0)),
                  pl.BlockSpec((SEQ, k.shape[1]), lambda b, i: (b, 0)),
                  pl.BlockSpec((nt, QKV_PAD, T_ATT), lambda b, i: (b, 0, 0))],
        out_specs=pl.BlockSpec((T_ATT, 256), lambda b, i: (b * nt + i, 0)),
        out_shape=jax.ShapeDtypeStruct((N_TOK, 256), BF16),
        compiler_params=_params("parallel", "parallel"),
        name=name,
    )(qt, k, vt)


def _conv_ln_swish(ext_ref, shift_ref, w_ref, b_ref, g_ref, beta_ref):
    acc = jnp.zeros((TM, CONV_CH), F32) + b_ref[...]
    first = CONV_HALO - (CONV_K - 1)
    for res in range(SUBLANES):
        taps = [kk for kk in range(CONV_K) if (first + kk) % SUBLANES == res]
        span = max((first + kk) // SUBLANES for kk in taps) * SUBLANES + TM
        shift_ref[0:span, :] = ext_ref[res:res + span, :]
        for kk in taps:
            start = (first + kk) // SUBLANES * SUBLANES
            acc = acc + shift_ref[start:start + TM, :] * w_ref[kk:kk + 1, :]
    mu = jnp.mean(acc, axis=-1, keepdims=True)
    xc = acc - mu
    var = jnp.mean(xc * xc, axis=-1, keepdims=True)
    y = xc * lax.rsqrt(var + NORM_EPS) * g_ref[...] + beta_ref[...]
    return (y * _sigmoid(y)).astype(BF16)


def _merge_kernel(x_ref, h_ref, oa_ref, ob_ref, oc_ref, od_ref, wg_ref, bg_ref, wbr_ref, wo_ref, out_ref):
    halves = [slice(r * TM, (r + 1) * TM) for r in range(MERGE_ROWS // TM)]
    acc = [jnp.zeros((TM, D_MODEL), F32) for _ in halves]
    for n, o_ref in enumerate((oa_ref, ob_ref, oc_ref, od_ref)):
        sl = slice(n * D_MODEL, (n + 1) * D_MODEL)
        for r, rows in enumerate(halves):
            gate = _sigmoid(_dot(h_ref[rows, :], wg_ref[:, sl]) + bg_ref[:, sl])
            acc[r] = acc[r] + gate * _dot(o_ref[rows, :], wbr_ref[n])
    for r, rows in enumerate(halves):
        out_ref[rows, :] = x_ref[rows, :] + _dot(acc[r].astype(BF16), wo_ref[...])


def _merge(l, x2, h, oa, ob, oc, od, wg, bg, wbr, wo):
    row = lambda w: pl.BlockSpec((MERGE_ROWS, w), lambda i: (i, 0))
    return pl.pallas_call(
        _merge_kernel,
        grid=(N_TOK // MERGE_ROWS,),
        in_specs=[row(D_MODEL), row(D_MODEL)] + [row(BRANCH_W)] * 4
                 + [_layer_spec(p, l) for p in (wg, bg, wbr, wo)],
        out_specs=row(D_MODEL),
        out_shape=jax.ShapeDtypeStruct((N_TOK, D_MODEL), F32),
        compiler_params=_params("parallel"),
        name="merge",
    )(x2, h, oa, ob, oc, od, wg, bg, wbr, wo)


def _ffn_kernel(x_ref, g_ref, wup_ref, wdn_ref, gfin_ref, out_ref, *, final):
    halves = [slice(r * TM, (r + 1) * TM) for r in range(MERGE_ROWS // TM)]
    acc = [x_ref[rows, :] for rows in halves]
    hf = [_rms(a, g_ref[...]).astype(BF16) for a in acc]
    for c in range(D_FF // D_MODEL):
        sl = slice(c * D_MODEL, (c + 1) * D_MODEL)
        for r in range(len(halves)):
            up = jnp.maximum(_dot(hf[r], wup_ref[:, sl]), 0.0)
            acc[r] = acc[r] + _dot((up * up).astype(BF16), wdn_ref[sl, :])
    for r, rows in enumerate(halves):
        out_ref[rows, :] = _rms(acc[r], gfin_ref[...]) if final else acc[r]


def _ffn(l, x2, g_ffn, wup, wdn, g_final, final):
    row = pl.BlockSpec((MERGE_ROWS, D_MODEL), lambda i: (i, 0))
    return pl.pallas_call(
        functools.partial(_ffn_kernel, final=final),
        grid=(N_TOK // MERGE_ROWS,),
        in_specs=[row] + [_layer_spec(p, l) for p in (g_ffn, wup, wdn)] + [_const_spec((1, D_MODEL))],
        out_specs=row,
        out_shape=jax.ShapeDtypeStruct((N_TOK, D_MODEL), F32),
        compiler_params=_params("parallel"),
        name="ffn",
    )(x2, g_ffn, wup, wdn, g_final)


def kernel(x, positions, g_mix, w_in, w_gate, b_gate, rel_bias, g_q_lat, w_uq, g_kv_lat, w_ukv, b_forget,
           w_dw, b_dw, g_conv_ln, b_conv_ln, w_branch, w_o, g_ffn, w_up, w_down, g_final):
    x2 = x.reshape(N_TOK, D_MODEL)
    cos_t, sin_t = _rope_tables(positions)
    consts = _inproj_consts()
    w, wuq, wukv, bf = _inproj_weights(w_in, w_uq, w_ukv, b_forget)
    tbl = _band_table(rel_bias)
    w_taps = jnp.pad(w_dw, ((0, 0), (0, 1), (0, 0)))
    wg, wbr, wo, wup, wdn = (p.astype(BF16) for p in (w_gate, w_branch, w_o, w_up, w_down))
    conv_params = (w_taps, _row(b_dw), _row(g_conv_ln), _row(b_conv_ln))
    for l in range(DEPTH):
        (h, qa, ka, va, qb, kb, vb, qc, kc, vc, od) = _inproj(
            l, x2, _row(g_mix), w, wuq, wukv, bf, _row(g_q_lat), _row(g_kv_lat), cos_t, sin_t, conv_params, consts)
        oa = _band_attention(l, qa, ka, va, tbl)
        ob = _flash_attention(qb, kb, vb, CHUNK, "flash_b")
        oc = _flash_attention(qc, kc, vc, 1, "flash_c")
        x2 = _merge(l, x2, h, oa, ob, oc, od, wg, _row(b_gate), wbr, wo)
        x2 = _ffn(l, x2, _row(g_ffn), wup, wdn, g_final.reshape(1, -1), final=(l == DEPTH - 1))
    return x2.reshape(BATCH, SEQ, D_MODEL)
```

```python
import functools

import numpy as np
import jax
import jax.numpy as jnp
from jax import lax
from jax.experimental import pallas as pl
from jax.experimental.pallas import tpu as pltpu

D_MODEL = 1024
BATCH = 4
SEQ = 4096
DEPTH = 2
N_TOK = BATCH * SEQ
CHUNK = 64
HEAD_DIM = 64
N_HEADS = 4
NORM_EPS = 1e-6
A_LEFT_CHUNKS = 8
A_MAX_REL = 128
B_Q_LORA = 256
B_KV_LORA = 128
B_NOPE = 64
B_ROPE = 32
B_V = 64
ROPE_THETA = 10000.0
CONV_CH = 256
CONV_K = 31
BRANCH_W = 256
D_FF = 4 * D_MODEL

LANES = 128
SUBLANES = 8
HEAD_PAD = LANES
QKV_PAD = N_HEADS * HEAD_PAD
NEG = -1e30
LOG2E = 1.4426950408889634

TM = 512
TILES_PER_SEQ = SEQ // TM
T_ATT = 512
TQ_A = 256
CONV_HALO = 32
FLASH_KEY_SPLIT = 2
MERGE_ROWS = 2 * TM
VMEM_LIMIT = 56 * 1024 * 1024

F32 = jnp.float32
BF16 = jnp.bfloat16


def _dot(a, b):
    return jnp.dot(a, b, preferred_element_type=F32)


def _dot_nt(a, b):
    return lax.dot_general(a, b, (((1,), (1,)), ((), ())), preferred_element_type=F32)


def _rms(x, g):
    return x * lax.rsqrt(jnp.mean(x * x, axis=-1, keepdims=True) + NORM_EPS) * g


def _sigmoid(x):
    return 1.0 / (1.0 + jnp.exp(-x))


def _split3(x):
    hi = x.astype(BF16).astype(F32)
    r1 = x - hi
    mid = r1.astype(BF16).astype(F32)
    return hi, mid, r1 - mid


def _emit_pipeline(n, lags, stage_a, stage_b, stage_c):
    for tick in range(n + lags[1]):
        if tick < n:
            stage_a(tick)
        if 0 <= tick - lags[0] < n:
            stage_b(tick - lags[0])
        if 0 <= tick - lags[1] < n:
            stage_c(tick - lags[1])


def _const_spec(shape):
    return pl.BlockSpec(shape, lambda *_: (0,) * len(shape))


def _layer_spec(arr, l):
    nd = arr.ndim - 1
    return pl.BlockSpec((None,) + tuple(arr.shape[1:]), lambda *_: (l,) + (0,) * nd,
                        pipeline_mode=pl.Buffered(1))


def _row(p):
    return p.reshape(p.shape[0], 1, p.shape[1])


def _params(*sem):
    return pltpu.CompilerParams(dimension_semantics=sem, vmem_limit_bytes=VMEM_LIMIT)


def _rope_kernel(pos_ref, invf_ref, cos_ref, sin_ref):
    half = B_ROPE // 2
    ang = invf_ref[...] * pos_ref[...].astype(F32)
    for ref, table, rest in ((cos_ref, jnp.cos(ang), 1.0), (sin_ref, jnp.sin(ang), 0.0)):
        ref[0:B_NOPE, :] = jnp.full((B_NOPE, ang.shape[1]), rest, F32)
        ref[B_NOPE:B_NOPE + half, :] = table
        ref[B_NOPE + half:B_NOPE + B_ROPE, :] = table
        ref[B_NOPE + B_ROPE:HEAD_PAD, :] = jnp.full((HEAD_PAD - B_NOPE - B_ROPE, ang.shape[1]), rest, F32)


def _rope_tables(positions):
    half = B_ROPE // 2
    inv_freq = 1.0 / (ROPE_THETA ** (jnp.arange(half, dtype=F32) / half))
    tm = 2048
    return pl.pallas_call(
        _rope_kernel,
        grid=(N_TOK // tm,),
        in_specs=[pl.BlockSpec((1, tm), lambda i: (0, i)), _const_spec((half, 1))],
        out_specs=[pl.BlockSpec((HEAD_PAD, tm), lambda i: (0, i))] * 2,
        out_shape=[jax.ShapeDtypeStruct((HEAD_PAD, N_TOK), F32)] * 2,
        compiler_params=_params("parallel"),
        name="rope_tables",
    )(positions.reshape(1, N_TOK), inv_freq.reshape(half, 1))


COL_A, COL_B, COL_C, COL_D, COL_END = 0, 768, 1536, 2304, 2816
F_COPIES = 9
AUG_PER_HEAD = 2 * F_COPIES


def _pick3(sel, parts):
    return jnp.where(sel == 0, parts[0], jnp.where(sel == 1, parts[1], parts[2]))


def _inproj_kernel(x_ref, gmix_ref, w_ref, cos_ref, sin_ref, gq_ref, wuq_ref, gkv_ref, wukv_ref, bf_ref,
                   wdw_ref, bdw_ref, gln_ref, bln_ref, sel_ref, tri_ref, e_ref, oneaug_ref, onev_ref, onesblk_ref,
                   h_ref, qa_ref, ka_ref, va_ref, qb_ref, kb_ref, vb_ref,
                   qc_ref, kc_ref, vc_ref, od_ref, carry_ref, ext_ref, shift_ref):
    i = pl.program_id(0)

    @pl.when(i % TILES_PER_SEQ == 0)
    def _():
        carry_ref[...] = jnp.zeros_like(carry_ref)
        ext_ref[0:CONV_HALO, :] = jnp.zeros((CONV_HALO, CONV_CH), F32)

    h = _rms(x_ref[...], gmix_ref[...]).astype(BF16)
    h_ref[...] = h

    pd = _dot_nt(h, w_ref[COL_D:COL_END, :])
    ext_ref[CONV_HALO:CONV_HALO + TM, :] = pd[:, 0:CONV_CH] * _sigmoid(pd[:, CONV_CH:2 * CONV_CH])
    od_ref[...] = _conv_ln_swish(ext_ref, shift_ref, wdw_ref, bdw_ref, gln_ref, bln_ref)
    ext_ref[0:CONV_HALO, :] = ext_ref[TM:TM + CONV_HALO, :]

    row = lax.broadcasted_iota(jnp.int32, (HEAD_PAD, TM), 0)
    own_rows = (row < HEAD_DIM, row >= HEAD_DIM)
    ones_blk = onesblk_ref[...]

    def pair_rows(q):
        qt = q.T
        return [jnp.where(own_rows[hd % 2], qt[(hd // 2) * HEAD_PAD:(hd // 2 + 1) * HEAD_PAD], 0.0)
                for hd in range(N_HEADS)]

    def v_rows(v):
        vt = v.T
        blocks = []
        for hd in range(N_HEADS):
            blocks += [vt[hd * HEAD_DIM:(hd + 1) * HEAD_DIM], ones_blk]
        return jnp.concatenate(blocks, axis=0).astype(BF16)

    pa = _dot_nt(h, w_ref[COL_A:COL_B, :])
    ka_ref[...] = pa[:, 256:512].astype(BF16)
    qa_ref[...] = jnp.concatenate(pair_rows(pa[:, 0:256] * LOG2E), axis=0).astype(BF16)
    va_ref[...] = v_rows(pa[:, 512:768])

    pb = _dot_nt(h, w_ref[COL_B:COL_C, :])
    cos = cos_ref[...].T
    sin = sin_ref[...].T
    cos4 = jnp.concatenate([cos] * N_HEADS, axis=-1)
    sin4 = jnp.concatenate([sin] * N_HEADS, axis=-1)
    qn = _rms(pb[:, 0:B_Q_LORA], gq_ref[...]).astype(BF16)
    qq = _dot(qn, wuq_ref[...])
    qb = qq[:, :QKV_PAD] * cos4 + qq[:, QKV_PAD:] * sin4
    qb_ref[...] = (qb * ((B_NOPE + B_ROPE) ** -0.5 * LOG2E)).T.astype(BF16)
    kvn = _rms(pb[:, B_Q_LORA:B_Q_LORA + B_KV_LORA], gkv_ref[...]).astype(BF16)
    krot = pb[:, 384:512] * cos + pb[:, 512:640] * sin
    kv = _dot(kvn, wukv_ref[...])
    kb_ref[...] = (kv[:, :QKV_PAD] + jnp.concatenate([krot] * N_HEADS, axis=-1)).astype(BF16)
    vb_ref[...] = (kv[:, QKV_PAD:] + onev_ref[...]).T.astype(BF16)

    pc = _dot_nt(h, w_ref[COL_C:COL_D, :])
    piece_a = sel_ref[0:1, :]
    piece_b = sel_ref[1:2, :]
    z = pb[:, 640:768] + bf_ref[...]
    logf = jnp.minimum(z, 0.0) - jnp.log(1.0 + jnp.exp(-jnp.abs(z)))
    part = _dot(tri_ref[...], _pick3(piece_a, _split3(logf)).astype(BF16))
    g = part * LOG2E + carry_ref[...]
    last = g[TM - SUBLANES:TM, :]
    total = last + pltpu.roll(last, LANES - 3 * N_HEADS, 1) + pltpu.roll(last, LANES - 6 * N_HEADS, 1)
    carry_ref[...] = jnp.where(sel_ref[2:3, :] == 1, total[SUBLANES - 1:], 0.0)
    aug = _dot(_pick3(piece_b, _split3(g)).astype(BF16), e_ref[...]) + oneaug_ref[...]
    kc_ref[...] = jnp.concatenate([pc[:, 256:512], aug[:, LANES:]], axis=-1).astype(BF16)
    aug_t = aug[:, :LANES].T
    q_blocks = []
    for hd, pair in enumerate(pair_rows(pc[:, 0:256] * LOG2E)):
        mine = jnp.where(row >= hd * AUG_PER_HEAD, jnp.where(row < (hd + 1) * AUG_PER_HEAD, aug_t, 0.0), 0.0)
        q_blocks += [pair, mine]
    qc_ref[...] = jnp.concatenate(q_blocks, axis=0).astype(BF16)
    vc_ref[...] = v_rows(pc[:, 512:768])


def _pad_last(w, before, after):
    return jnp.pad(w, ((0, 0),) * (w.ndim - 1) + ((before, after),))


def _pad_heads(w):
    return _pad_last(w, 0, HEAD_PAD - w.shape[-1]).reshape(w.shape[:-2] + (QKV_PAD,))


def _rot_half_cols(w):
    half = B_ROPE // 2
    return jnp.concatenate([-w[..., half:], w[..., :half]], axis=-1)


def _inproj_consts():
    tri = np.tril(np.ones((TM, TM), np.float32))
    sel = np.full((SUBLANES, LANES), 2, np.int32)
    e = np.zeros((LANES, 2 * LANES), np.float32)
    oneaug = np.zeros((1, 2 * LANES), np.float32)
    for copy in range(F_COPIES):
        for hd in range(N_HEADS):
            lane = copy * N_HEADS + hd
            sel[0, lane], sel[1, lane] = divmod(copy, 3)
            e[lane, hd * AUG_PER_HEAD + copy] = 1.0
            oneaug[0, LANES + hd * AUG_PER_HEAD + copy] = 1.0
            e[lane, LANES + hd * AUG_PER_HEAD + F_COPIES + copy] = -1.0
            oneaug[0, hd * AUG_PER_HEAD + F_COPIES + copy] = 1.0
    sel[2, :] = 0
    sel[2, :3 * N_HEADS] = 1
    onev = np.zeros((1, QKV_PAD), np.float32)
    onev[0, HEAD_DIM::HEAD_PAD] = 1.0
    ones_blk = np.zeros((HEAD_DIM, TM), np.float32)
    ones_blk[0] = 1.0
    return (jnp.asarray(sel), jnp.asarray(tri, BF16), jnp.asarray(e, BF16), jnp.asarray(oneaug),
            jnp.asarray(onev), jnp.asarray(ones_blk))


def _inproj_weights(w_in, w_uq, w_ukv, b_forget):
    wt = jnp.swapaxes(w_in, 1, 2).astype(BF16)
    o = 0
    cols = {}
    for name, width in (("aq", 256), ("ak", 256), ("av", 256), ("bq", B_Q_LORA), ("bkv", B_KV_LORA),
                        ("bkr", B_ROPE), ("cq", 256), ("ck", 256), ("cv", 256), ("cf", N_HEADS),
                        ("dv", CONV_CH), ("dg", CONV_CH)):
        cols[name] = wt[:, o:o + width, :]
        o += width
    scale = HEAD_DIM ** -0.5

    def pad_rows(w, before, after):
        return jnp.pad(w, ((0, 0), (before, after), (0, 0)))

    def rope_block(w):
        return pad_rows(w, B_NOPE, HEAD_PAD - B_NOPE - B_ROPE)

    half = B_ROPE // 2
    bkr_rot = jnp.concatenate([-cols["bkr"][:, half:], cols["bkr"][:, :half]], axis=1)
    cf = pad_rows(jnp.tile(cols["cf"], (1, F_COPIES, 1)), 0, LANES - F_COPIES * N_HEADS)
    w = jnp.concatenate([cols["aq"] * scale, cols["ak"], cols["av"],
                         cols["bq"], cols["bkv"], rope_block(cols["bkr"]), rope_block(bkr_rot), cf,
                         cols["cq"] * scale, cols["ck"], cols["cv"],
                         cols["dv"], cols["dg"]], axis=1)

    uq = w_uq.reshape(DEPTH, B_Q_LORA, N_HEADS, B_NOPE + B_ROPE)
    uq_rot = jnp.concatenate([jnp.zeros_like(uq[..., :B_NOPE]), _rot_half_cols(uq[..., B_NOPE:])], axis=-1)
    wuq = jnp.concatenate([_pad_heads(uq), _pad_heads(uq_rot)], axis=-1).astype(BF16)
    ukv = w_ukv.reshape(DEPTH, B_KV_LORA, N_HEADS, B_NOPE + B_V)
    wukv = jnp.concatenate([_pad_heads(ukv[..., :B_NOPE]), _pad_heads(ukv[..., B_NOPE:])], axis=-1).astype(BF16)
    bf = _row(_pad_last(jnp.tile(b_forget, (1, F_COPIES)), 0, LANES - F_COPIES * N_HEADS))
    return w, wuq, wukv, bf


def _inproj(l, x2, g_mix, w, wuq, wukv, bf, g_q_lat, g_kv_lat, cos_t, sin_t, conv_params, consts):
    assert w.shape == (DEPTH, COL_END, D_MODEL)
    row = lambda w: pl.BlockSpec((TM, w), lambda i: (i, 0))
    layer = lambda a: (a, _layer_spec(a, l))
    rope_spec = pl.BlockSpec((HEAD_PAD, TM), lambda i: (0, i))
    ins = [
        (x2, row(D_MODEL)), layer(g_mix), layer(w), (cos_t, rope_spec), (sin_t, rope_spec),
        layer(g_q_lat), layer(wuq), layer(g_kv_lat), layer(wukv), layer(bf),
    ] + [layer(p) for p in conv_params] + [(c, None) for c in consts]
    args = [a for a, _ in ins]
    specs = [s if s is not None else _const_spec(a.shape) for a, s in ins]

    def rows(w, dt):
        return jax.ShapeDtypeStruct((N_TOK, w), dt), row(w)

    def slab(nrows):
        return (jax.ShapeDtypeStruct((N_TOK // TM, nrows, TM), BF16),
                pl.BlockSpec((None, nrows, TM), lambda i: (i, 0, 0)))

    outs = [rows(D_MODEL, BF16), slab(QKV_PAD), rows(256, BF16), slab(QKV_PAD),
            slab(QKV_PAD), rows(QKV_PAD, BF16), slab(QKV_PAD),
            slab(2 * QKV_PAD), rows(3 * HEAD_PAD, BF16), slab(QKV_PAD), rows(CONV_CH, BF16)]
    return pl.pallas_call(
        _inproj_kernel,
        grid=(N_TOK // TM,),
        in_specs=specs,
        out_specs=[s for _, s in outs],
        out_shape=[a for a, _ in outs],
        scratch_shapes=[pltpu.VMEM((1, LANES), F32)] + [pltpu.VMEM((CONV_HALO + TM, CONV_CH), F32)] * 2,
        compiler_params=_params("arbitrary"),
        name="inproj",
    )(*args)


def _band_kernel(qt_ref, kp_ref, kc_ref, vtp_ref, vtc_ref, tbl_ref, o_ref):
    i = pl.program_id(1)
    w = TQ_A
    k_blocks = [(kp_ref, 0), (kp_ref, 1), (kc_ref, 0), (kc_ref, 1)]
    vt_blocks = [(vtp_ref, 0), (vtp_ref, 1), (vtc_ref, 0), (vtc_ref, 1)]

    def scores(t, hd):
        pair = slice((hd // 2) * HEAD_PAD, (hd // 2 + 1) * HEAD_PAD)
        qh = qt_ref[hd * HEAD_PAD:(hd + 1) * HEAD_PAD, t * w:(t + 1) * w]
        s = []
        for j in range(3):
            ref, half = k_blocks[t + j]
            sj = _dot(ref[half * w:(half + 1) * w, pair], qh) + tbl_ref[j, hd]
            if t + j < 2:
                sj = jnp.where(i > 0, sj, NEG)
            s.append(sj)
        return s

    def softmax(s):
        m = jnp.maximum(jnp.maximum(s[0].max(0, keepdims=True), s[1].max(0, keepdims=True)),
                        s[2].max(0, keepdims=True))
        return [jnp.exp2(sj - m).astype(BF16) for sj in s]

    def weighted_sum(t, hd, p):
        rows = slice(hd * HEAD_PAD, (hd + 1) * HEAD_PAD)
        acc = None
        for j in range(3):
            ref, half = vt_blocks[t + j]
            part = _dot(ref[rows, half * w:(half + 1) * w], p[j])
            acc = part if acc is None else acc + part
        return acc[:HEAD_DIM, :] / acc[HEAD_DIM:HEAD_DIM + 1, :]

    units = [(t, hd) for t in range(TM // w) for hd in range(N_HEADS)]
    s, p, outs = {}, {}, {}
    _emit_pipeline(len(units), (1, 2),
                   lambda u: s.__setitem__(u, scores(*units[u])),
                   lambda u: p.__setitem__(u, softmax(s.pop(u))),
                   lambda u: outs.__setitem__(units[u], weighted_sum(*units[u], p.pop(u))))
    for t in range(TM // w):
        o_ref[t * w:(t + 1) * w, :] = jnp.concatenate(
            [outs[(t, hd)] for hd in range(N_HEADS)], axis=0).T.astype(BF16)


def _band_table(rel_bias):
    w = TQ_A
    key = np.arange(w)[:, None]
    qry = np.arange(w)[None, :]
    diags, valid = [], []
    for j in range(3):
        rel = (2 - j) * w + (np.arange(2 * w) - (w - 1))
        diags.append(rel_bias[..., np.clip(rel, -A_MAX_REL, A_MAX_REL) + A_MAX_REL])
        cd = (2 - j) * (w // CHUNK) + qry // CHUNK - key // CHUNK
        valid.append((cd >= 0) & (cd <= A_LEFT_CHUNKS))
    diags = jnp.stack(diags, axis=1)
    valid = jnp.asarray(np.stack(valid), F32)
    return pl.pallas_call(
        _band_table_kernel,
        grid=(DEPTH, 3),
        in_specs=[pl.BlockSpec((None, None, N_HEADS, 2 * w), lambda l, j: (l, j, 0, 0)),
                  pl.BlockSpec((None, w, w), lambda l, j: (j, 0, 0))],
        out_specs=pl.BlockSpec((None, None, N_HEADS, w, w), lambda l, j: (l, j, 0, 0, 0)),
        out_shape=jax.ShapeDtypeStruct((DEPTH, 3, N_HEADS, w, w), F32),
        compiler_params=_params("parallel", "parallel"),
        name="band_table",
    )(diags, valid)


def _band_table_kernel(diag_ref, valid_ref, o_ref):
    for hd in range(N_HEADS):
        rows = jnp.broadcast_to(diag_ref[hd:hd + 1, :], (TQ_A, 2 * TQ_A))
        toeplitz = pltpu.roll(rows, TQ_A + 1, 1, stride=1, stride_axis=0)[:, :TQ_A]
        o_ref[hd] = jnp.where(valid_ref[...] > 0.0, toeplitz * LOG2E, NEG)


def _band_attention(l, qat, ka, vat, tbl):
    assert TM == 2 * TQ_A
    nt = TILES_PER_SEQ

    def cur(b, i):
        return b * nt + i

    def prev(b, i):
        return b * nt + jnp.maximum(i - 1, 0)

    def slab_spec(which):
        return pl.BlockSpec((None, QKV_PAD, TM), lambda b, i: (which(b, i), 0, 0))

    def k_spec(which):
        return pl.BlockSpec((TM, 256), lambda b, i: (which(b, i), 0))

    return pl.pallas_call(
        _band_kernel,
        grid=(BATCH, nt),
        in_specs=[slab_spec(cur), k_spec(prev), k_spec(cur), slab_spec(prev), slab_spec(cur), _layer_spec(tbl, l)],
        out_specs=pl.BlockSpec((TM, 256), lambda b, i: (b * nt + i, 0)),
        out_shape=jax.ShapeDtypeStruct((N_TOK, 256), BF16),
        compiler_params=_params("parallel", "parallel"),
        name="band_attention",
    )(qat, ka, ka, vat, vat, tbl)


def _flash_kernel(qt_ref, k_ref, vt_ref, o_ref, *, unit, pair_layout, lags):
    i = pl.program_id(1)
    t = T_ATT
    qrows = 2 * HEAD_PAD if pair_layout else HEAD_PAD

    def visible(sub, tk):
        key = lax.broadcasted_iota(jnp.int32, (tk, t), 0) + sub * tk
        last_visible = lax.broadcasted_iota(jnp.int32, (1, t), 1) | (unit - 1)
        return key <= last_visible

    def k_tile(k_ref, off, hd, rows):
        if not pair_layout:
            return k_ref[pl.ds(off, rows), hd * HEAD_PAD:(hd + 1) * HEAD_PAD]
        pair = hd // 2
        return jnp.concatenate([k_ref[pl.ds(off, rows), pair * HEAD_PAD:(pair + 1) * HEAD_PAD],
                                k_ref[pl.ds(off, rows), 2 * HEAD_PAD:3 * HEAD_PAD]], axis=1)

    def step(tiles, carry):
        heads = [slice(hd * HEAD_PAD, (hd + 1) * HEAD_PAD) for hd in range(N_HEADS)]
        m = [c[0] for c in carry]
        acc = [c[1] for c in carry]

        tk = t // FLASH_KEY_SPLIT

        def scores(j, sub, diag, hd):
            off = pl.multiple_of(j * t, t) + sub * tk
            s = _dot(k_tile(k_ref, off, hd, tk), qt_ref[hd * qrows:(hd + 1) * qrows, :])
            return jnp.where(visible(sub, tk), s, NEG) if diag else s

        def softmax(hd, s):
            m_new = jnp.maximum(m[hd], s.max(0, keepdims=True))
            alpha = jnp.exp2(m[hd] - m_new)
            m[hd] = m_new
            return alpha, jnp.exp2(s - m_new).astype(BF16)

        def update(j, sub, hd, alpha, p):
            acc[hd] = alpha * acc[hd] + _dot(vt_ref[j, heads[hd], sub * tk:(sub + 1) * tk], p)

        units = [(j, sub, diag, hd) for j, diag in tiles for sub in range(FLASH_KEY_SPLIT)
                 for hd in range(N_HEADS)]
        s, sm = {}, {}
        _emit_pipeline(len(units), lags,
                       lambda u: s.__setitem__(u, scores(*units[u])),
                       lambda u: sm.__setitem__(u, softmax(units[u][3], s.pop(u))),
                       lambda u: update(units[u][0], units[u][1], units[u][3], *sm.pop(u)))
        return tuple(zip(m, acc))

    init = tuple((jnp.full((1, t), NEG, F32), jnp.zeros((HEAD_PAD, t), F32)) for _ in range(N_HEADS))
    carry = lax.fori_loop(0, i // 2, lambda jj, c: step([(2 * jj, False), (2 * jj + 1, False)], c), init)
    def finish(tiles):
        outs = [acc[:HEAD_DIM, :] / acc[HEAD_DIM:HEAD_DIM + 1, :] for _, acc in step(tiles, carry)]
        o_ref[...] = jnp.concatenate(outs, axis=0).T.astype(BF16)

    @pl.when(i % 2 == 1)
    def _():
        finish([(i - 1, False), (i, True)])

    @pl.when(i % 2 == 0)
    def _():
        finish([(i, True)])


def _flash_attention(qt, k, vt, unit, name):
    nt = SEQ // T_ATT
    pair_layout = k.shape[1] == 3 * HEAD_PAD
    return pl.pallas_call(
        functools.partial(_flash_kernel, unit=unit, pair_layout=pair_layout,
                          lags=(2, 4) if pair_layout else (2, 6)),
        grid=(BATCH, nt),
        in_specs=[pl.BlockSpec((None, qt.shape[1], T_ATT), lambda b, i: (b * nt + i, 0, 0)),
                  pl.BlockSpec((SEQ, k.shape[1]), lambda b, i: (b, 0)),
                  pl.BlockSpec((nt, QKV_PAD, T_ATT), lambda b, i: (b, 0, 0))],
        out_specs=pl.BlockSpec((T_ATT, 256), lambda b, i: (b * nt + i, 0)),
        out_shape=jax.ShapeDtypeStruct((N_TOK, 256), BF16),
        compiler_params=_params("parallel", "parallel"),
        name=name,
    )(qt, k, vt)


def _conv_ln_swish(ext_ref, shift_ref, w_ref, b_ref, g_ref, beta_ref):
    acc = jnp.zeros((TM, CONV_CH), F32) + b_ref[...]
    first = CONV_HALO - (CONV_K - 1)
    for res in range(SUBLANES):
        taps = [kk for kk in range(CONV_K) if (first + kk) % SUBLANES == res]
        span = max((first + kk) // SUBLANES for kk in taps) * SUBLANES + TM
        shift_ref[0:span, :] = ext_ref[res:res + span, :]
        for kk in taps:
            start = (first + kk) // SUBLANES * SUBLANES
            acc = acc + shift_ref[start:start + TM, :] * w_ref[kk:kk + 1, :]
    mu = jnp.mean(acc, axis=-1, keepdims=True)
    xc = acc - mu
    var = jnp.mean(xc * xc, axis=-1, keepdims=True)
    y = xc * lax.rsqrt(var + NORM_EPS) * g_ref[...] + beta_ref[...]
    return (y * _sigmoid(y)).astype(BF16)


def _merge_kernel(x_ref, h_ref, oa_ref, ob_ref, oc_ref, od_ref, wg_ref, bg_ref, wbr_ref, wo_ref, out_ref):
    halves = [slice(r * TM, (r + 1) * TM) for r in range(MERGE_ROWS // TM)]
    acc = [jnp.zeros((TM, D_MODEL), F32) for _ in halves]
    for n, o_ref in enumerate((oa_ref, ob_ref, oc_ref, od_ref)):
        sl = slice(n * D_MODEL, (n + 1) * D_MODEL)
        for r, rows in enumerate(halves):
            gate = _sigmoid(_dot(h_ref[rows, :], wg_ref[:, sl]) + bg_ref[:, sl])
            acc[r] = acc[r] + gate * _dot(o_ref[rows, :], wbr_ref[n])
    for r, rows in enumerate(halves):
        out_ref[rows, :] = x_ref[rows, :] + _dot(acc[r].astype(BF16), wo_ref[...])


def _merge(l, x2, h, oa, ob, oc, od, wg, bg, wbr, wo):
    row = lambda w: pl.BlockSpec((MERGE_ROWS, w), lambda i: (i, 0))
    return pl.pallas_call(
        _merge_kernel,
        grid=(N_TOK // MERGE_ROWS,),
        in_specs=[row(D_MODEL), row(D_MODEL)] + [row(BRANCH_W)] * 4
                 + [_layer_spec(p, l) for p in (wg, bg, wbr, wo)],
        out_specs=row(D_MODEL),
        out_shape=jax.ShapeDtypeStruct((N_TOK, D_MODEL), F32),
        compiler_params=_params("parallel"),
        name="merge",
    )(x2, h, oa, ob, oc, od, wg, bg, wbr, wo)


def _ffn_kernel(x_ref, g_ref, wup_ref, wdn_ref, gfin_ref, out_ref, *, final):
    halves = [slice(r * TM, (r + 1) * TM) for r in range(MERGE_ROWS // TM)]
    acc = [x_ref[rows, :] for rows in halves]
    hf = [_rms(a, g_ref[...]).astype(BF16) for a in acc]
    for c in range(D_FF // D_MODEL):
        sl = slice(c * D_MODEL, (c + 1) * D_MODEL)
        for r in range(len(halves)):
            up = jnp.maximum(_dot(hf[r], wup_ref[:, sl]), 0.0)
            acc[r] = acc[r] + _dot((up * up).astype(BF16), wdn_ref[sl, :])
    for r, rows in enumerate(halves):
        out_ref[rows, :] = _rms(acc[r], gfin_ref[...]) if final else acc[r]


def _ffn(l, x2, g_ffn, wup, wdn, g_final, final):
    row = pl.BlockSpec((MERGE_ROWS, D_MODEL), lambda i: (i, 0))
    return pl.pallas_call(
        functools.partial(_ffn_kernel, final=final),
        grid=(N_TOK // MERGE_ROWS,),
        in_specs=[row] + [_layer_spec(p, l) for p in (g_ffn, wup, wdn)] + [_const_spec((1, D_MODEL))],
        out_specs=row,
        out_shape=jax.ShapeDtypeStruct((N_TOK, D_MODEL), F32),
        compiler_params=_params("parallel"),
        name="ffn",
    )(x2, g_ffn, wup, wdn, g_final)


def kernel(x, positions, g_mix, w_in, w_gate, b_gate, rel_bias, g_q_lat, w_uq, g_kv_lat, w_ukv, b_forget,
           w_dw, b_dw, g_conv_ln, b_conv_ln, w_branch, w_o, g_ffn, w_up, w_down, g_final):
    x2 = x.reshape(N_TOK, D_MODEL)
    cos_t, sin_t = _rope_tables(positions)
    consts = _inproj_consts()
    w, wuq, wukv, bf = _inproj_weights(w_in, w_uq, w_ukv, b_forget)
    tbl = _band_table(rel_bias)
    w_taps = jnp.pad(w_dw, ((0, 0), (0, 1), (0, 0)))
    wg, wbr, wo, wup, wdn = (p.astype(BF16) for p in (w_gate, w_branch, w_o, w_up, w_down))
    conv_params = (w_taps, _row(b_dw), _row(g_conv_ln), _row(b_conv_ln))
    for l in range(DEPTH):
        (h, qa, ka, va, qb, kb, vb, qc, kc, vc, od) = _inproj(
            l, x2, _row(g_mix), w, wuq, wukv, bf, _row(g_q_lat), _row(g_kv_lat), cos_t, sin_t, conv_params, consts)
        oa = _band_attention(l, qa, ka, va, tbl)
        ob = _flash_attention(qb, kb, vb, CHUNK, "flash_b")
        oc = _flash_attention(qc, kc, vc, 1, "flash_c")
        x2 = _merge(l, x2, h, oa, ob, oc, od, wg, _row(b_gate), wbr, wo)
        x2 = _ffn(l, x2, _row(g_ffn), wup, wdn, g_final.reshape(1, -1), final=(l == DEPTH - 1))
    return x2.reshape(BATCH, SEQ, D_MODEL)
```

```python
import functools

import numpy as np
import jax
import jax.numpy as jnp
from jax import lax
from jax.experimental import pallas as pl
from jax.experimental.pallas import tpu as pltpu

D_MODEL = 1024
BATCH = 4
SEQ = 4096
DEPTH = 2
N_TOK = BATCH * SEQ
CHUNK = 64
HEAD_DIM = 64
N_HEADS = 4
NORM_EPS = 1e-6
A_LEFT_CHUNKS = 8
A_MAX_REL = 128
B_Q_LORA = 256
B_KV_LORA = 128
B_NOPE = 64
B_ROPE = 32
B_V = 64
ROPE_THETA = 10000.0
CONV_CH = 256
CONV_K = 31
BRANCH_W = 256
D_FF = 4 * D_MODEL

LANES = 128
SUBLANES = 8
HEAD_PAD = LANES
QKV_PAD = N_HEADS * HEAD_PAD
NEG = -1e30
LOG2E = 1.4426950408889634

TM = 512
TILES_PER_SEQ = SEQ // TM
T_ATT = 512
TQ_A = 256
CONV_HALO = 32
FLASH_KEY_SPLIT = 2
MERGE_ROWS = 2 * TM
VMEM_LIMIT = 56 * 1024 * 1024

F32 = jnp.float32
BF16 = jnp.bfloat16


def _dot(a, b):
    return jnp.dot(a, b, preferred_element_type=F32)


def _dot_nt(a, b):
    return lax.dot_general(a, b, (((1,), (1,)), ((), ())), preferred_element_type=F32)


def _rms(x, g):
    return x * lax.rsqrt(jnp.mean(x * x, axis=-1, keepdims=True) + NORM_EPS) * g


def _sigmoid(x):
    return 1.0 / (1.0 + jnp.exp(-x))


def _split3(x):
    hi = x.astype(BF16).astype(F32)
    r1 = x - hi
    mid = r1.astype(BF16).astype(F32)
    return hi, mid, r1 - mid


def _emit_pipeline(n, lags, stage_a, stage_b, stage_c):
    for tick in range(n + lags[1]):
        if tick < n:
            stage_a(tick)
        if 0 <= tick - lags[0] < n:
            stage_b(tick - lags[0])
        if 0 <= tick - lags[1] < n:
            stage_c(tick - lags[1])


def _const_spec(shape):
    return pl.BlockSpec(shape, lambda *_: (0,) * len(shape))


def _layer_spec(arr, l):
    nd = arr.ndim - 1
    return pl.BlockSpec((None,) + tuple(arr.shape[1:]), lambda *_: (l,) + (0,) * nd,
                        pipeline_mode=pl.Buffered(1))


def _row(p):
    return p.reshape(p.shape[0], 1, p.shape[1])


def _params(*sem):
    return pltpu.CompilerParams(dimension_semantics=sem, vmem_limit_bytes=VMEM_LIMIT)


def _rope_kernel(pos_ref, invf_ref, cos_ref, sin_ref):
    half = B_ROPE // 2
    ang = invf_ref[...] * pos_ref[...].astype(F32)
    for ref, table, rest in ((cos_ref, jnp.cos(ang), 1.0), (sin_ref, jnp.sin(ang), 0.0)):
        ref[0:B_NOPE, :] = jnp.full((B_NOPE, ang.shape[1]), rest, F32)
        ref[B_NOPE:B_NOPE + half, :] = table
        ref[B_NOPE + half:B_NOPE + B_ROPE, :] = table
        ref[B_NOPE + B_ROPE:HEAD_PAD, :] = jnp.full((HEAD_PAD - B_NOPE - B_ROPE, ang.shape[1]), rest, F32)


def _rope_tables(positions):
    half = B_ROPE // 2
    inv_freq = 1.0 / (ROPE_THETA ** (jnp.arange(half, dtype=F32) / half))
    tm = 2048
    return pl.pallas_call(
        _rope_kernel,
        grid=(N_TOK // tm,),
        in_specs=[pl.BlockSpec((1, tm), lambda i: (0, i)), _const_spec((half, 1))],
        out_specs=[pl.BlockSpec((HEAD_PAD, tm), lambda i: (0, i))] * 2,
        out_shape=[jax.ShapeDtypeStruct((HEAD_PAD, N_TOK), F32)] * 2,
        compiler_params=_params("parallel"),
        name="rope_tables",
    )(positions.reshape(1, N_TOK), inv_freq.reshape(half, 1))


COL_A, COL_B, COL_C, COL_D, COL_END = 0, 768, 1536, 2304, 2816
F_COPIES = 9
AUG_PER_HEAD = 2 * F_COPIES


def _pick3(sel, parts):
    return jnp.where(sel == 0, parts[0], jnp.where(sel == 1, parts[1], parts[2]))


def _inproj_kernel(x_ref, gmix_ref, w_ref, cos_ref, sin_ref, gq_ref, wuq_ref, gkv_ref, wukv_ref, bf_ref,
                   wdw_ref, bdw_ref, gln_ref, bln_ref, sel_ref, tri_ref, e_ref, oneaug_ref, onev_ref, onesblk_ref,
                   h_ref, qa_ref, ka_ref, va_ref, qb_ref, kb_ref, vb_ref,
                   qc_ref, kc_ref, vc_ref, od_ref, carry_ref, ext_ref, shift_ref):
    i = pl.program_id(0)

    @pl.when(i % TILES_PER_SEQ == 0)
    def _():
        carry_ref[...] = jnp.zeros_like(carry_ref)
        ext_ref[0:CONV_HALO, :] = jnp.zeros((CONV_HALO, CONV_CH), F32)

    h = _rms(x_ref[...], gmix_ref[...]).astype(BF16)
    h_ref[...] = h

    pd = _dot_nt(h, w_ref[COL_D:COL_END, :])
    ext_ref[CONV_HALO:CONV_HALO + TM, :] = pd[:, 0:CONV_CH] * _sigmoid(pd[:, CONV_CH:2 * CONV_CH])
    od_ref[...] = _conv_ln_swish(ext_ref, shift_ref, wdw_ref, bdw_ref, gln_ref, bln_ref)
    ext_ref[0:CONV_HALO, :] = ext_ref[TM:TM + CONV_HALO, :]

    row = lax.broadcasted_iota(jnp.int32, (HEAD_PAD, TM), 0)
    own_rows = (row < HEAD_DIM, row >= HEAD_DIM)
    ones_blk = onesblk_ref[...]

    def pair_rows(q):
        qt = q.T
        return [jnp.where(own_rows[hd % 2], qt[(hd // 2) * HEAD_PAD:(hd // 2 + 1) * HEAD_PAD], 0.0)
                for hd in range(N_HEADS)]

    def v_rows(v):
        vt = v.T
        blocks = []
        for hd in range(N_HEADS):
            blocks += [vt[hd * HEAD_DIM:(hd + 1) * HEAD_DIM], ones_blk]
        return jnp.concatenate(blocks, axis=0).astype(BF16)

    pa = _dot_nt(h, w_ref[COL_A:COL_B, :])
    ka_ref[...] = pa[:, 256:512].astype(BF16)
    qa_ref[...] = jnp.concatenate(pair_rows(pa[:, 0:256] * LOG2E), axis=0).astype(BF16)
    va_ref[...] = v_rows(pa[:, 512:768])

    pb = _dot_nt(h, w_ref[COL_B:COL_C, :])
    cos = cos_ref[...].T
    sin = sin_ref[...].T
    cos4 = jnp.concatenate([cos] * N_HEADS, axis=-1)
    sin4 = jnp.concatenate([sin] * N_HEADS, axis=-1)
    qn = _rms(pb[:, 0:B_Q_LORA], gq_ref[...]).astype(BF16)
    qq = _dot(qn, wuq_ref[...])
    qb = qq[:, :QKV_PAD] * cos4 + qq[:, QKV_PAD:] * sin4
    qb_ref[...] = (qb * ((B_NOPE + B_ROPE) ** -0.5 * LOG2E)).T.astype(BF16)
    kvn = _rms(pb[:, B_Q_LORA:B_Q_LORA + B_KV_LORA], gkv_ref[...]).astype(BF16)
    krot = pb[:, 384:512] * cos + pb[:, 512:640] * sin
    kv = _dot(kvn, wukv_ref[...])
    kb_ref[...] = (kv[:, :QKV_PAD] + jnp.concatenate([krot] * N_HEADS, axis=-1)).astype(BF16)
    vb_ref[...] = (kv[:, QKV_PAD:] + onev_ref[...]).T.astype(BF16)

    pc = _dot_nt(h, w_ref[COL_C:COL_D, :])
    piece_a = sel_ref[0:1, :]
    piece_b = sel_ref[1:2, :]
    z = pb[:, 640:768] + bf_ref[...]
    logf = jnp.minimum(z, 0.0) - jnp.log(1.0 + jnp.exp(-jnp.abs(z)))
    part = _dot(tri_ref[...], _pick3(piece_a, _split3(logf)).astype(BF16))
    g = part * LOG2E + carry_ref[...]
    last = g[TM - SUBLANES:TM, :]
    total = last + pltpu.roll(last, LANES - 3 * N_HEADS, 1) + pltpu.roll(last, LANES - 6 * N_HEADS, 1)
    carry_ref[...] = jnp.where(sel_ref[2:3, :] == 1, total[SUBLANES - 1:], 0.0)
    aug = _dot(_pick3(piece_b, _split3(g)).astype(BF16), e_ref[...]) + oneaug_ref[...]
    kc_ref[...] = jnp.concatenate([pc[:, 256:512], aug[:, LANES:]], axis=-1).astype(BF16)
    aug_t = aug[:, :LANES].T
    q_blocks = []
    for hd, pair in enumerate(pair_rows(pc[:, 0:256] * LOG2E)):
        mine = jnp.where(row >= hd * AUG_PER_HEAD, jnp.where(row < (hd + 1) * AUG_PER_HEAD, aug_t, 0.0), 0.0)
        q_blocks += [pair, mine]
    qc_ref[...] = jnp.concatenate(q_blocks, axis=0).astype(BF16)
    vc_ref[...] = v_rows(pc[:, 512:768])


def _pad_last(w, before, after):
    return jnp.pad(w, ((0, 0),) * (w.ndim - 1) + ((before, after),))


def _pad_heads(w):
    return _pad_last(w, 0, HEAD_PAD - w.shape[-1]).reshape(w.shape[:-2] + (QKV_PAD,))


def _rot_half_cols(w):
    half = B_ROPE // 2
    return jnp.concatenate([-w[..., half:], w[..., :half]], axis=-1)


def _inproj_consts():
    tri = np.tril(np.ones((TM, TM), np.float32))
    sel = np.full((SUBLANES, LANES), 2, np.int32)
    e = np.zeros((LANES, 2 * LANES), np.float32)
    oneaug = np.zeros((1, 2 * LANES), np.float32)
    for copy in range(F_COPIES):
        for hd in range(N_HEADS):
            lane = copy * N_HEADS + hd
            sel[0, lane], sel[1, lane] = divmod(copy, 3)
            e[lane, hd * AUG_PER_HEAD + copy] = 1.0
            oneaug[0, LANES + hd * AUG_PER_HEAD + copy] = 1.0
            e[lane, LANES + hd * AUG_PER_HEAD + F_COPIES + copy] = -1.0
            oneaug[0, hd * AUG_PER_HEAD + F_COPIES + copy] = 1.0
    sel[2, :] = 0
    sel[2, :3 * N_HEADS] = 1
    onev = np.zeros((1, QKV_PAD), np.float32)
    onev[0, HEAD_DIM::HEAD_PAD] = 1.0
    ones_blk = np.zeros((HEAD_DIM, TM), np.float32)
    ones_blk[0] = 1.0
    return (jnp.asarray(sel), jnp.asarray(tri, BF16), jnp.asarray(e, BF16), jnp.asarray(oneaug),
            jnp.asarray(onev), jnp.asarray(ones_blk))


def _inproj_weights(w_in, w_uq, w_ukv, b_forget):
    wt = jnp.swapaxes(w_in, 1, 2).astype(BF16)
    o = 0
    cols = {}
    for name, width in (("aq", 256), ("ak", 256), ("av", 256), ("bq", B_Q_LORA), ("bkv", B_KV_LORA),
                        ("bkr", B_ROPE), ("cq", 256), ("ck", 256), ("cv", 256), ("cf", N_HEADS),
                        ("dv", CONV_CH), ("dg", CONV_CH)):
        cols[name] = wt[:, o:o + width, :]
        o += width
    scale = HEAD_DIM ** -0.5

    def pad_rows(w, before, after):
        return jnp.pad(w, ((0, 0), (before, after), (0, 0)))

    def rope_block(w):
        return pad_rows(w, B_NOPE, HEAD_PAD - B_NOPE - B_ROPE)

    half = B_ROPE // 2
    bkr_rot = jnp.concatenate([-cols["bkr"][:, half:], cols["bkr"][:, :half]], axis=1)
    cf = pad_rows(jnp.tile(cols["cf"], (1, F_COPIES, 1)), 0, LANES - F_COPIES * N_HEADS)
    w = jnp.concatenate([cols["aq"] * scale, cols["ak"], cols["av"],
                         cols["bq"], cols["bkv"], rope_block(cols["bkr"]), rope_block(bkr_rot), cf,
                         cols["cq"] * scale, cols["ck"], cols["cv"],
                         cols["dv"], cols["dg"]], axis=1)

    uq = w_uq.reshape(DEPTH, B_Q_LORA, N_HEADS, B_NOPE + B_ROPE)
    uq_rot = jnp.concatenate([jnp.zeros_like(uq[..., :B_NOPE]), _rot_half_cols(uq[..., B_NOPE:])], axis=-1)
    wuq = jnp.concatenate([_pad_heads(uq), _pad_heads(uq_rot)], axis=-1).astype(BF16)
    ukv = w_ukv.reshape(DEPTH, B_KV_LORA, N_HEADS, B_NOPE + B_V)
    wukv = jnp.concatenate([_pad_heads(ukv[..., :B_NOPE]), _pad_heads(ukv[..., B_NOPE:])], axis=-1).astype(BF16)
    bf = _row(_pad_last(jnp.tile(b_forget, (1, F_COPIES)), 0, LANES - F_COPIES * N_HEADS))
    return w, wuq, wukv, bf


def _inproj(l, x2, g_mix, w, wuq, wukv, bf, g_q_lat, g_kv_lat, cos_t, sin_t, conv_params, consts):
    assert w.shape == (DEPTH, COL_END, D_MODEL)
    row = lambda w: pl.BlockSpec((TM, w), lambda i: (i, 0))
    layer = lambda a: (a, _layer_spec(a, l))
    rope_spec = pl.BlockSpec((HEAD_PAD, TM), lambda i: (0, i))
    ins = [
        (x2, row(D_MODEL)), layer(g_mix), layer(w), (cos_t, rope_spec), (sin_t, rope_spec),
        layer(g_q_lat), layer(wuq), layer(g_kv_lat), layer(wukv), layer(bf),
    ] + [layer(p) for p in conv_params] + [(c, None) for c in consts]
    args = [a for a, _ in ins]
    specs = [s if s is not None else _const_spec(a.shape) for a, s in ins]

    def rows(w, dt):
        return jax.ShapeDtypeStruct((N_TOK, w), dt), row(w)

    def slab(nrows):
        return (jax.ShapeDtypeStruct((N_TOK // TM, nrows, TM), BF16),
                pl.BlockSpec((None, nrows, TM), lambda i: (i, 0, 0)))

    outs = [rows(D_MODEL, BF16), slab(QKV_PAD), rows(256, BF16), slab(QKV_PAD),
            slab(QKV_PAD), rows(QKV_PAD, BF16), slab(QKV_PAD),
            slab(2 * QKV_PAD), rows(3 * HEAD_PAD, BF16), slab(QKV_PAD), rows(CONV_CH, BF16)]
    return pl.pallas_call(
        _inproj_kernel,
        grid=(N_TOK // TM,),
        in_specs=specs,
        out_specs=[s for _, s in outs],
        out_shape=[a for a, _ in outs],
        scratch_shapes=[pltpu.VMEM((1, LANES), F32)] + [pltpu.VMEM((CONV_HALO + TM, CONV_CH), F32)] * 2,
        compiler_params=_params("arbitrary"),
        name="inproj",
    )(*args)


def _band_kernel(qt_ref, kp_ref, kc_ref, vtp_ref, vtc_ref, tbl_ref, o_ref):
    i = pl.program_id(1)
    w = TQ_A
    k_blocks = [(kp_ref, 0), (kp_ref, 1), (kc_ref, 0), (kc_ref, 1)]
    vt_blocks = [(vtp_ref, 0), (vtp_ref, 1), (vtc_ref, 0), (vtc_ref, 1)]

    m, acc = {}, {}

    def scores(t, j, hd):
        pair = slice((hd // 2) * HEAD_PAD, (hd // 2 + 1) * HEAD_PAD)
        qh = qt_ref[hd * HEAD_PAD:(hd + 1) * HEAD_PAD, t * w:(t + 1) * w]
        ref, half = k_blocks[t + j]
        s = _dot(ref[half * w:(half + 1) * w, pair], qh) + tbl_ref[j, hd]
        if t + j < 2:
            s = jnp.where(i > 0, s, NEG)
        return s

    def softmax(t, j, hd, s):
        smax = s.max(0, keepdims=True)
        if j == 0:
            m[t, hd], alpha = smax, None
        else:
            m_new = jnp.maximum(m[t, hd], smax)
            alpha = jnp.exp2(m[t, hd] - m_new)
            m[t, hd] = m_new
        return alpha, jnp.exp2(s - m[t, hd]).astype(BF16)

    def update(t, j, hd, alpha, p):
        ref, half = vt_blocks[t + j]
        part = _dot(ref[hd * HEAD_PAD:(hd + 1) * HEAD_PAD, half * w:(half + 1) * w], p)
        acc[t, hd] = part if alpha is None else alpha * acc[t, hd] + part

    units = [(t, j, hd) for t in range(TM // w) for j in range(3) for hd in range(N_HEADS)]
    s, sm = {}, {}
    _emit_pipeline(len(units), (2, 5),
                   lambda u: s.__setitem__(u, scores(*units[u])),
                   lambda u: sm.__setitem__(u, softmax(*units[u], s.pop(u))),
                   lambda u: update(*units[u], *sm.pop(u)))
    for t in range(TM // w):
        outs = [acc[t, hd][:HEAD_DIM, :] / acc[t, hd][HEAD_DIM:HEAD_DIM + 1, :] for hd in range(N_HEADS)]
        o_ref[t * w:(t + 1) * w, :] = jnp.concatenate(outs, axis=0).T.astype(BF16)


def _band_table(rel_bias):
    w = TQ_A
    key = np.arange(w)[:, None]
    qry = np.arange(w)[None, :]
    diags, valid = [], []
    for j in range(3):
        rel = (2 - j) * w + (np.arange(2 * w) - (w - 1))
        diags.append(rel_bias[..., np.clip(rel, -A_MAX_REL, A_MAX_REL) + A_MAX_REL])
        cd = (2 - j) * (w // CHUNK) + qry // CHUNK - key // CHUNK
        valid.append((cd >= 0) & (cd <= A_LEFT_CHUNKS))
    diags = jnp.stack(diags, axis=1)
    valid = jnp.asarray(np.stack(valid), F32)
    return pl.pallas_call(
        _band_table_kernel,
        grid=(DEPTH, 3),
        in_specs=[pl.BlockSpec((None, None, N_HEADS, 2 * w), lambda l, j: (l, j, 0, 0)),
                  pl.BlockSpec((None, w, w), lambda l, j: (j, 0, 0))],
        out_specs=pl.BlockSpec((None, None, N_HEADS, w, w), lambda l, j: (l, j, 0, 0, 0)),
        out_shape=jax.ShapeDtypeStruct((DEPTH, 3, N_HEADS, w, w), F32),
        compiler_params=_params("parallel", "parallel"),
        name="band_table",
    )(diags, valid)


def _band_table_kernel(diag_ref, valid_ref, o_ref):
    for hd in range(N_HEADS):
        rows = jnp.broadcast_to(diag_ref[hd:hd + 1, :], (TQ_A, 2 * TQ_A))
        toeplitz = pltpu.roll(rows, TQ_A + 1, 1, stride=1, stride_axis=0)[:, :TQ_A]
        o_ref[hd] = jnp.where(valid_ref[...] > 0.0, toeplitz * LOG2E, NEG)


def _band_attention(l, qat, ka, vat, tbl):
    assert TM == 2 * TQ_A
    nt = TILES_PER_SEQ

    def cur(b, i):
        return b * nt + i

    def prev(b, i):
        return b * nt + jnp.maximum(i - 1, 0)

    def slab_spec(which):
        return pl.BlockSpec((None, QKV_PAD, TM), lambda b, i: (which(b, i), 0, 0))

    def k_spec(which):
        return pl.BlockSpec((TM, 256), lambda b, i: (which(b, i), 0))

    return pl.pallas_call(
        _band_kernel,
        grid=(BATCH, nt),
        in_specs=[slab_spec(cur), k_spec(prev), k_spec(cur), slab_spec(prev), slab_spec(cur), _layer_spec(tbl, l)],
        out_specs=pl.BlockSpec((TM, 256), lambda b, i: (b * nt + i, 0)),
        out_shape=jax.ShapeDtypeStruct((N_TOK, 256), BF16),
        compiler_params=_params("parallel", "parallel"),
        name="band_attention",
    )(qat, ka, ka, vat, vat, tbl)


def _flash_kernel(qt_ref, k_ref, vt_ref, o_ref, *, unit, pair_layout, lags):
    i = pl.program_id(1)
    t = T_ATT
    qrows = 2 * HEAD_PAD if pair_layout else HEAD_PAD

    def visible(sub, tk):
        key = lax.broadcasted_iota(jnp.int32, (tk, t), 0) + sub * tk
        last_visible = lax.broadcasted_iota(jnp.int32, (1, t), 1) | (unit - 1)
        return key <= last_visible

    def k_tile(k_ref, off, hd, rows):
        if not pair_layout:
            return k_ref[pl.ds(off, rows), hd * HEAD_PAD:(hd + 1) * HEAD_PAD]
        pair = hd // 2
        return jnp.concatenate([k_ref[pl.ds(off, rows), pair * HEAD_PAD:(pair + 1) * HEAD_PAD],
                                k_ref[pl.ds(off, rows), 2 * HEAD_PAD:3 * HEAD_PAD]], axis=1)

    def step(tiles, carry):
        heads = [slice(hd * HEAD_PAD, (hd + 1) * HEAD_PAD) for hd in range(N_HEADS)]
        m = [c[0] for c in carry]
        acc = [c[1] for c in carry]

        tk = t // FLASH_KEY_SPLIT

        def scores(j, sub, diag, hd):
            off = pl.multiple_of(j * t, t) + sub * tk
            s = _dot(k_tile(k_ref, off, hd, tk), qt_ref[hd * qrows:(hd + 1) * qrows, :])
            return jnp.where(visible(sub, tk), s, NEG) if diag else s

        def softmax(hd, s):
            m_new = jnp.maximum(m[hd], s.max(0, keepdims=True))
            alpha = jnp.exp2(m[hd] - m_new)
            m[hd] = m_new
            return alpha, jnp.exp2(s - m_new).astype(BF16)

        def update(j, sub, hd, alpha, p):
            acc[hd] = alpha * acc[hd] + _dot(vt_ref[j, heads[hd], sub * tk:(sub + 1) * tk], p)

        units = [(j, sub, diag, hd) for j, diag in tiles for sub in range(FLASH_KEY_SPLIT)
                 for hd in range(N_HEADS)]
        s, sm = {}, {}
        _emit_pipeline(len(units), lags,
                       lambda u: s.__setitem__(u, scores(*units[u])),
                       lambda u: sm.__setitem__(u, softmax(units[u][3], s.pop(u))),
                       lambda u: update(units[u][0], units[u][1], units[u][3], *sm.pop(u)))
        return tuple(zip(m, acc))

    init = tuple((jnp.full((1, t), NEG, F32), jnp.zeros((HEAD_PAD, t), F32)) for _ in range(N_HEADS))
    carry = lax.fori_loop(0, i // 2, lambda jj, c: step([(2 * jj, False), (2 * jj + 1, False)], c), init)
    def finish(tiles):
        outs = [acc[:HEAD_DIM, :] / acc[HEAD_DIM:HEAD_DIM + 1, :] for _, acc in step(tiles, carry)]
        o_ref[...] = jnp.concatenate(outs, axis=0).T.astype(BF16)

    @pl.when(i % 2 == 1)
    def _():
        finish([(i - 1, False), (i, True)])

    @pl.when(i % 2 == 0)
    def _():
        finish([(i, True)])


def _flash_attention(qt, k, vt, unit, name):
    nt = SEQ // T_ATT
    pair_layout = k.shape[1] == 3 * HEAD_PAD
    return pl.pallas_call(
        functools.partial(_flash_kernel, unit=unit, pair_layout=pair_layout,
                          lags=(2, 4) if pair_layout else (2, 6)),
        grid=(BATCH, nt),
        in_specs=[pl.BlockSpec((None, qt.shape[1], T_ATT), lambda b, i: (b * nt + i, 0, 0)),
                  pl.BlockSpec((SEQ, k.shape[1]), lambda b, i: (b, 0)),
                  pl.BlockSpec((nt, QKV_PAD, T_ATT), lambda b, i: (b, 0, 0))],
        out_specs=pl.BlockSpec((T_ATT, 256), lambda b, i: (b * nt + i, 0)),
        out_shape=jax.ShapeDtypeStruct((N_TOK, 256), BF16),
        compiler_params=_params("parallel", "parallel"),
        name=name,
    )(qt, k, vt)


def _conv_ln_swish(ext_ref, shift_ref, w_ref, b_ref, g_ref, beta_ref):
    acc = jnp.zeros((TM, CONV_CH), F32) + b_ref[...]
    first = CONV_HALO - (CONV_K - 1)
    for res in range(SUBLANES):
        taps = [kk for kk in range(CONV_K) if (first + kk) % SUBLANES == res]
        span = max((first + kk) // SUBLANES for kk in taps) * SUBLANES + TM
        shift_ref[0:span, :] = ext_ref[res:res + span, :]
        for kk in taps:
            start = (first + kk) // SUBLANES * SUBLANES
            acc = acc + shift_ref[start:start + TM, :] * w_ref[kk:kk + 1, :]
    mu = jnp.mean(acc, axis=-1, keepdims=True)
    xc = acc - mu
    var = jnp.mean(xc * xc, axis=-1, keepdims=True)
    y = xc * lax.rsqrt(var + NORM_EPS) * g_ref[...] + beta_ref[...]
    return (y * _sigmoid(y)).astype(BF16)


def _merge_kernel(x_ref, h_ref, oa_ref, ob_ref, oc_ref, od_ref, wg_ref, bg_ref, wbr_ref, wo_ref, out_ref):
    halves = [slice(r * TM, (r + 1) * TM) for r in range(MERGE_ROWS // TM)]
    acc = [jnp.zeros((TM, D_MODEL), F32) for _ in halves]
    for n, o_ref in enumerate((oa_ref, ob_ref, oc_ref, od_ref)):
        sl = slice(n * D_MODEL, (n + 1) * D_MODEL)
        for r, rows in enumerate(halves):
            gate = _sigmoid(_dot(h_ref[rows, :], wg_ref[:, sl]) + bg_ref[:, sl])
            acc[r] = acc[r] + gate * _dot(o_ref[rows, :], wbr_ref[n])
    for r, rows in enumerate(halves):
        out_ref[rows, :] = x_ref[rows, :] + _dot(acc[r].astype(BF16), wo_ref[...])


def _merge(l, x2, h, oa, ob, oc, od, wg, bg, wbr, wo):
    row = lambda w: pl.BlockSpec((MERGE_ROWS, w), lambda i: (i, 0))
    return pl.pallas_call(
        _merge_kernel,
        grid=(N_TOK // MERGE_ROWS,),
        in_specs=[row(D_MODEL), row(D_MODEL)] + [row(BRANCH_W)] * 4
                 + [_layer_spec(p, l) for p in (wg, bg, wbr, wo)],
        out_specs=row(D_MODEL),
        out_shape=jax.ShapeDtypeStruct((N_TOK, D_MODEL), F32),
        compiler_params=_params("parallel"),
        name="merge",
    )(x2, h, oa, ob, oc, od, wg, bg, wbr, wo)


def _ffn_kernel(x_ref, g_ref, wup_ref, wdn_ref, gfin_ref, out_ref, *, final):
    halves = [slice(r * TM, (r + 1) * TM) for r in range(MERGE_ROWS // TM)]
    acc = [x_ref[rows, :] for rows in halves]
    hf = [_rms(a, g_ref[...]).astype(BF16) for a in acc]
    for c in range(D_FF // D_MODEL):
        sl = slice(c * D_MODEL, (c + 1) * D_MODEL)
        for r in range(len(halves)):
            up = jnp.maximum(_dot(hf[r], wup_ref[:, sl]), 0.0)
            acc[r] = acc[r] + _dot((up * up).astype(BF16), wdn_ref[sl, :])
    for r, rows in enumerate(halves):
        out_ref[rows, :] = _rms(acc[r], gfin_ref[...]) if final else acc[r]


def _ffn(l, x2, g_ffn, wup, wdn, g_final, final):
    row = pl.BlockSpec((MERGE_ROWS, D_MODEL), lambda i: (i, 0))
    return pl.pallas_call(
        functools.partial(_ffn_kernel, final=final),
        grid=(N_TOK // MERGE_ROWS,),
        in_specs=[row] + [_layer_spec(p, l) for p in (g_ffn, wup, wdn)] + [_const_spec((1, D_MODEL))],
        out_specs=row,
        out_shape=jax.ShapeDtypeStruct((N_TOK, D_MODEL), F32),
        compiler_params=_params("parallel"),
        name="ffn",
    )(x2, g_ffn, wup, wdn, g_final)


def kernel(x, positions, g_mix, w_in, w_gate, b_gate, rel_bias, g_q_lat, w_uq, g_kv_lat, w_ukv, b_forget,
           w_dw, b_dw, g_conv_ln, b_conv_ln, w_branch, w_o, g_ffn, w_up, w_down, g_final):
    x2 = x.reshape(N_TOK, D_MODEL)
    cos_t, sin_t = _rope_tables(positions)
    consts = _inproj_consts()
    w, wuq, wukv, bf = _inproj_weights(w_in, w_uq, w_ukv, b_forget)
    tbl = _band_table(rel_bias)
    w_taps = jnp.pad(w_dw, ((0, 0), (0, 1), (0, 0)))
    wg, wbr, wo, wup, wdn = (p.astype(BF16) for p in (w_gate, w_branch, w_o, w_up, w_down))
    conv_params = (w_taps, _row(b_dw), _row(g_conv_ln), _row(b_conv_ln))
    for l in range(DEPTH):
        (h, qa, ka, va, qb, kb, vb, qc, kc, vc, od) = _inproj(
            l, x2, _row(g_mix), w, wuq, wukv, bf, _row(g_q_lat), _row(g_kv_lat), cos_t, sin_t, conv_params, consts)
        oa = _band_attention(l, qa, ka, va, tbl)
        ob = _flash_attention(qb, kb, vb, CHUNK, "flash_b")
        oc = _flash_attention(qc, kc, vc, 1, "flash_c")
        x2 = _merge(l, x2, h, oa, ob, oc, od, wg, _row(b_gate), wbr, wo)
        x2 = _ffn(l, x2, _row(g_ffn), wup, wdn, g_final.reshape(1, -1), final=(l == DEPTH - 1))
    return x2.reshape(BATCH, SEQ, D_MODEL)
```

```python
import functools

import numpy as np
import jax
import jax.numpy as jnp
from jax import lax
from jax.experimental import pallas as pl
from jax.experimental.pallas import tpu as pltpu

D_MODEL = 1024
BATCH = 4
SEQ = 4096
DEPTH = 2
N_TOK = BATCH * SEQ
CHUNK = 64
HEAD_DIM = 64
N_HEADS = 4
NORM_EPS = 1e-6
A_LEFT_CHUNKS = 8
A_MAX_REL = 128
B_Q_LORA = 256
B_KV_LORA = 128
B_NOPE = 64
B_ROPE = 32
B_V = 64
ROPE_THETA = 10000.0
CONV_CH = 256
CONV_K = 31
BRANCH_W = 256
D_FF = 4 * D_MODEL

LANES = 128
SUBLANES = 8
HEAD_PAD = LANES
QKV_PAD = N_HEADS * HEAD_PAD
NEG = -1e30
LOG2E = 1.4426950408889634

TM = 512
TILES_PER_SEQ = SEQ // TM
T_ATT = 512
TQ_A = 256
CONV_HALO = 32
CONV_ROWS = 64
FLASH_KEY_SPLIT = 2
MERGE_ROWS = 2 * TM
VMEM_LIMIT = 56 * 1024 * 1024

F32 = jnp.float32
BF16 = jnp.bfloat16


def _dot(a, b):
    return jnp.dot(a, b, preferred_element_type=F32)


def _dot_nt(a, b):
    return lax.dot_general(a, b, (((1,), (1,)), ((), ())), preferred_element_type=F32)


def _rms(x, g):
    return x * lax.rsqrt(jnp.mean(x * x, axis=-1, keepdims=True) + NORM_EPS) * g


def _sigmoid(x):
    return 1.0 / (1.0 + jnp.exp(-x))


def _split3(x):
    hi = x.astype(BF16).astype(F32)
    r1 = x - hi
    mid = r1.astype(BF16).astype(F32)
    return hi, mid, r1 - mid


def _emit_pipeline(n, lags, stage_a, stage_b, stage_c):
    for tick in range(n + lags[1]):
        if tick < n:
            stage_a(tick)
        if 0 <= tick - lags[0] < n:
            stage_b(tick - lags[0])
        if 0 <= tick - lags[1] < n:
            stage_c(tick - lags[1])


def _const_spec(shape):
    return pl.BlockSpec(shape, lambda *_: (0,) * len(shape))


def _layer_spec(arr, l):
    nd = arr.ndim - 1
    return pl.BlockSpec((None,) + tuple(arr.shape[1:]), lambda *_: (l,) + (0,) * nd,
                        pipeline_mode=pl.Buffered(1))


def _row(p):
    return p.reshape(p.shape[0], 1, p.shape[1])


def _params(*sem):
    return pltpu.CompilerParams(dimension_semantics=sem, vmem_limit_bytes=VMEM_LIMIT)


def _rope_kernel(pos_ref, invf_ref, cos_ref, sin_ref):
    half = B_ROPE // 2
    ang = invf_ref[...] * pos_ref[...].astype(F32)
    for ref, table, rest in ((cos_ref, jnp.cos(ang), 1.0), (sin_ref, jnp.sin(ang), 0.0)):
        ref[0:B_NOPE, :] = jnp.full((B_NOPE, ang.shape[1]), rest, F32)
        ref[B_NOPE:B_NOPE + half, :] = table
        ref[B_NOPE + half:B_NOPE + B_ROPE, :] = table
        ref[B_NOPE + B_ROPE:HEAD_PAD, :] = jnp.full((HEAD_PAD - B_NOPE - B_ROPE, ang.shape[1]), rest, F32)


def _rope_tables(positions):
    half = B_ROPE // 2
    inv_freq = 1.0 / (ROPE_THETA ** (jnp.arange(half, dtype=F32) / half))
    tm = 2048
    return pl.pallas_call(
        _rope_kernel,
        grid=(N_TOK // tm,),
        in_specs=[pl.BlockSpec((1, tm), lambda i: (0, i)), _const_spec((half, 1))],
        out_specs=[pl.BlockSpec((HEAD_PAD, tm), lambda i: (0, i))] * 2,
        out_shape=[jax.ShapeDtypeStruct((HEAD_PAD, N_TOK), F32)] * 2,
        compiler_params=_params("parallel"),
        name="rope_tables",
    )(positions.reshape(1, N_TOK), inv_freq.reshape(half, 1))


COL_A, COL_B, COL_C, COL_D, COL_END = 0, 768, 1536, 2304, 2816
F_COPIES = 9
AUG_PER_HEAD = 2 * F_COPIES


def _pick3(sel, parts):
    return jnp.where(sel == 0, parts[0], jnp.where(sel == 1, parts[1], parts[2]))


def _inproj_kernel(x_ref, gmix_ref, w_ref, cos_ref, sin_ref, gq_ref, wuq_ref, gkv_ref, wukv_ref, bf_ref,
                   wdw_ref, bdw_ref, gln_ref, bln_ref, sel_ref, tri_ref, e_ref, oneaug_ref, onev_ref, onesblk_ref,
                   h_ref, qa_ref, ka_ref, va_ref, qb_ref, kb_ref, vb_ref,
                   qc_ref, kc_ref, vc_ref, od_ref, carry_ref, ext_ref, shift_ref):
    i = pl.program_id(0)

    @pl.when(i % TILES_PER_SEQ == 0)
    def _():
        carry_ref[...] = jnp.zeros_like(carry_ref)
        ext_ref[0:CONV_HALO, :] = jnp.zeros((CONV_HALO, CONV_CH), F32)

    h = _rms(x_ref[...], gmix_ref[...]).astype(BF16)
    h_ref[...] = h

    pd = _dot_nt(h, w_ref[COL_D:COL_END, :])
    ext_ref[CONV_HALO:CONV_HALO + TM, :] = pd[:, 0:CONV_CH] * _sigmoid(pd[:, CONV_CH:2 * CONV_CH])
    od_ref[...] = _conv_ln_swish(ext_ref, shift_ref, wdw_ref, bdw_ref, gln_ref, bln_ref)
    ext_ref[0:CONV_HALO, :] = ext_ref[TM:TM + CONV_HALO, :]

    row = lax.broadcasted_iota(jnp.int32, (HEAD_PAD, TM), 0)
    own_rows = (row < HEAD_DIM, row >= HEAD_DIM)
    ones_blk = onesblk_ref[...]

    def pair_rows(q):
        qt = q.T
        return [jnp.where(own_rows[hd % 2], qt[(hd // 2) * HEAD_PAD:(hd // 2 + 1) * HEAD_PAD], 0.0)
                for hd in range(N_HEADS)]

    def v_rows(v):
        vt = v.T
        blocks = []
        for hd in range(N_HEADS):
            blocks += [vt[hd * HEAD_DIM:(hd + 1) * HEAD_DIM], ones_blk]
        return jnp.concatenate(blocks, axis=0).astype(BF16)

    pa = _dot_nt(h, w_ref[COL_A:COL_B, :])
    ka_ref[...] = pa[:, 256:512].astype(BF16)
    qa_ref[...] = jnp.concatenate(pair_rows(pa[:, 0:256] * LOG2E), axis=0).astype(BF16)
    va_ref[...] = v_rows(pa[:, 512:768])

    pb = _dot_nt(h, w_ref[COL_B:COL_C, :])
    cos = cos_ref[...].T
    sin = sin_ref[...].T
    cos4 = jnp.concatenate([cos] * N_HEADS, axis=-1)
    sin4 = jnp.concatenate([sin] * N_HEADS, axis=-1)
    qn = _rms(pb[:, 0:B_Q_LORA], gq_ref[...]).astype(BF16)
    qq = _dot(qn, wuq_ref[...])
    qb = qq[:, :QKV_PAD] * cos4 + qq[:, QKV_PAD:] * sin4
    qb_ref[...] = (qb * ((B_NOPE + B_ROPE) ** -0.5 * LOG2E)).T.astype(BF16)
    kvn = _rms(pb[:, B_Q_LORA:B_Q_LORA + B_KV_LORA], gkv_ref[...]).astype(BF16)
    krot = pb[:, 384:512] * cos + pb[:, 512:640] * sin
    kv = _dot(kvn, wukv_ref[...])
    kb_ref[...] = (kv[:, :QKV_PAD] + jnp.concatenate([krot] * N_HEADS, axis=-1)).astype(BF16)
    vb_ref[...] = (kv[:, QKV_PAD:] + onev_ref[...]).T.astype(BF16)

    pc = _dot_nt(h, w_ref[COL_C:COL_D, :])
    piece_a = sel_ref[0:1, :]
    piece_b = sel_ref[1:2, :]
    z = pb[:, 640:768] + bf_ref[...]
    logf = jnp.minimum(z, 0.0) - jnp.log(1.0 + jnp.exp(-jnp.abs(z)))
    part = _dot(tri_ref[...], _pick3(piece_a, _split3(logf)).astype(BF16))
    g = part * LOG2E + carry_ref[...]
    last = g[TM - SUBLANES:TM, :]
    total = last + pltpu.roll(last, LANES - 3 * N_HEADS, 1) + pltpu.roll(last, LANES - 6 * N_HEADS, 1)
    carry_ref[...] = jnp.where(sel_ref[2:3, :] == 1, total[SUBLANES - 1:], 0.0)
    aug = _dot(_pick3(piece_b, _split3(g)).astype(BF16), e_ref[...]) + oneaug_ref[...]
    kc_ref[...] = jnp.concatenate([pc[:, 256:512], aug[:, LANES:]], axis=-1).astype(BF16)
    aug_t = aug[:, :LANES].T
    q_blocks = []
    for hd, pair in enumerate(pair_rows(pc[:, 0:256] * LOG2E)):
        mine = jnp.where(row >= hd * AUG_PER_HEAD, jnp.where(row < (hd + 1) * AUG_PER_HEAD, aug_t, 0.0), 0.0)
        q_blocks += [pair, mine]
    qc_ref[...] = jnp.concatenate(q_blocks, axis=0).astype(BF16)
    vc_ref[...] = v_rows(pc[:, 512:768])


def _pad_last(w, before, after):
    return jnp.pad(w, ((0, 0),) * (w.ndim - 1) + ((before, after),))


def _pad_heads(w):
    return _pad_last(w, 0, HEAD_PAD - w.shape[-1]).reshape(w.shape[:-2] + (QKV_PAD,))


def _rot_half_cols(w):
    half = B_ROPE // 2
    return jnp.concatenate([-w[..., half:], w[..., :half]], axis=-1)


def _inproj_consts():
    tri = np.tril(np.ones((TM, TM), np.float32))
    sel = np.full((SUBLANES, LANES), 2, np.int32)
    e = np.zeros((LANES, 2 * LANES), np.float32)
    oneaug = np.zeros((1, 2 * LANES), np.float32)
    for copy in range(F_COPIES):
        for hd in range(N_HEADS):
            lane = copy * N_HEADS + hd
            sel[0, lane], sel[1, lane] = divmod(copy, 3)
            e[lane, hd * AUG_PER_HEAD + copy] = 1.0
            oneaug[0, LANES + hd * AUG_PER_HEAD + copy] = 1.0
            e[lane, LANES + hd * AUG_PER_HEAD + F_COPIES + copy] = -1.0
            oneaug[0, hd * AUG_PER_HEAD + F_COPIES + copy] = 1.0
    sel[2, :] = 0
    sel[2, :3 * N_HEADS] = 1
    onev = np.zeros((1, QKV_PAD), np.float32)
    onev[0, HEAD_DIM::HEAD_PAD] = 1.0
    ones_blk = np.zeros((HEAD_DIM, TM), np.float32)
    ones_blk[0] = 1.0
    return (jnp.asarray(sel), jnp.asarray(tri, BF16), jnp.asarray(e, BF16), jnp.asarray(oneaug),
            jnp.asarray(onev), jnp.asarray(ones_blk))


def _inproj_weights(w_in, w_uq, w_ukv, b_forget):
    wt = jnp.swapaxes(w_in, 1, 2).astype(BF16)
    o = 0
    cols = {}
    for name, width in (("aq", 256), ("ak", 256), ("av", 256), ("bq", B_Q_LORA), ("bkv", B_KV_LORA),
                        ("bkr", B_ROPE), ("cq", 256), ("ck", 256), ("cv", 256), ("cf", N_HEADS),
                        ("dv", CONV_CH), ("dg", CONV_CH)):
        cols[name] = wt[:, o:o + width, :]
        o += width
    scale = HEAD_DIM ** -0.5

    def pad_rows(w, before, after):
        return jnp.pad(w, ((0, 0), (before, after), (0, 0)))

    def rope_block(w):
        return pad_rows(w, B_NOPE, HEAD_PAD - B_NOPE - B_ROPE)

    half = B_ROPE // 2
    bkr_rot = jnp.concatenate([-cols["bkr"][:, half:], cols["bkr"][:, :half]], axis=1)
    cf = pad_rows(jnp.tile(cols["cf"], (1, F_COPIES, 1)), 0, LANES - F_COPIES * N_HEADS)
    w = jnp.concatenate([cols["aq"] * scale, cols["ak"], cols["av"],
                         cols["bq"], cols["bkv"], rope_block(cols["bkr"]), rope_block(bkr_rot), cf,
                         cols["cq"] * scale, cols["ck"], cols["cv"],
                         cols["dv"], cols["dg"]], axis=1)

    uq = w_uq.reshape(DEPTH, B_Q_LORA, N_HEADS, B_NOPE + B_ROPE)
    uq_rot = jnp.concatenate([jnp.zeros_like(uq[..., :B_NOPE]), _rot_half_cols(uq[..., B_NOPE:])], axis=-1)
    wuq = jnp.concatenate([_pad_heads(uq), _pad_heads(uq_rot)], axis=-1).astype(BF16)
    ukv = w_ukv.reshape(DEPTH, B_KV_LORA, N_HEADS, B_NOPE + B_V)
    wukv = jnp.concatenate([_pad_heads(ukv[..., :B_NOPE]), _pad_heads(ukv[..., B_NOPE:])], axis=-1).astype(BF16)
    bf = _row(_pad_last(jnp.tile(b_forget, (1, F_COPIES)), 0, LANES - F_COPIES * N_HEADS))
    return w, wuq, wukv, bf


def _inproj(l, x2, g_mix, w, wuq, wukv, bf, g_q_lat, g_kv_lat, cos_t, sin_t, conv_params, consts):
    assert w.shape == (DEPTH, COL_END, D_MODEL)
    row = lambda w: pl.BlockSpec((TM, w), lambda i: (i, 0))
    layer = lambda a: (a, _layer_spec(a, l))
    rope_spec = pl.BlockSpec((HEAD_PAD, TM), lambda i: (0, i))
    ins = [
        (x2, row(D_MODEL)), layer(g_mix), layer(w), (cos_t, rope_spec), (sin_t, rope_spec),
        layer(g_q_lat), layer(wuq), layer(g_kv_lat), layer(wukv), layer(bf),
    ] + [layer(p) for p in conv_params] + [(c, None) for c in consts]
    args = [a for a, _ in ins]
    specs = [s if s is not None else _const_spec(a.shape) for a, s in ins]

    def rows(w, dt):
        return jax.ShapeDtypeStruct((N_TOK, w), dt), row(w)

    def slab(nrows):
        return (jax.ShapeDtypeStruct((N_TOK // TM, nrows, TM), BF16),
                pl.BlockSpec((None, nrows, TM), lambda i: (i, 0, 0)))

    outs = [rows(D_MODEL, BF16), slab(QKV_PAD), rows(256, BF16), slab(QKV_PAD),
            slab(QKV_PAD), rows(QKV_PAD, BF16), slab(QKV_PAD),
            slab(2 * QKV_PAD), rows(3 * HEAD_PAD, BF16), slab(QKV_PAD), rows(CONV_CH, BF16)]
    return pl.pallas_call(
        _inproj_kernel,
        grid=(N_TOK // TM,),
        in_specs=specs,
        out_specs=[s for _, s in outs],
        out_shape=[a for a, _ in outs],
        scratch_shapes=[pltpu.VMEM((1, LANES), F32), pltpu.VMEM((CONV_HALO + TM, CONV_CH), F32),
                        pltpu.VMEM((SUBLANES, CONV_HALO + TM, CONV_CH), F32)],
        compiler_params=_params("arbitrary"),
        name="inproj",
    )(*args)


def _band_kernel(qt_ref, kp_ref, kc_ref, vtp_ref, vtc_ref, tbl_ref, o_ref):
    i = pl.program_id(1)
    w = TQ_A
    k_blocks = [(kp_ref, 0), (kp_ref, 1), (kc_ref, 0), (kc_ref, 1)]
    vt_blocks = [(vtp_ref, 0), (vtp_ref, 1), (vtc_ref, 0), (vtc_ref, 1)]

    m, acc = {}, {}

    def scores(t, j, hd):
        pair = slice((hd // 2) * HEAD_PAD, (hd // 2 + 1) * HEAD_PAD)
        qh = qt_ref[hd * HEAD_PAD:(hd + 1) * HEAD_PAD, t * w:(t + 1) * w]
        ref, half = k_blocks[t + j]
        s = _dot(ref[half * w:(half + 1) * w, pair], qh) + tbl_ref[j, hd]
        if t + j < 2:
            s = jnp.where(i > 0, s, NEG)
        return s

    def softmax(t, j, hd, s):
        smax = s.max(0, keepdims=True)
        if j == 0:
            m[t, hd], alpha = smax, None
        else:
            m_new = jnp.maximum(m[t, hd], smax)
            alpha = jnp.exp2(m[t, hd] - m_new)
            m[t, hd] = m_new
        return alpha, jnp.exp2(s - m[t, hd]).astype(BF16)

    def update(t, j, hd, alpha, p):
        ref, half = vt_blocks[t + j]
        part = _dot(ref[hd * HEAD_PAD:(hd + 1) * HEAD_PAD, half * w:(half + 1) * w], p)
        acc[t, hd] = part if alpha is None else alpha * acc[t, hd] + part

    units = [(t, j, hd) for t in range(TM // w) for j in range(3) for hd in range(N_HEADS)]
    s, sm = {}, {}
    _emit_pipeline(len(units), (2, 5),
                   lambda u: s.__setitem__(u, scores(*units[u])),
                   lambda u: sm.__setitem__(u, softmax(*units[u], s.pop(u))),
                   lambda u: update(*units[u], *sm.pop(u)))
    for t in range(TM // w):
        outs = [acc[t, hd][:HEAD_DIM, :] / acc[t, hd][HEAD_DIM:HEAD_DIM + 1, :] for hd in range(N_HEADS)]
        o_ref[t * w:(t + 1) * w, :] = jnp.concatenate(outs, axis=0).T.astype(BF16)


def _band_table(rel_bias):
    w = TQ_A
    key = np.arange(w)[:, None]
    qry = np.arange(w)[None, :]
    diags, valid = [], []
    for j in range(3):
        rel = (2 - j) * w + (np.arange(2 * w) - (w - 1))
        diags.append(rel_bias[..., np.clip(rel, -A_MAX_REL, A_MAX_REL) + A_MAX_REL])
        cd = (2 - j) * (w // CHUNK) + qry // CHUNK - key // CHUNK
        valid.append((cd >= 0) & (cd <= A_LEFT_CHUNKS))
    diags = jnp.stack(diags, axis=1)
    valid = jnp.asarray(np.stack(valid), F32)
    return pl.pallas_call(
        _band_table_kernel,
        grid=(DEPTH, 3),
        in_specs=[pl.BlockSpec((None, None, N_HEADS, 2 * w), lambda l, j: (l, j, 0, 0)),
                  pl.BlockSpec((None, w, w), lambda l, j: (j, 0, 0))],
        out_specs=pl.BlockSpec((None, None, N_HEADS, w, w), lambda l, j: (l, j, 0, 0, 0)),
        out_shape=jax.ShapeDtypeStruct((DEPTH, 3, N_HEADS, w, w), F32),
        compiler_params=_params("parallel", "parallel"),
        name="band_table",
    )(diags, valid)


def _band_table_kernel(diag_ref, valid_ref, o_ref):
    for hd in range(N_HEADS):
        rows = jnp.broadcast_to(diag_ref[hd:hd + 1, :], (TQ_A, 2 * TQ_A))
        toeplitz = pltpu.roll(rows, TQ_A + 1, 1, stride=1, stride_axis=0)[:, :TQ_A]
        o_ref[hd] = jnp.where(valid_ref[...] > 0.0, toeplitz * LOG2E, NEG)


def _band_attention(l, qat, ka, vat, tbl):
    assert TM == 2 * TQ_A
    nt = TILES_PER_SEQ

    def cur(b, i):
        return b * nt + i

    def prev(b, i):
        return b * nt + jnp.maximum(i - 1, 0)

    def slab_spec(which):
        return pl.BlockSpec((None, QKV_PAD, TM), lambda b, i: (which(b, i), 0, 0))

    def k_spec(which):
        return pl.BlockSpec((TM, 256), lambda b, i: (which(b, i), 0))

    return pl.pallas_call(
        _band_kernel,
        grid=(BATCH, nt),
        in_specs=[slab_spec(cur), k_spec(prev), k_spec(cur), slab_spec(prev), slab_spec(cur), _layer_spec(tbl, l)],
        out_specs=pl.BlockSpec((TM, 256), lambda b, i: (b * nt + i, 0)),
        out_shape=jax.ShapeDtypeStruct((N_TOK, 256), BF16),
        compiler_params=_params("parallel", "parallel"),
        name="band_attention",
    )(qat, ka, ka, vat, vat, tbl)


def _flash_kernel(qt_ref, k_ref, vt_ref, o_ref, *, unit, pair_layout, lags):
    i = pl.program_id(1)
    t = T_ATT
    qrows = 2 * HEAD_PAD if pair_layout else HEAD_PAD

    def visible(sub, tk):
        key = lax.broadcasted_iota(jnp.int32, (tk, t), 0) + sub * tk
        last_visible = lax.broadcasted_iota(jnp.int32, (1, t), 1) | (unit - 1)
        return key <= last_visible

    def k_tile(k_ref, off, hd, rows):
        if not pair_layout:
            return k_ref[pl.ds(off, rows), hd * HEAD_PAD:(hd + 1) * HEAD_PAD]
        pair = hd // 2
        return jnp.concatenate([k_ref[pl.ds(off, rows), pair * HEAD_PAD:(pair + 1) * HEAD_PAD],
                                k_ref[pl.ds(off, rows), 2 * HEAD_PAD:3 * HEAD_PAD]], axis=1)

    def step(tiles, carry):
        heads = [slice(hd * HEAD_PAD, (hd + 1) * HEAD_PAD) for hd in range(N_HEADS)]
        m = [c[0] for c in carry]
        acc = [c[1] for c in carry]

        tk = t // FLASH_KEY_SPLIT

        def scores(j, sub, diag, hd):
            off = pl.multiple_of(j * t, t) + sub * tk
            s = _dot(k_tile(k_ref, off, hd, tk), qt_ref[hd * qrows:(hd + 1) * qrows, :])
            return jnp.where(visible(sub, tk), s, NEG) if diag else s

        def softmax(hd, s):
            m_new = jnp.maximum(m[hd], s.max(0, keepdims=True))
            alpha = jnp.exp2(m[hd] - m_new)
            m[hd] = m_new
            return alpha, jnp.exp2(s - m_new).astype(BF16)

        def update(j, sub, hd, alpha, p):
            acc[hd] = alpha * acc[hd] + _dot(vt_ref[j, heads[hd], sub * tk:(sub + 1) * tk], p)

        units = [(j, sub, diag, hd) for j, diag in tiles for sub in range(FLASH_KEY_SPLIT)
                 for hd in range(N_HEADS)]
        s, sm = {}, {}
        _emit_pipeline(len(units), lags,
                       lambda u: s.__setitem__(u, scores(*units[u])),
                       lambda u: sm.__setitem__(u, softmax(units[u][3], s.pop(u))),
                       lambda u: update(units[u][0], units[u][1], units[u][3], *sm.pop(u)))
        return tuple(zip(m, acc))

    init = tuple((jnp.full((1, t), NEG, F32), jnp.zeros((HEAD_PAD, t), F32)) for _ in range(N_HEADS))
    carry = lax.fori_loop(0, i // 2, lambda jj, c: step([(2 * jj, False), (2 * jj + 1, False)], c), init)
    def finish(tiles):
        outs = [acc[:HEAD_DIM, :] / acc[HEAD_DIM:HEAD_DIM + 1, :] for _, acc in step(tiles, carry)]
        o_ref[...] = jnp.concatenate(outs, axis=0).T.astype(BF16)

    @pl.when(i % 2 == 1)
    def _():
        finish([(i - 1, False), (i, True)])

    @pl.when(i % 2 == 0)
    def _():
        finish([(i, True)])


def _flash_attention(qt, k, vt, unit, name):
    nt = SEQ // T_ATT
    pair_layout = k.shape[1] == 3 * HEAD_PAD
    return pl.pallas_call(
        functools.partial(_flash_kernel, unit=unit, pair_layout=pair_layout,
                          lags=(2, 4) if pair_layout else (2, 6)),
        grid=(BATCH, nt),
        in_specs=[pl.BlockSpec((None, qt.shape[1], T_ATT), lambda b, i: (b * nt + i, 0, 0)),
                  pl.BlockSpec((SEQ, k.shape[1]), lambda b, i: (b, 0)),
                  pl.BlockSpec((nt, QKV_PAD, T_ATT), lambda b, i: (b, 0, 0))],
        out_specs=pl.BlockSpec((T_ATT, 256), lambda b, i: (b * nt + i, 0)),
        out_shape=jax.ShapeDtypeStruct((N_TOK, 256), BF16),
        compiler_params=_params("parallel", "parallel"),
        name=name,
    )(qt, k, vt)


def _conv_ln_swish(ext_ref, shift_ref, w_ref, b_ref, g_ref, beta_ref):
    first = CONV_HALO - (CONV_K - 1)
    for res in range(SUBLANES):
        span = max((first + kk) // SUBLANES for kk in range(CONV_K) if (first + kk) % SUBLANES == res) * SUBLANES + TM
        shift_ref[res, 0:span, :] = ext_ref[res:res + span, :]
    chunks = []
    for c in range(TM // CONV_ROWS):
        acc = jnp.zeros((CONV_ROWS, CONV_CH), F32) + b_ref[...]
        for kk in range(CONV_K):
            start = (first + kk) // SUBLANES * SUBLANES + c * CONV_ROWS
            acc = acc + shift_ref[(first + kk) % SUBLANES, start:start + CONV_ROWS, :] * w_ref[kk:kk + 1, :]
        mu = jnp.mean(acc, axis=-1, keepdims=True)
        xc = acc - mu
        var = jnp.mean(xc * xc, axis=-1, keepdims=True)
        y = xc * lax.rsqrt(var + NORM_EPS) * g_ref[...] + beta_ref[...]
        chunks.append((y * _sigmoid(y)).astype(BF16))
    return jnp.concatenate(chunks, axis=0)


def _merge_kernel(x_ref, h_ref, oa_ref, ob_ref, oc_ref, od_ref, wg_ref, bg_ref, wbr_ref, wo_ref, out_ref):
    halves = [slice(r * TM, (r + 1) * TM) for r in range(MERGE_ROWS // TM)]
    acc = [jnp.zeros((TM, D_MODEL), F32) for _ in halves]
    for n, o_ref in enumerate((oa_ref, ob_ref, oc_ref, od_ref)):
        sl = slice(n * D_MODEL, (n + 1) * D_MODEL)
        for r, rows in enumerate(halves):
            gate = _sigmoid(_dot(h_ref[rows, :], wg_ref[:, sl]) + bg_ref[:, sl])
            acc[r] = acc[r] + gate * _dot(o_ref[rows, :], wbr_ref[n])
    for r, rows in enumerate(halves):
        out_ref[rows, :] = x_ref[rows, :] + _dot(acc[r].astype(BF16), wo_ref[...])


def _merge(l, x2, h, oa, ob, oc, od, wg, bg, wbr, wo):
    row = lambda w: pl.BlockSpec((MERGE_ROWS, w), lambda i: (i, 0))
    return pl.pallas_call(
        _merge_kernel,
        grid=(N_TOK // MERGE_ROWS,),
        in_specs=[row(D_MODEL), row(D_MODEL)] + [row(BRANCH_W)] * 4
                 + [_layer_spec(p, l) for p in (wg, bg, wbr, wo)],
        out_specs=row(D_MODEL),
        out_shape=jax.ShapeDtypeStruct((N_TOK, D_MODEL), F32),
        compiler_params=_params("parallel"),
        name="merge",
    )(x2, h, oa, ob, oc, od, wg, bg, wbr, wo)


def _ffn_kernel(x_ref, g_ref, wup_ref, wdn_ref, gfin_ref, out_ref, *, final):
    halves = [slice(r * TM, (r + 1) * TM) for r in range(MERGE_ROWS // TM)]
    acc = [x_ref[rows, :] for rows in halves]
    hf = [_rms(a, g_ref[...]).astype(BF16) for a in acc]
    for c in range(D_FF // D_MODEL):
        sl = slice(c * D_MODEL, (c + 1) * D_MODEL)
        for r in range(len(halves)):
            up = jnp.maximum(_dot(hf[r], wup_ref[:, sl]), 0.0)
            acc[r] = acc[r] + _dot((up * up).astype(BF16), wdn_ref[sl, :])
    for r, rows in enumerate(halves):
        out_ref[rows, :] = _rms(acc[r], gfin_ref[...]) if final else acc[r]


def _ffn(l, x2, g_ffn, wup, wdn, g_final, final):
    row = pl.BlockSpec((MERGE_ROWS, D_MODEL), lambda i: (i, 0))
    return pl.pallas_call(
        functools.partial(_ffn_kernel, final=final),
        grid=(N_TOK // MERGE_ROWS,),
        in_specs=[row] + [_layer_spec(p, l) for p in (g_ffn, wup, wdn)] + [_const_spec((1, D_MODEL))],
        out_specs=row,
        out_shape=jax.ShapeDtypeStruct((N_TOK, D_MODEL), F32),
        compiler_params=_params("parallel"),
        name="ffn",
    )(x2, g_ffn, wup, wdn, g_final)


def kernel(x, positions, g_mix, w_in, w_gate, b_gate, rel_bias, g_q_lat, w_uq, g_kv_lat, w_ukv, b_forget,
           w_dw, b_dw, g_conv_ln, b_conv_ln, w_branch, w_o, g_ffn, w_up, w_down, g_final):
    x2 = x.reshape(N_TOK, D_MODEL)
    cos_t, sin_t = _rope_tables(positions)
    consts = _inproj_consts()
    w, wuq, wukv, bf = _inproj_weights(w_in, w_uq, w_ukv, b_forget)
    tbl = _band_table(rel_bias)
    w_taps = jnp.pad(w_dw, ((0, 0), (0, 1), (0, 0)))
    wg, wbr, wo, wup, wdn = (p.astype(BF16) for p in (w_gate, w_branch, w_o, w_up, w_down))
    conv_params = (w_taps, _row(b_dw), _row(g_conv_ln), _row(b_conv_ln))
    for l in range(DEPTH):
        (h, qa, ka, va, qb, kb, vb, qc, kc, vc, od) = _inproj(
            l, x2, _row(g_mix), w, wuq, wukv, bf, _row(g_q_lat), _row(g_kv_lat), cos_t, sin_t, conv_params, consts)
        oa = _band_attention(l, qa, ka, va, tbl)
        ob = _flash_attention(qb, kb, vb, CHUNK, "flash_b")
        oc = _flash_attention(qc, kc, vc, 1, "flash_c")
        x2 = _merge(l, x2, h, oa, ob, oc, od, wg, _row(b_gate), wbr, wo)
        x2 = _ffn(l, x2, _row(g_ffn), wup, wdn, g_final.reshape(1, -1), final=(l == DEPTH - 1))
    return x2.reshape(BATCH, SEQ, D_MODEL)
```

```python
import functools

import numpy as np
import jax
import jax.numpy as jnp
from jax import lax
from jax.experimental import pallas as pl
from jax.experimental.pallas import tpu as pltpu

D_MODEL = 1024
BATCH = 4
SEQ = 4096
DEPTH = 2
N_TOK = BATCH * SEQ
CHUNK = 64
HEAD_DIM = 64
N_HEADS = 4
NORM_EPS = 1e-6
A_LEFT_CHUNKS = 8
A_MAX_REL = 128
B_Q_LORA = 256
B_KV_LORA = 128
B_NOPE = 64
B_ROPE = 32
B_V = 64
ROPE_THETA = 10000.0
CONV_CH = 256
CONV_K = 31
BRANCH_W = 256
D_FF = 4 * D_MODEL

LANES = 128
SUBLANES = 8
HEAD_PAD = LANES
QKV_PAD = N_HEADS * HEAD_PAD
V_ROWS = 80
VT_ROWS = N_HEADS * V_ROWS
NEG = -1e30
LOG2E = 1.4426950408889634

TM = 512
TILES_PER_SEQ = SEQ // TM
T_ATT = 512
TQ_A = 256
CONV_HALO = 32
CONV_ROWS = 64
FLASH_KEY_SPLIT = 2
MERGE_ROWS = 2 * TM
VMEM_LIMIT = 56 * 1024 * 1024

F32 = jnp.float32
BF16 = jnp.bfloat16


def _dot(a, b):
    return jnp.dot(a, b, preferred_element_type=F32)


def _dot_nt(a, b):
    return lax.dot_general(a, b, (((1,), (1,)), ((), ())), preferred_element_type=F32)


def _rms(x, g):
    return x * lax.rsqrt(jnp.mean(x * x, axis=-1, keepdims=True) + NORM_EPS) * g


def _sigmoid(x):
    return 1.0 / (1.0 + jnp.exp(-x))


def _split3(x):
    hi = x.astype(BF16).astype(F32)
    r1 = x - hi
    mid = r1.astype(BF16).astype(F32)
    return hi, mid, r1 - mid


def _emit_pipeline(n, lags, stage_a, stage_b, stage_c):
    for tick in range(n + lags[1]):
        if tick < n:
            stage_a(tick)
        if 0 <= tick - lags[0] < n:
            stage_b(tick - lags[0])
        if 0 <= tick - lags[1] < n:
            stage_c(tick - lags[1])


def _const_spec(shape):
    return pl.BlockSpec(shape, lambda *_: (0,) * len(shape))


def _layer_spec(arr, l):
    nd = arr.ndim - 1
    return pl.BlockSpec((None,) + tuple(arr.shape[1:]), lambda *_: (l,) + (0,) * nd,
                        pipeline_mode=pl.Buffered(1))


def _row(p):
    return p.reshape(p.shape[0], 1, p.shape[1])


def _params(*sem):
    return pltpu.CompilerParams(dimension_semantics=sem, vmem_limit_bytes=VMEM_LIMIT)


def _rope_kernel(pos_ref, invf_ref, cos_ref, sin_ref):
    half = B_ROPE // 2
    ang = invf_ref[...] * pos_ref[...].astype(F32)
    for ref, table, rest in ((cos_ref, jnp.cos(ang), 1.0), (sin_ref, jnp.sin(ang), 0.0)):
        ref[0:B_NOPE, :] = jnp.full((B_NOPE, ang.shape[1]), rest, F32)
        ref[B_NOPE:B_NOPE + half, :] = table
        ref[B_NOPE + half:B_NOPE + B_ROPE, :] = table
        ref[B_NOPE + B_ROPE:HEAD_PAD, :] = jnp.full((HEAD_PAD - B_NOPE - B_ROPE, ang.shape[1]), rest, F32)


def _rope_tables(positions):
    half = B_ROPE // 2
    inv_freq = 1.0 / (ROPE_THETA ** (jnp.arange(half, dtype=F32) / half))
    tm = 2048
    return pl.pallas_call(
        _rope_kernel,
        grid=(N_TOK // tm,),
        in_specs=[pl.BlockSpec((1, tm), lambda i: (0, i)), _const_spec((half, 1))],
        out_specs=[pl.BlockSpec((HEAD_PAD, tm), lambda i: (0, i))] * 2,
        out_shape=[jax.ShapeDtypeStruct((HEAD_PAD, N_TOK), F32)] * 2,
        compiler_params=_params("parallel"),
        name="rope_tables",
    )(positions.reshape(1, N_TOK), inv_freq.reshape(half, 1))


COL_A, COL_B, COL_C, COL_D, COL_END = 0, 768, 1536, 2304, 2816
F_COPIES = 9
AUG_PER_HEAD = 2 * F_COPIES


def _pick3(sel, parts):
    return jnp.where(sel == 0, parts[0], jnp.where(sel == 1, parts[1], parts[2]))


def _inproj_kernel(x_ref, gmix_ref, w_ref, cos_ref, sin_ref, gq_ref, wuq_ref, gkv_ref, wukv_ref, bf_ref,
                   wdw_ref, bdw_ref, gln_ref, bln_ref, sel_ref, tri_ref, e_ref, oneaug_ref, onesblk_ref,
                   h_ref, qa_ref, ka_ref, va_ref, qb_ref, kb_ref, vb_ref,
                   qc_ref, kc_ref, vc_ref, od_ref, carry_ref, ext_ref, shift_ref):
    i = pl.program_id(0)

    @pl.when(i % TILES_PER_SEQ == 0)
    def _():
        carry_ref[...] = jnp.zeros_like(carry_ref)
        ext_ref[0:CONV_HALO, :] = jnp.zeros((CONV_HALO, CONV_CH), F32)

    h = _rms(x_ref[...], gmix_ref[...]).astype(BF16)
    h_ref[...] = h

    pd = _dot_nt(h, w_ref[COL_D:COL_END, :])
    ext_ref[CONV_HALO:CONV_HALO + TM, :] = pd[:, 0:CONV_CH] * _sigmoid(pd[:, CONV_CH:2 * CONV_CH])
    od_ref[...] = _conv_ln_swish(ext_ref, shift_ref, wdw_ref, bdw_ref, gln_ref, bln_ref)
    ext_ref[0:CONV_HALO, :] = ext_ref[TM:TM + CONV_HALO, :]

    row = lax.broadcasted_iota(jnp.int32, (HEAD_PAD, TM), 0)
    own_rows = (row < HEAD_DIM, row >= HEAD_DIM)
    ones_blk = onesblk_ref[...]

    def pair_rows(q):
        qt = q.T
        return [jnp.where(own_rows[hd % 2], qt[(hd // 2) * HEAD_PAD:(hd // 2 + 1) * HEAD_PAD], 0.0)
                for hd in range(N_HEADS)]

    def v_rows(v):
        vt = v.T
        blocks = []
        for hd in range(N_HEADS):
            blocks += [vt[hd * HEAD_DIM:(hd + 1) * HEAD_DIM], ones_blk]
        return jnp.concatenate(blocks, axis=0).astype(BF16)

    pa = _dot_nt(h, w_ref[COL_A:COL_B, :])
    ka_ref[...] = pa[:, 256:512].astype(BF16)
    qa_ref[...] = jnp.concatenate(pair_rows(pa[:, 0:256] * LOG2E), axis=0).astype(BF16)
    va_ref[...] = v_rows(pa[:, 512:768])

    pb = _dot_nt(h, w_ref[COL_B:COL_C, :])
    cos = cos_ref[...].T
    sin = sin_ref[...].T
    cos4 = jnp.concatenate([cos] * N_HEADS, axis=-1)
    sin4 = jnp.concatenate([sin] * N_HEADS, axis=-1)
    qn = _rms(pb[:, 0:B_Q_LORA], gq_ref[...]).astype(BF16)
    qq = _dot(qn, wuq_ref[...])
    qb = qq[:, :QKV_PAD] * cos4 + qq[:, QKV_PAD:] * sin4
    qb_ref[...] = (qb * ((B_NOPE + B_ROPE) ** -0.5 * LOG2E)).T.astype(BF16)
    kvn = _rms(pb[:, B_Q_LORA:B_Q_LORA + B_KV_LORA], gkv_ref[...]).astype(BF16)
    krot = pb[:, 384:512] * cos + pb[:, 512:640] * sin
    kv = _dot(kvn, wukv_ref[...])
    kb_ref[...] = (kv[:, :QKV_PAD] + jnp.concatenate([krot] * N_HEADS, axis=-1)).astype(BF16)
    vb_ref[...] = v_rows(kv[:, QKV_PAD:])

    pc = _dot_nt(h, w_ref[COL_C:COL_D, :])
    piece_a = sel_ref[0:1, :]
    piece_b = sel_ref[1:2, :]
    z = pb[:, 640:768] + bf_ref[...]
    logf = jnp.minimum(z, 0.0) - jnp.log(1.0 + jnp.exp(-jnp.abs(z)))
    part = _dot(tri_ref[...], _pick3(piece_a, _split3(logf)).astype(BF16))
    g = part * LOG2E + carry_ref[...]
    last = g[TM - SUBLANES:TM, :]
    total = last + pltpu.roll(last, LANES - 3 * N_HEADS, 1) + pltpu.roll(last, LANES - 6 * N_HEADS, 1)
    carry_ref[...] = jnp.where(sel_ref[2:3, :] == 1, total[SUBLANES - 1:], 0.0)
    aug = _dot(_pick3(piece_b, _split3(g)).astype(BF16), e_ref[...]) + oneaug_ref[...]
    kc_ref[...] = jnp.concatenate([pc[:, 256:512], aug[:, LANES:]], axis=-1).astype(BF16)
    aug_t = aug[:, :LANES].T
    q_blocks = []
    for hd, pair in enumerate(pair_rows(pc[:, 0:256] * LOG2E)):
        mine = jnp.where(row >= hd * AUG_PER_HEAD, jnp.where(row < (hd + 1) * AUG_PER_HEAD, aug_t, 0.0), 0.0)
        q_blocks += [pair, mine]
    qc_ref[...] = jnp.concatenate(q_blocks, axis=0).astype(BF16)
    vc_ref[...] = v_rows(pc[:, 512:768])


def _pad_last(w, before, after):
    return jnp.pad(w, ((0, 0),) * (w.ndim - 1) + ((before, after),))


def _pad_heads(w):
    return _pad_last(w, 0, HEAD_PAD - w.shape[-1]).reshape(w.shape[:-2] + (QKV_PAD,))


def _rot_half_cols(w):
    half = B_ROPE // 2
    return jnp.concatenate([-w[..., half:], w[..., :half]], axis=-1)


def _inproj_consts():
    tri = np.tril(np.ones((TM, TM), np.float32))
    sel = np.full((SUBLANES, LANES), 2, np.int32)
    e = np.zeros((LANES, 2 * LANES), np.float32)
    oneaug = np.zeros((1, 2 * LANES), np.float32)
    for copy in range(F_COPIES):
        for hd in range(N_HEADS):
            lane = copy * N_HEADS + hd
            sel[0, lane], sel[1, lane] = divmod(copy, 3)
            e[lane, hd * AUG_PER_HEAD + copy] = 1.0
            oneaug[0, LANES + hd * AUG_PER_HEAD + copy] = 1.0
            e[lane, LANES + hd * AUG_PER_HEAD + F_COPIES + copy] = -1.0
            oneaug[0, hd * AUG_PER_HEAD + F_COPIES + copy] = 1.0
    sel[2, :] = 0
    sel[2, :3 * N_HEADS] = 1
    ones_blk = np.zeros((V_ROWS - HEAD_DIM, TM), np.float32)
    ones_blk[0] = 1.0
    return (jnp.asarray(sel), jnp.asarray(tri, BF16), jnp.asarray(e, BF16), jnp.asarray(oneaug),
            jnp.asarray(ones_blk))


def _inproj_weights(w_in, w_uq, w_ukv, b_forget):
    wt = jnp.swapaxes(w_in, 1, 2).astype(BF16)
    o = 0
    cols = {}
    for name, width in (("aq", 256), ("ak", 256), ("av", 256), ("bq", B_Q_LORA), ("bkv", B_KV_LORA),
                        ("bkr", B_ROPE), ("cq", 256), ("ck", 256), ("cv", 256), ("cf", N_HEADS),
                        ("dv", CONV_CH), ("dg", CONV_CH)):
        cols[name] = wt[:, o:o + width, :]
        o += width
    scale = HEAD_DIM ** -0.5

    def pad_rows(w, before, after):
        return jnp.pad(w, ((0, 0), (before, after), (0, 0)))

    def rope_block(w):
        return pad_rows(w, B_NOPE, HEAD_PAD - B_NOPE - B_ROPE)

    half = B_ROPE // 2
    bkr_rot = jnp.concatenate([-cols["bkr"][:, half:], cols["bkr"][:, :half]], axis=1)
    cf = pad_rows(jnp.tile(cols["cf"], (1, F_COPIES, 1)), 0, LANES - F_COPIES * N_HEADS)
    w = jnp.concatenate([cols["aq"] * scale, cols["ak"], cols["av"],
                         cols["bq"], cols["bkv"], rope_block(cols["bkr"]), rope_block(bkr_rot), cf,
                         cols["cq"] * scale, cols["ck"], cols["cv"],
                         cols["dv"], cols["dg"]], axis=1)

    uq = w_uq.reshape(DEPTH, B_Q_LORA, N_HEADS, B_NOPE + B_ROPE)
    uq_rot = jnp.concatenate([jnp.zeros_like(uq[..., :B_NOPE]), _rot_half_cols(uq[..., B_NOPE:])], axis=-1)
    wuq = jnp.concatenate([_pad_heads(uq), _pad_heads(uq_rot)], axis=-1).astype(BF16)
    ukv = w_ukv.reshape(DEPTH, B_KV_LORA, N_HEADS, B_NOPE + B_V)
    wukv = jnp.concatenate([_pad_heads(ukv[..., :B_NOPE]),
                            ukv[..., B_NOPE:].reshape(DEPTH, B_KV_LORA, N_HEADS * B_V)], axis=-1).astype(BF16)
    bf = _row(_pad_last(jnp.tile(b_forget, (1, F_COPIES)), 0, LANES - F_COPIES * N_HEADS))
    return w, wuq, wukv, bf


def _inproj(l, x2, g_mix, w, wuq, wukv, bf, g_q_lat, g_kv_lat, cos_t, sin_t, conv_params, consts):
    assert w.shape == (DEPTH, COL_END, D_MODEL)
    row = lambda w: pl.BlockSpec((TM, w), lambda i: (i, 0))
    layer = lambda a: (a, _layer_spec(a, l))
    rope_spec = pl.BlockSpec((HEAD_PAD, TM), lambda i: (0, i))
    ins = [
        (x2, row(D_MODEL)), layer(g_mix), layer(w), (cos_t, rope_spec), (sin_t, rope_spec),
        layer(g_q_lat), layer(wuq), layer(g_kv_lat), layer(wukv), layer(bf),
    ] + [layer(p) for p in conv_params] + [(c, None) for c in consts]
    args = [a for a, _ in ins]
    specs = [s if s is not None else _const_spec(a.shape) for a, s in ins]

    def rows(w, dt):
        return jax.ShapeDtypeStruct((N_TOK, w), dt), row(w)

    def slab(nrows):
        return (jax.ShapeDtypeStruct((N_TOK // TM, nrows, TM), BF16),
                pl.BlockSpec((None, nrows, TM), lambda i: (i, 0, 0)))

    outs = [rows(D_MODEL, BF16), slab(QKV_PAD), rows(256, BF16), slab(VT_ROWS),
            slab(QKV_PAD), rows(QKV_PAD, BF16), slab(VT_ROWS),
            slab(2 * QKV_PAD), rows(3 * HEAD_PAD, BF16), slab(VT_ROWS), rows(CONV_CH, BF16)]
    return pl.pallas_call(
        _inproj_kernel,
        grid=(N_TOK // TM,),
        in_specs=specs,
        out_specs=[s for _, s in outs],
        out_shape=[a for a, _ in outs],
        scratch_shapes=[pltpu.VMEM((1, LANES), F32), pltpu.VMEM((CONV_HALO + TM, CONV_CH), F32),
                        pltpu.VMEM((SUBLANES, CONV_HALO + TM, CONV_CH), F32)],
        compiler_params=_params("arbitrary"),
        name="inproj",
    )(*args)


def _band_kernel(qt_ref, kp_ref, kc_ref, vtp_ref, vtc_ref, tbl_ref, o_ref):
    i = pl.program_id(1)
    w = TQ_A
    k_blocks = [(kp_ref, 0), (kp_ref, 1), (kc_ref, 0), (kc_ref, 1)]
    vt_blocks = [(vtp_ref, 0), (vtp_ref, 1), (vtc_ref, 0), (vtc_ref, 1)]

    m, acc = {}, {}

    def scores(t, j, hd):
        pair = slice((hd // 2) * HEAD_PAD, (hd // 2 + 1) * HEAD_PAD)
        qh = qt_ref[hd * HEAD_PAD:(hd + 1) * HEAD_PAD, t * w:(t + 1) * w]
        ref, half = k_blocks[t + j]
        s = _dot(ref[half * w:(half + 1) * w, pair], qh) + tbl_ref[j, hd]
        if t + j < 2:
            s = jnp.where(i > 0, s, NEG)
        return s

    def softmax(t, j, hd, s):
        smax = s.max(0, keepdims=True)
        if j == 0:
            m[t, hd], alpha = smax, None
        else:
            m_new = jnp.maximum(m[t, hd], smax)
            alpha = jnp.exp2(m[t, hd] - m_new)
            m[t, hd] = m_new
        return alpha, jnp.exp2(s - m[t, hd]).astype(BF16)

    def update(t, j, hd, alpha, p):
        ref, half = vt_blocks[t + j]
        part = _dot(ref[hd * V_ROWS:(hd + 1) * V_ROWS, half * w:(half + 1) * w], p)
        acc[t, hd] = part if alpha is None else alpha * acc[t, hd] + part

    units = [(t, j, hd) for t in range(TM // w) for j in range(3) for hd in range(N_HEADS)]
    s, sm = {}, {}
    _emit_pipeline(len(units), (2, 5),
                   lambda u: s.__setitem__(u, scores(*units[u])),
                   lambda u: sm.__setitem__(u, softmax(*units[u], s.pop(u))),
                   lambda u: update(*units[u], *sm.pop(u)))
    for t in range(TM // w):
        outs = [acc[t, hd][:HEAD_DIM, :] / acc[t, hd][HEAD_DIM:HEAD_DIM + 1, :] for hd in range(N_HEADS)]
        o_ref[t * w:(t + 1) * w, :] = jnp.concatenate(outs, axis=0).T.astype(BF16)


def _band_table(rel_bias):
    w = TQ_A
    key = np.arange(w)[:, None]
    qry = np.arange(w)[None, :]
    diags, valid = [], []
    for j in range(3):
        rel = (2 - j) * w + (np.arange(2 * w) - (w - 1))
        diags.append(rel_bias[..., np.clip(rel, -A_MAX_REL, A_MAX_REL) + A_MAX_REL])
        cd = (2 - j) * (w // CHUNK) + qry // CHUNK - key // CHUNK
        valid.append((cd >= 0) & (cd <= A_LEFT_CHUNKS))
    diags = jnp.stack(diags, axis=1)
    valid = jnp.asarray(np.stack(valid), F32)
    return pl.pallas_call(
        _band_table_kernel,
        grid=(DEPTH, 3),
        in_specs=[pl.BlockSpec((None, None, N_HEADS, 2 * w), lambda l, j: (l, j, 0, 0)),
                  pl.BlockSpec((None, w, w), lambda l, j: (j, 0, 0))],
        out_specs=pl.BlockSpec((None, None, N_HEADS, w, w), lambda l, j: (l, j, 0, 0, 0)),
        out_shape=jax.ShapeDtypeStruct((DEPTH, 3, N_HEADS, w, w), F32),
        compiler_params=_params("parallel", "parallel"),
        name="band_table",
    )(diags, valid)


def _band_table_kernel(diag_ref, valid_ref, o_ref):
    for hd in range(N_HEADS):
        rows = jnp.broadcast_to(diag_ref[hd:hd + 1, :], (TQ_A, 2 * TQ_A))
        toeplitz = pltpu.roll(rows, TQ_A + 1, 1, stride=1, stride_axis=0)[:, :TQ_A]
        o_ref[hd] = jnp.where(valid_ref[...] > 0.0, toeplitz * LOG2E, NEG)


def _band_attention(l, qat, ka, vat, tbl):
    assert TM == 2 * TQ_A
    nt = TILES_PER_SEQ

    def cur(b, i):
        return b * nt + i

    def prev(b, i):
        return b * nt + jnp.maximum(i - 1, 0)

    def slab_spec(which, nrows=QKV_PAD):
        return pl.BlockSpec((None, nrows, TM), lambda b, i: (which(b, i), 0, 0))

    def k_spec(which):
        return pl.BlockSpec((TM, 256), lambda b, i: (which(b, i), 0))

    return pl.pallas_call(
        _band_kernel,
        grid=(BATCH, nt),
        in_specs=[slab_spec(cur), k_spec(prev), k_spec(cur), slab_spec(prev, VT_ROWS), slab_spec(cur, VT_ROWS),
                  _layer_spec(tbl, l)],
        out_specs=pl.BlockSpec((TM, 256), lambda b, i: (b * nt + i, 0)),
        out_shape=jax.ShapeDtypeStruct((N_TOK, 256), BF16),
        compiler_params=_params("parallel", "parallel"),
        name="band_attention",
    )(qat, ka, ka, vat, vat, tbl)


def _flash_kernel(qt_ref, k_ref, vt_ref, o_ref, *, unit, pair_layout, lags):
    i = pl.program_id(1)
    t = T_ATT
    qrows = 2 * HEAD_PAD if pair_layout else HEAD_PAD

    def visible(sub, tk):
        key = lax.broadcasted_iota(jnp.int32, (tk, t), 0) + sub * tk
        last_visible = lax.broadcasted_iota(jnp.int32, (1, t), 1) | (unit - 1)
        return key <= last_visible

    def k_tile(k_ref, off, hd, rows):
        if not pair_layout:
            return k_ref[pl.ds(off, rows), hd * HEAD_PAD:(hd + 1) * HEAD_PAD]
        pair = hd // 2
        return jnp.concatenate([k_ref[pl.ds(off, rows), pair * HEAD_PAD:(pair + 1) * HEAD_PAD],
                                k_ref[pl.ds(off, rows), 2 * HEAD_PAD:3 * HEAD_PAD]], axis=1)

    def step(tiles, carry):
        heads = [slice(hd * V_ROWS, (hd + 1) * V_ROWS) for hd in range(N_HEADS)]
        m = [c[0] for c in carry]
        acc = [c[1] for c in carry]

        tk = t // FLASH_KEY_SPLIT

        def scores(j, sub, diag, hd):
            off = pl.multiple_of(j * t, t) + sub * tk
            s = _dot(k_tile(k_ref, off, hd, tk), qt_ref[hd * qrows:(hd + 1) * qrows, :])
            return jnp.where(visible(sub, tk), s, NEG) if diag else s

        def softmax(hd, s):
            m_new = jnp.maximum(m[hd], s.max(0, keepdims=True))
            alpha = jnp.exp2(m[hd] - m_new)
            m[hd] = m_new
            return alpha, jnp.exp2(s - m_new).astype(BF16)

        def update(j, sub, hd, alpha, p):
            acc[hd] = alpha * acc[hd] + _dot(vt_ref[j, heads[hd], sub * tk:(sub + 1) * tk], p)

        units = [(j, sub, diag, hd) for j, diag in tiles for sub in range(FLASH_KEY_SPLIT)
                 for hd in range(N_HEADS)]
        s, sm = {}, {}
        _emit_pipeline(len(units), lags,
                       lambda u: s.__setitem__(u, scores(*units[u])),
                       lambda u: sm.__setitem__(u, softmax(units[u][3], s.pop(u))),
                       lambda u: update(units[u][0], units[u][1], units[u][3], *sm.pop(u)))
        return tuple(zip(m, acc))

    init = tuple((jnp.full((1, t), NEG, F32), jnp.zeros((V_ROWS, t), F32)) for _ in range(N_HEADS))
    carry = lax.fori_loop(0, i // 2, lambda jj, c: step([(2 * jj, False), (2 * jj + 1, False)], c), init)
    def finish(tiles):
        outs = [acc[:HEAD_DIM, :] / acc[HEAD_DIM:HEAD_DIM + 1, :] for _, acc in step(tiles, carry)]
        o_ref[...] = jnp.concatenate(outs, axis=0).T.astype(BF16)

    @pl.when(i % 2 == 1)
    def _():
        finish([(i - 1, False), (i, True)])

    @pl.when(i % 2 == 0)
    def _():
        finish([(i, True)])


def _flash_attention(qt, k, vt, unit, name):
    nt = SEQ // T_ATT
    pair_layout = k.shape[1] == 3 * HEAD_PAD
    return pl.pallas_call(
        functools.partial(_flash_kernel, unit=unit, pair_layout=pair_layout,
                          lags=(2, 4) if pair_layout else (2, 6)),
        grid=(BATCH, nt),
        in_specs=[pl.BlockSpec((None, qt.shape[1], T_ATT), lambda b, i: (b * nt + i, 0, 0)),
                  pl.BlockSpec((SEQ, k.shape[1]), lambda b, i: (b, 0)),
                  pl.BlockSpec((nt, VT_ROWS, T_ATT), lambda b, i: (b, 0, 0))],
        out_specs=pl.BlockSpec((T_ATT, 256), lambda b, i: (b * nt + i, 0)),
        out_shape=jax.ShapeDtypeStruct((N_TOK, 256), BF16),
        compiler_params=_params("parallel", "parallel"),
        name=name,
    )(qt, k, vt)


def _conv_ln_swish(ext_ref, shift_ref, w_ref, b_ref, g_ref, beta_ref):
    first = CONV_HALO - (CONV_K - 1)
    for res in range(SUBLANES):
        span = max((first + kk) // SUBLANES for kk in range(CONV_K) if (first + kk) % SUBLANES == res) * SUBLANES + TM
        shift_ref[res, 0:span, :] = ext_ref[res:res + span, :]
    chunks = []
    for c in range(TM // CONV_ROWS):
        acc = jnp.zeros((CONV_ROWS, CONV_CH), F32) + b_ref[...]
        for kk in range(CONV_K):
            start = (first + kk) // SUBLANES * SUBLANES + c * CONV_ROWS
            acc = acc + shift_ref[(first + kk) % SUBLANES, start:start + CONV_ROWS, :] * w_ref[kk:kk + 1, :]
        mu = jnp.mean(acc, axis=-1, keepdims=True)
        xc = acc - mu
        var = jnp.mean(xc * xc, axis=-1, keepdims=True)
        y = xc * lax.rsqrt(var + NORM_EPS) * g_ref[...] + beta_ref[...]
        chunks.append((y * _sigmoid(y)).astype(BF16))
    return jnp.concatenate(chunks, axis=0)


def _merge_kernel(x_ref, h_ref, oa_ref, ob_ref, oc_ref, od_ref, wg_ref, bg_ref, wbr_ref, wo_ref, out_ref):
    halves = [slice(r * TM, (r + 1) * TM) for r in range(MERGE_ROWS // TM)]
    acc = [jnp.zeros((TM, D_MODEL), F32) for _ in halves]
    for n, o_ref in enumerate((oa_ref, ob_ref, oc_ref, od_ref)):
        sl = slice(n * D_MODEL, (n + 1) * D_MODEL)
        for r, rows in enumerate(halves):
            gate = _sigmoid(_dot(h_ref[rows, :], wg_ref[:, sl]) + bg_ref[:, sl])
            acc[r] = acc[r] + gate * _dot(o_ref[rows, :], wbr_ref[n])
    for r, rows in enumerate(halves):
        out_ref[rows, :] = x_ref[rows, :] + _dot(acc[r].astype(BF16), wo_ref[...])


def _merge(l, x2, h, oa, ob, oc, od, wg, bg, wbr, wo):
    row = lambda w: pl.BlockSpec((MERGE_ROWS, w), lambda i: (i, 0))
    return pl.pallas_call(
        _merge_kernel,
        grid=(N_TOK // MERGE_ROWS,),
        in_specs=[row(D_MODEL), row(D_MODEL)] + [row(BRANCH_W)] * 4
                 + [_layer_spec(p, l) for p in (wg, bg, wbr, wo)],
        out_specs=row(D_MODEL),
        out_shape=jax.ShapeDtypeStruct((N_TOK, D_MODEL), F32),
        compiler_params=_params("parallel"),
        name="merge",
    )(x2, h, oa, ob, oc, od, wg, bg, wbr, wo)


def _ffn_kernel(x_ref, g_ref, wup_ref, wdn_ref, gfin_ref, out_ref, *, final):
    halves = [slice(r * TM, (r + 1) * TM) for r in range(MERGE_ROWS // TM)]
    acc = [x_ref[rows, :] for rows in halves]
    hf = [_rms(a, g_ref[...]).astype(BF16) for a in acc]
    for c in range(D_FF // D_MODEL):
        sl = slice(c * D_MODEL, (c + 1) * D_MODEL)
        for r in range(len(halves)):
            up = jnp.maximum(_dot(hf[r], wup_ref[:, sl]), 0.0)
            acc[r] = acc[r] + _dot((up * up).astype(BF16), wdn_ref[sl, :])
    for r, rows in enumerate(halves):
        out_ref[rows, :] = _rms(acc[r], gfin_ref[...]) if final else acc[r]


def _ffn(l, x2, g_ffn, wup, wdn, g_final, final):
    row = pl.BlockSpec((MERGE_ROWS, D_MODEL), lambda i: (i, 0))
    return pl.pallas_call(
        functools.partial(_ffn_kernel, final=final),
        grid=(N_TOK // MERGE_ROWS,),
        in_specs=[row] + [_layer_spec(p, l) for p in (g_ffn, wup, wdn)] + [_const_spec((1, D_MODEL))],
        out_specs=row,
        out_shape=jax.ShapeDtypeStruct((N_TOK, D_MODEL), F32),
        compiler_params=_params("parallel"),
        name="ffn",
    )(x2, g_ffn, wup, wdn, g_final)


def kernel(x, positions, g_mix, w_in, w_gate, b_gate, rel_bias, g_q_lat, w_uq, g_kv_lat, w_ukv, b_forget,
           w_dw, b_dw, g_conv_ln, b_conv_ln, w_branch, w_o, g_ffn, w_up, w_down, g_final):
    x2 = x.reshape(N_TOK, D_MODEL)
    cos_t, sin_t = _rope_tables(positions)
    consts = _inproj_consts()
    w, wuq, wukv, bf = _inproj_weights(w_in, w_uq, w_ukv, b_forget)
    tbl = _band_table(rel_bias)
    w_taps = jnp.pad(w_dw, ((0, 0), (0, 1), (0, 0)))
    wg, wbr, wo, wup, wdn = (p.astype(BF16) for p in (w_gate, w_branch, w_o, w_up, w_down))
    conv_params = (w_taps, _row(b_dw), _row(g_conv_ln), _row(b_conv_ln))
    for l in range(DEPTH):
        (h, qa, ka, va, qb, kb, vb, qc, kc, vc, od) = _inproj(
            l, x2, _row(g_mix), w, wuq, wukv, bf, _row(g_q_lat), _row(g_kv_lat), cos_t, sin_t, conv_params, consts)
        oa = _band_attention(l, qa, ka, va, tbl)
        ob = _flash_attention(qb, kb, vb, CHUNK, "flash_b")
        oc = _flash_attention(qc, kc, vc, 1, "flash_c")
        x2 = _merge(l, x2, h, oa, ob, oc, od, wg, _row(b_gate), wbr, wo)
        x2 = _ffn(l, x2, _row(g_ffn), wup, wdn, g_final.reshape(1, -1), final=(l == DEPTH - 1))
    return x2.reshape(BATCH, SEQ, D_MODEL)
```

```python
import functools

import numpy as np
import jax
import jax.numpy as jnp
from jax import lax
from jax.experimental import pallas as pl
from jax.experimental.pallas import tpu as pltpu

D_MODEL = 1024
BATCH = 4
SEQ = 4096
DEPTH = 2
N_TOK = BATCH * SEQ
CHUNK = 64
HEAD_DIM = 64
N_HEADS = 4
NORM_EPS = 1e-6
A_LEFT_CHUNKS = 8
A_MAX_REL = 128
B_Q_LORA = 256
B_KV_LORA = 128
B_NOPE = 64
B_ROPE = 32
B_V = 64
ROPE_THETA = 10000.0
CONV_CH = 256
CONV_K = 31
BRANCH_W = 256
D_FF = 4 * D_MODEL

LANES = 128
SUBLANES = 8
HEAD_PAD = LANES
QKV_PAD = N_HEADS * HEAD_PAD
V_ROWS = 80
VT_ROWS = N_HEADS * V_ROWS
NEG = -1e30
LOG2E = 1.4426950408889634

TM = 512
TILES_PER_SEQ = SEQ // TM
T_ATT = 512
TQ_A = 256
CONV_HALO = 32
CONV_ROWS = 64
FLASH_KEY_SPLIT = 2
MERGE_ROWS = 2 * TM
VMEM_LIMIT = 56 * 1024 * 1024

F32 = jnp.float32
BF16 = jnp.bfloat16


def _dot(a, b):
    return jnp.dot(a, b, preferred_element_type=F32)


def _dot_nt(a, b):
    return lax.dot_general(a, b, (((1,), (1,)), ((), ())), preferred_element_type=F32)


def _rms(x, g):
    return x * lax.rsqrt(jnp.mean(x * x, axis=-1, keepdims=True) + NORM_EPS) * g


def _sigmoid(x):
    return 1.0 / (1.0 + jnp.exp(-x))


def _split3(x):
    hi = x.astype(BF16).astype(F32)
    r1 = x - hi
    mid = r1.astype(BF16).astype(F32)
    return hi, mid, r1 - mid


def _emit_pipeline(n, lags, stage_a, stage_b, stage_c):
    for tick in range(n + lags[1]):
        if tick < n:
            stage_a(tick)
        if 0 <= tick - lags[0] < n:
            stage_b(tick - lags[0])
        if 0 <= tick - lags[1] < n:
            stage_c(tick - lags[1])


def _const_spec(shape):
    return pl.BlockSpec(shape, lambda *_: (0,) * len(shape))


def _layer_spec(arr, l):
    nd = arr.ndim - 1
    return pl.BlockSpec((None,) + tuple(arr.shape[1:]), lambda *_: (l,) + (0,) * nd,
                        pipeline_mode=pl.Buffered(1))


def _row(p):
    return p.reshape(p.shape[0], 1, p.shape[1])


def _params(*sem):
    return pltpu.CompilerParams(dimension_semantics=sem, vmem_limit_bytes=VMEM_LIMIT)


def _rope_kernel(pos_ref, invf_ref, cos_ref, sin_ref):
    half = B_ROPE // 2
    ang = invf_ref[...] * pos_ref[...].astype(F32)
    for ref, table, rest in ((cos_ref, jnp.cos(ang), 1.0), (sin_ref, jnp.sin(ang), 0.0)):
        ref[0:B_NOPE, :] = jnp.full((B_NOPE, ang.shape[1]), rest, F32)
        ref[B_NOPE:B_NOPE + half, :] = table
        ref[B_NOPE + half:B_NOPE + B_ROPE, :] = table
        ref[B_NOPE + B_ROPE:HEAD_PAD, :] = jnp.full((HEAD_PAD - B_NOPE - B_ROPE, ang.shape[1]), rest, F32)


def _rope_tables(positions):
    half = B_ROPE // 2
    inv_freq = 1.0 / (ROPE_THETA ** (jnp.arange(half, dtype=F32) / half))
    tm = 2048
    return pl.pallas_call(
        _rope_kernel,
        grid=(N_TOK // tm,),
        in_specs=[pl.BlockSpec((1, tm), lambda i: (0, i)), _const_spec((half, 1))],
        out_specs=[pl.BlockSpec((HEAD_PAD, tm), lambda i: (0, i))] * 2,
        out_shape=[jax.ShapeDtypeStruct((HEAD_PAD, N_TOK), F32)] * 2,
        compiler_params=_params("parallel"),
        name="rope_tables",
    )(positions.reshape(1, N_TOK), inv_freq.reshape(half, 1))


COL_A, COL_B, COL_C, COL_D, COL_END = 0, 768, 1536, 2304, 2816
F_COPIES = 9
AUG_PER_HEAD = 2 * F_COPIES


def _pick3(sel, parts):
    return jnp.where(sel == 0, parts[0], jnp.where(sel == 1, parts[1], parts[2]))


def _inproj_kernel(x_ref, gmix_ref, w_ref, cos_ref, sin_ref, gq_ref, wuq_ref, gkv_ref, wukv_ref, bf_ref,
                   wdw_ref, bdw_ref, gln_ref, bln_ref, sel_ref, tri_ref, e_ref, oneaug_ref, onesblk_ref,
                   h_ref, qa_ref, ka_ref, va_ref, qb_ref, kb_ref, vb_ref,
                   qc_ref, kc_ref, vc_ref, od_ref, carry_ref, ext_ref, shift_ref):
    i = pl.program_id(0)

    @pl.when(i % TILES_PER_SEQ == 0)
    def _():
        carry_ref[...] = jnp.zeros_like(carry_ref)
        ext_ref[0:CONV_HALO, :] = jnp.zeros((CONV_HALO, CONV_CH), F32)

    h = _rms(x_ref[...], gmix_ref[...]).astype(BF16)
    h_ref[...] = h

    pd = _dot_nt(h, w_ref[COL_D:COL_END, :])
    ext_ref[CONV_HALO:CONV_HALO + TM, :] = pd[:, 0:CONV_CH] * _sigmoid(pd[:, CONV_CH:2 * CONV_CH])
    od_ref[...] = _conv_ln_swish(ext_ref, shift_ref, wdw_ref, bdw_ref, gln_ref, bln_ref)
    ext_ref[0:CONV_HALO, :] = ext_ref[TM:TM + CONV_HALO, :]

    row = lax.broadcasted_iota(jnp.int32, (HEAD_PAD, TM), 0)
    own_rows = (row < HEAD_DIM, row >= HEAD_DIM)
    ones_blk = onesblk_ref[...]

    def pair_rows(q):
        qt = q.T
        return [jnp.where(own_rows[hd % 2], qt[(hd // 2) * HEAD_PAD:(hd // 2 + 1) * HEAD_PAD], 0.0)
                for hd in range(N_HEADS)]

    def v_rows(v):
        vt = v.T
        blocks = []
        for hd in range(N_HEADS):
            blocks += [vt[hd * HEAD_DIM:(hd + 1) * HEAD_DIM], ones_blk]
        return jnp.concatenate(blocks, axis=0).astype(BF16)

    pa = _dot_nt(h, w_ref[COL_A:COL_B, :])
    ka_ref[...] = pa[:, 256:512].astype(BF16)
    qa_ref[...] = jnp.concatenate(pair_rows(pa[:, 0:256] * LOG2E), axis=0).astype(BF16)
    va_ref[...] = v_rows(pa[:, 512:768])

    pb = _dot_nt(h, w_ref[COL_B:COL_C, :])
    cos = cos_ref[...].T
    sin = sin_ref[...].T
    cos4 = jnp.concatenate([cos] * N_HEADS, axis=-1)
    sin4 = jnp.concatenate([sin] * N_HEADS, axis=-1)
    qn = _rms(pb[:, 0:B_Q_LORA], gq_ref[...]).astype(BF16)
    qq = _dot(qn, wuq_ref[...])
    qb = qq[:, :QKV_PAD] * cos4 + qq[:, QKV_PAD:] * sin4
    qb_ref[...] = (qb * ((B_NOPE + B_ROPE) ** -0.5 * LOG2E)).T.astype(BF16)
    kvn = _rms(pb[:, B_Q_LORA:B_Q_LORA + B_KV_LORA], gkv_ref[...]).astype(BF16)
    krot = pb[:, 384:512] * cos + pb[:, 512:640] * sin
    kv = _dot(kvn, wukv_ref[...])
    kb_ref[...] = (kv[:, :QKV_PAD] + jnp.concatenate([krot] * N_HEADS, axis=-1)).astype(BF16)
    vb_ref[...] = v_rows(kv[:, QKV_PAD:])

    pc = _dot_nt(h, w_ref[COL_C:COL_D, :])
    piece_a = sel_ref[0:1, :]
    piece_b = sel_ref[1:2, :]
    z = pb[:, 640:768] + bf_ref[...]
    logf = jnp.minimum(z, 0.0) - jnp.log(1.0 + jnp.exp(-jnp.abs(z)))
    part = _dot(tri_ref[...], _pick3(piece_a, _split3(logf)).astype(BF16))
    g = part * LOG2E + carry_ref[...]
    last = g[TM - SUBLANES:TM, :]
    total = last + pltpu.roll(last, LANES - 3 * N_HEADS, 1) + pltpu.roll(last, LANES - 6 * N_HEADS, 1)
    carry_ref[...] = jnp.where(sel_ref[2:3, :] == 1, total[SUBLANES - 1:], 0.0)
    aug = _dot(_pick3(piece_b, _split3(g)).astype(BF16), e_ref[...]) + oneaug_ref[...]
    kc_ref[...] = jnp.concatenate([pc[:, 256:512], aug[:, LANES:]], axis=-1).astype(BF16)
    aug_t = aug[:, :LANES].T
    q_blocks = []
    for hd, pair in enumerate(pair_rows(pc[:, 0:256] * LOG2E)):
        mine = jnp.where(row >= hd * AUG_PER_HEAD, jnp.where(row < (hd + 1) * AUG_PER_HEAD, aug_t, 0.0), 0.0)
        q_blocks += [pair, mine]
    qc_ref[...] = jnp.concatenate(q_blocks, axis=0).astype(BF16)
    vc_ref[...] = v_rows(pc[:, 512:768])


def _pad_last(w, before, after):
    return jnp.pad(w, ((0, 0),) * (w.ndim - 1) + ((before, after),))


def _pad_heads(w):
    return _pad_last(w, 0, HEAD_PAD - w.shape[-1]).reshape(w.shape[:-2] + (QKV_PAD,))


def _rot_half_cols(w):
    half = B_ROPE // 2
    return jnp.concatenate([-w[..., half:], w[..., :half]], axis=-1)


def _inproj_consts():
    tri = np.tril(np.ones((TM, TM), np.float32))
    sel = np.full((SUBLANES, LANES), 2, np.int32)
    e = np.zeros((LANES, 2 * LANES), np.float32)
    oneaug = np.zeros((1, 2 * LANES), np.float32)
    for copy in range(F_COPIES):
        for hd in range(N_HEADS):
            lane = copy * N_HEADS + hd
            sel[0, lane], sel[1, lane] = divmod(copy, 3)
            e[lane, hd * AUG_PER_HEAD + copy] = 1.0
            oneaug[0, LANES + hd * AUG_PER_HEAD + copy] = 1.0
            e[lane, LANES + hd * AUG_PER_HEAD + F_COPIES + copy] = -1.0
            oneaug[0, hd * AUG_PER_HEAD + F_COPIES + copy] = 1.0
    sel[2, :] = 0
    sel[2, :3 * N_HEADS] = 1
    ones_blk = np.zeros((V_ROWS - HEAD_DIM, TM), np.float32)
    ones_blk[0] = 1.0
    return (jnp.asarray(sel), jnp.asarray(tri, BF16), jnp.asarray(e, BF16), jnp.asarray(oneaug),
            jnp.asarray(ones_blk))


def _inproj_weights(w_in, w_uq, w_ukv, b_forget):
    wt = jnp.swapaxes(w_in, 1, 2).astype(BF16)
    o = 0
    cols = {}
    for name, width in (("aq", 256), ("ak", 256), ("av", 256), ("bq", B_Q_LORA), ("bkv", B_KV_LORA),
                        ("bkr", B_ROPE), ("cq", 256), ("ck", 256), ("cv", 256), ("cf", N_HEADS),
                        ("dv", CONV_CH), ("dg", CONV_CH)):
        cols[name] = wt[:, o:o + width, :]
        o += width
    scale = HEAD_DIM ** -0.5

    def pad_rows(w, before, after):
        return jnp.pad(w, ((0, 0), (before, after), (0, 0)))

    def rope_block(w):
        return pad_rows(w, B_NOPE, HEAD_PAD - B_NOPE - B_ROPE)

    half = B_ROPE // 2
    bkr_rot = jnp.concatenate([-cols["bkr"][:, half:], cols["bkr"][:, :half]], axis=1)
    cf = pad_rows(jnp.tile(cols["cf"], (1, F_COPIES, 1)), 0, LANES - F_COPIES * N_HEADS)
    w = jnp.concatenate([cols["aq"] * scale, cols["ak"], cols["av"],
                         cols["bq"], cols["bkv"], rope_block(cols["bkr"]), rope_block(bkr_rot), cf,
                         cols["cq"] * scale, cols["ck"], cols["cv"],
                         cols["dv"], cols["dg"]], axis=1)

    uq = w_uq.reshape(DEPTH, B_Q_LORA, N_HEADS, B_NOPE + B_ROPE)
    uq_rot = jnp.concatenate([jnp.zeros_like(uq[..., :B_NOPE]), _rot_half_cols(uq[..., B_NOPE:])], axis=-1)
    wuq = jnp.concatenate([_pad_heads(uq), _pad_heads(uq_rot)], axis=-1).astype(BF16)
    ukv = w_ukv.reshape(DEPTH, B_KV_LORA, N_HEADS, B_NOPE + B_V)
    wukv = jnp.concatenate([_pad_heads(ukv[..., :B_NOPE]),
                            ukv[..., B_NOPE:].reshape(DEPTH, B_KV_LORA, N_HEADS * B_V)], axis=-1).astype(BF16)
    bf = _row(_pad_last(jnp.tile(b_forget, (1, F_COPIES)), 0, LANES - F_COPIES * N_HEADS))
    return w, wuq, wukv, bf


def _inproj(l, x2, g_mix, w, wuq, wukv, bf, g_q_lat, g_kv_lat, cos_t, sin_t, conv_params, consts):
    assert w.shape == (DEPTH, COL_END, D_MODEL)
    row = lambda w: pl.BlockSpec((TM, w), lambda i: (i, 0))
    layer = lambda a: (a, _layer_spec(a, l))
    rope_spec = pl.BlockSpec((HEAD_PAD, TM), lambda i: (0, i))
    ins = [
        (x2, row(D_MODEL)), layer(g_mix), layer(w), (cos_t, rope_spec), (sin_t, rope_spec),
        layer(g_q_lat), layer(wuq), layer(g_kv_lat), layer(wukv), layer(bf),
    ] + [layer(p) for p in conv_params] + [(c, None) for c in consts]
    args = [a for a, _ in ins]
    specs = [s if s is not None else _const_spec(a.shape) for a, s in ins]

    def rows(w, dt):
        return jax.ShapeDtypeStruct((N_TOK, w), dt), row(w)

    def slab(nrows):
        return (jax.ShapeDtypeStruct((N_TOK // TM, nrows, TM), BF16),
                pl.BlockSpec((None, nrows, TM), lambda i: (i, 0, 0)))

    outs = [rows(D_MODEL, BF16), slab(QKV_PAD), rows(256, BF16), slab(VT_ROWS),
            slab(QKV_PAD), rows(QKV_PAD, BF16), slab(VT_ROWS),
            slab(2 * QKV_PAD), rows(3 * HEAD_PAD, BF16), slab(VT_ROWS), rows(CONV_CH, BF16)]
    return pl.pallas_call(
        _inproj_kernel,
        grid=(N_TOK // TM,),
        in_specs=specs,
        out_specs=[s for _, s in outs],
        out_shape=[a for a, _ in outs],
        scratch_shapes=[pltpu.VMEM((1, LANES), F32), pltpu.VMEM((CONV_HALO + TM, CONV_CH), F32),
                        pltpu.VMEM((SUBLANES, CONV_HALO + TM, CONV_CH), F32)],
        compiler_params=_params("arbitrary"),
        name="inproj",
    )(*args)


def _band_kernel(qt_ref, kp_ref, kc_ref, vtp_ref, vtc_ref, tbl_ref, o_ref):
    i = pl.program_id(1)
    w = TQ_A
    k_blocks = [(kp_ref, 0), (kp_ref, 1), (kc_ref, 0), (kc_ref, 1)]
    vt_blocks = [(vtp_ref, 0), (vtp_ref, 1), (vtc_ref, 0), (vtc_ref, 1)]

    m, acc = {}, {}

    def scores(t, j, hd):
        pair = slice((hd // 2) * HEAD_PAD, (hd // 2 + 1) * HEAD_PAD)
        qh = qt_ref[hd * HEAD_PAD:(hd + 1) * HEAD_PAD, t * w:(t + 1) * w]
        ref, half = k_blocks[t + j]
        s = _dot(ref[half * w:(half + 1) * w, pair], qh) + tbl_ref[j, hd]
        if t + j < 2:
            s = jnp.where(i > 0, s, NEG)
        return s

    def softmax(t, j, hd, s):
        smax = s.max(0, keepdims=True)
        if j == 0:
            m[t, hd], alpha = smax, None
        else:
            m_new = jnp.maximum(m[t, hd], smax)
            alpha = jnp.exp2(m[t, hd] - m_new)
            m[t, hd] = m_new
        return alpha, jnp.exp2(s - m[t, hd]).astype(BF16)

    def update(t, j, hd, alpha, p):
        ref, half = vt_blocks[t + j]
        part = _dot(ref[hd * V_ROWS:(hd + 1) * V_ROWS, half * w:(half + 1) * w], p)
        acc[t, hd] = part if alpha is None else alpha * acc[t, hd] + part

    units = [(t, j, hd) for t in range(TM // w) for j in range(3) for hd in range(N_HEADS)]
    s, sm = {}, {}
    _emit_pipeline(len(units), (2, 5),
                   lambda u: s.__setitem__(u, scores(*units[u])),
                   lambda u: sm.__setitem__(u, softmax(*units[u], s.pop(u))),
                   lambda u: update(*units[u], *sm.pop(u)))
    for t in range(TM // w):
        outs = [acc[t, hd][:HEAD_DIM, :] / acc[t, hd][HEAD_DIM:HEAD_DIM + 1, :] for hd in range(N_HEADS)]
        o_ref[t * w:(t + 1) * w, :] = jnp.concatenate(outs, axis=0).T.astype(BF16)


def _band_table(rel_bias):
    w = TQ_A
    key = np.arange(w)[:, None]
    qry = np.arange(w)[None, :]
    diags, valid = [], []
    for j in range(3):
        rel = (2 - j) * w + (np.arange(2 * w) - (w - 1))
        diags.append(rel_bias[..., np.clip(rel, -A_MAX_REL, A_MAX_REL) + A_MAX_REL])
        cd = (2 - j) * (w // CHUNK) + qry // CHUNK - key // CHUNK
        valid.append((cd >= 0) & (cd <= A_LEFT_CHUNKS))
    diags = jnp.stack(diags, axis=1)
    valid = jnp.asarray(np.stack(valid), F32)
    return pl.pallas_call(
        _band_table_kernel,
        grid=(DEPTH, 3),
        in_specs=[pl.BlockSpec((None, None, N_HEADS, 2 * w), lambda l, j: (l, j, 0, 0)),
                  pl.BlockSpec((None, w, w), lambda l, j: (j, 0, 0))],
        out_specs=pl.BlockSpec((None, None, N_HEADS, w, w), lambda l, j: (l, j, 0, 0, 0)),
        out_shape=jax.ShapeDtypeStruct((DEPTH, 3, N_HEADS, w, w), F32),
        compiler_params=_params("parallel", "parallel"),
        name="band_table",
    )(diags, valid)


def _band_table_kernel(diag_ref, valid_ref, o_ref):
    for hd in range(N_HEADS):
        rows = jnp.broadcast_to(diag_ref[hd:hd + 1, :], (TQ_A, 2 * TQ_A))
        toeplitz = pltpu.roll(rows, TQ_A + 1, 1, stride=1, stride_axis=0)[:, :TQ_A]
        o_ref[hd] = jnp.where(valid_ref[...] > 0.0, toeplitz * LOG2E, NEG)


def _band_attention(l, qat, ka, vat, tbl):
    assert TM == 2 * TQ_A
    nt = TILES_PER_SEQ

    def cur(b, i):
        return b * nt + i

    def prev(b, i):
        return b * nt + jnp.maximum(i - 1, 0)

    def slab_spec(which, nrows=QKV_PAD):
        return pl.BlockSpec((None, nrows, TM), lambda b, i: (which(b, i), 0, 0))

    def k_spec(which):
        return pl.BlockSpec((TM, 256), lambda b, i: (which(b, i), 0))

    return pl.pallas_call(
        _band_kernel,
        grid=(BATCH, nt),
        in_specs=[slab_spec(cur), k_spec(prev), k_spec(cur), slab_spec(prev, VT_ROWS), slab_spec(cur, VT_ROWS),
                  _layer_spec(tbl, l)],
        out_specs=pl.BlockSpec((TM, 256), lambda b, i: (b * nt + i, 0)),
        out_shape=jax.ShapeDtypeStruct((N_TOK, 256), BF16),
        compiler_params=_params("parallel", "parallel"),
        name="band_attention",
    )(qat, ka, ka, vat, vat, tbl)


def _flash_kernel(qt_ref, k_ref, vt_ref, o_ref, *, unit, pair_layout, lags):
    i = pl.program_id(1)
    t = T_ATT
    qrows = 2 * HEAD_PAD if pair_layout else HEAD_PAD

    def visible(sub, tk):
        key = lax.broadcasted_iota(jnp.int32, (tk, t), 0) + sub * tk
        last_visible = lax.broadcasted_iota(jnp.int32, (1, t), 1) | (unit - 1)
        return key <= last_visible

    def k_tile(k_ref, off, hd, rows):
        if not pair_layout:
            return k_ref[pl.ds(off, rows), hd * HEAD_PAD:(hd + 1) * HEAD_PAD]
        pair = hd // 2
        return jnp.concatenate([k_ref[pl.ds(off, rows), pair * HEAD_PAD:(pair + 1) * HEAD_PAD],
                                k_ref[pl.ds(off, rows), 2 * HEAD_PAD:3 * HEAD_PAD]], axis=1)

    def step(tiles, carry):
        heads = [slice(hd * V_ROWS, (hd + 1) * V_ROWS) for hd in range(N_HEADS)]
        m = [c[0] for c in carry]
        acc = [c[1] for c in carry]

        tk = t // FLASH_KEY_SPLIT

        def scores(j, sub, diag, hd):
            off = pl.multiple_of(j * t, t) + sub * tk
            s = _dot(k_tile(k_ref, off, hd, tk), qt_ref[hd * qrows:(hd + 1) * qrows, :])
            return jnp.where(visible(sub, tk), s, NEG) if diag else s

        def softmax(hd, s):
            m_new = jnp.maximum(m[hd], s.max(0, keepdims=True))
            alpha = jnp.exp2(m[hd] - m_new)
            m[hd] = m_new
            return alpha, jnp.exp2(s - m_new).astype(BF16)

        def update(j, sub, hd, alpha, p):
            acc[hd] = alpha * acc[hd] + _dot(vt_ref[j, heads[hd], sub * tk:(sub + 1) * tk], p)

        units = [(j, sub, diag, hd) for j, diag in tiles for sub in range(FLASH_KEY_SPLIT)
                 for hd in range(N_HEADS)]
        s, sm = {}, {}
        _emit_pipeline(len(units), lags,
                       lambda u: s.__setitem__(u, scores(*units[u])),
                       lambda u: sm.__setitem__(u, softmax(units[u][3], s.pop(u))),
                       lambda u: update(units[u][0], units[u][1], units[u][3], *sm.pop(u)))
        return tuple(zip(m, acc))

    init = tuple((jnp.full((1, t), NEG, F32), jnp.zeros((V_ROWS, t), F32)) for _ in range(N_HEADS))
    carry = lax.fori_loop(0, i // 2, lambda jj, c: step([(2 * jj, False), (2 * jj + 1, False)], c), init)
    def finish(tiles):
        outs = [acc[:HEAD_DIM, :] / acc[HEAD_DIM:HEAD_DIM + 1, :] for _, acc in step(tiles, carry)]
        o_ref[...] = jnp.concatenate(outs, axis=0).T.astype(BF16)

    @pl.when(i % 2 == 1)
    def _():
        finish([(i - 1, False), (i, True)])

    @pl.when(i % 2 == 0)
    def _():
        finish([(i, True)])


def _flash_attention(qt, k, vt, unit, name):
    nt = SEQ // T_ATT
    pair_layout = k.shape[1] == 3 * HEAD_PAD
    return pl.pallas_call(
        functools.partial(_flash_kernel, unit=unit, pair_layout=pair_layout,
                          lags=(2, 5) if pair_layout else (2, 5)),
        grid=(BATCH, nt),
        in_specs=[pl.BlockSpec((None, qt.shape[1], T_ATT), lambda b, i: (b * nt + i, 0, 0)),
                  pl.BlockSpec((SEQ, k.shape[1]), lambda b, i: (b, 0)),
                  pl.BlockSpec((nt, VT_ROWS, T_ATT), lambda b, i: (b, 0, 0))],
        out_specs=pl.BlockSpec((T_ATT, 256), lambda b, i: (b * nt + i, 0)),
        out_shape=jax.ShapeDtypeStruct((N_TOK, 256), BF16),
        compiler_params=_params("parallel", "parallel"),
        name=name,
    )(qt, k, vt)


def _conv_ln_swish(ext_ref, shift_ref, w_ref, b_ref, g_ref, beta_ref):
    first = CONV_HALO - (CONV_K - 1)
    for res in range(SUBLANES):
        span = max((first + kk) // SUBLANES for kk in range(CONV_K) if (first + kk) % SUBLANES == res) * SUBLANES + TM
        shift_ref[res, 0:span, :] = ext_ref[res:res + span, :]
    chunks = []
    for c in range(TM // CONV_ROWS):
        acc = jnp.zeros((CONV_ROWS, CONV_CH), F32) + b_ref[...]
        for kk in range(CONV_K):
            start = (first + kk) // SUBLANES * SUBLANES + c * CONV_ROWS
            acc = acc + shift_ref[(first + kk) % SUBLANES, start:start + CONV_ROWS, :] * w_ref[kk:kk + 1, :]
        mu = jnp.mean(acc, axis=-1, keepdims=True)
        xc = acc - mu
        var = jnp.mean(xc * xc, axis=-1, keepdims=True)
        y = xc * lax.rsqrt(var + NORM_EPS) * g_ref[...] + beta_ref[...]
        chunks.append((y * _sigmoid(y)).astype(BF16))
    return jnp.concatenate(chunks, axis=0)


def _merge_kernel(x_ref, h_ref, oa_ref, ob_ref, oc_ref, od_ref, wg_ref, bg_ref, wbr_ref, wo_ref, out_ref):
    halves = [slice(r * TM, (r + 1) * TM) for r in range(MERGE_ROWS // TM)]
    acc = [jnp.zeros((TM, D_MODEL), F32) for _ in halves]
    for n, o_ref in enumerate((oa_ref, ob_ref, oc_ref, od_ref)):
        sl = slice(n * D_MODEL, (n + 1) * D_MODEL)
        for r, rows in enumerate(halves):
            gate = _sigmoid(_dot(h_ref[rows, :], wg_ref[:, sl]) + bg_ref[:, sl])
            acc[r] = acc[r] + gate * _dot(o_ref[rows, :], wbr_ref[n])
    for r, rows in enumerate(halves):
        out_ref[rows, :] = x_ref[rows, :] + _dot(acc[r].astype(BF16), wo_ref[...])


def _merge(l, x2, h, oa, ob, oc, od, wg, bg, wbr, wo):
    row = lambda w: pl.BlockSpec((MERGE_ROWS, w), lambda i: (i, 0))
    return pl.pallas_call(
        _merge_kernel,
        grid=(N_TOK // MERGE_ROWS,),
        in_specs=[row(D_MODEL), row(D_MODEL)] + [row(BRANCH_W)] * 4
                 + [_layer_spec(p, l) for p in (wg, bg, wbr, wo)],
        out_specs=row(D_MODEL),
        out_shape=jax.ShapeDtypeStruct((N_TOK, D_MODEL), F32),
        compiler_params=_params("parallel"),
        name="merge",
    )(x2, h, oa, ob, oc, od, wg, bg, wbr, wo)


def _ffn_kernel(x_ref, g_ref, wup_ref, wdn_ref, gfin_ref, out_ref, *, final):
    halves = [slice(r * TM, (r + 1) * TM) for r in range(MERGE_ROWS // TM)]
    acc = [x_ref[rows, :] for rows in halves]
    hf = [_rms(a, g_ref[...]).astype(BF16) for a in acc]
    for c in range(D_FF // D_MODEL):
        sl = slice(c * D_MODEL, (c + 1) * D_MODEL)
        for r in range(len(halves)):
            up = jnp.maximum(_dot(hf[r], wup_ref[:, sl]), 0.0)
            acc[r] = acc[r] + _dot((up * up).astype(BF16), wdn_ref[sl, :])
    for r, rows in enumerate(halves):
        out_ref[rows, :] = _rms(acc[r], gfin_ref[...]) if final else acc[r]


def _ffn(l, x2, g_ffn, wup, wdn, g_final, final):
    row = pl.BlockSpec((MERGE_ROWS, D_MODEL), lambda i: (i, 0))
    return pl.pallas_call(
        functools.partial(_ffn_kernel, final=final),
        grid=(N_TOK // MERGE_ROWS,),
        in_specs=[row] + [_layer_spec(p, l) for p in (g_ffn, wup, wdn)] + [_const_spec((1, D_MODEL))],
        out_specs=row,
        out_shape=jax.ShapeDtypeStruct((N_TOK, D_MODEL), F32),
        compiler_params=_params("parallel"),
        name="ffn",
    )(x2, g_ffn, wup, wdn, g_final)


def kernel(x, positions, g_mix, w_in, w_gate, b_gate, rel_bias, g_q_lat, w_uq, g_kv_lat, w_ukv, b_forget,
           w_dw, b_dw, g_conv_ln, b_conv_ln, w_branch, w_o, g_ffn, w_up, w_down, g_final):
    x2 = x.reshape(N_TOK, D_MODEL)
    cos_t, sin_t = _rope_tables(positions)
    consts = _inproj_consts()
    w, wuq, wukv, bf = _inproj_weights(w_in, w_uq, w_ukv, b_forget)
    tbl = _band_table(rel_bias)
    w_taps = jnp.pad(w_dw, ((0, 0), (0, 1), (0, 0)))
    wg, wbr, wo, wup, wdn = (p.astype(BF16) for p in (w_gate, w_branch, w_o, w_up, w_down))
    conv_params = (w_taps, _row(b_dw), _row(g_conv_ln), _row(b_conv_ln))
    for l in range(DEPTH):
        (h, qa, ka, va, qb, kb, vb, qc, kc, vc, od) = _inproj(
            l, x2, _row(g_mix), w, wuq, wukv, bf, _row(g_q_lat), _row(g_kv_lat), cos_t, sin_t, conv_params, consts)
        oa = _band_attention(l, qa, ka, va, tbl)
        ob = _flash_attention(qb, kb, vb, CHUNK, "flash_b")
        oc = _flash_attention(qc, kc, vc, 1, "flash_c")
        x2 = _merge(l, x2, h, oa, ob, oc, od, wg, _row(b_gate), wbr, wo)
        x2 = _ffn(l, x2, _row(g_ffn), wup, wdn, g_final.reshape(1, -1), final=(l == DEPTH - 1))
    return x2.reshape(BATCH, SEQ, D_MODEL)
```

```python
import functools

import numpy as np
import jax
import jax.numpy as jnp
from jax import lax
from jax.experimental import pallas as pl
from jax.experimental.pallas import tpu as pltpu

D_MODEL = 1024
BATCH = 4
SEQ = 4096
DEPTH = 2
N_TOK = BATCH * SEQ
CHUNK = 64
HEAD_DIM = 64
N_HEADS = 4
NORM_EPS = 1e-6
A_LEFT_CHUNKS = 8
A_MAX_REL = 128
B_Q_LORA = 256
B_KV_LORA = 128
B_NOPE = 64
B_ROPE = 32
B_V = 64
ROPE_THETA = 10000.0
CONV_CH = 256
CONV_K = 31
BRANCH_W = 256
D_FF = 4 * D_MODEL

LANES = 128
SUBLANES = 8
HEAD_PAD = LANES
QKV_PAD = N_HEADS * HEAD_PAD
V_ROWS = 80
VT_ROWS = N_HEADS * V_ROWS
NEG = -1e30
LOG2E = 1.4426950408889634

TM = 512
TILES_PER_SEQ = SEQ // TM
T_ATT = 512
TQ_A = 256
CONV_HALO = 32
CONV_ROWS = 64
FLASH_KEY_SPLIT = 2
MERGE_ROWS = 2 * TM
VMEM_LIMIT = 56 * 1024 * 1024

F32 = jnp.float32
BF16 = jnp.bfloat16


def _dot(a, b):
    return jnp.dot(a, b, preferred_element_type=F32)


def _dot_nt(a, b):
    return lax.dot_general(a, b, (((1,), (1,)), ((), ())), preferred_element_type=F32)


def _rms(x, g):
    return x * lax.rsqrt(jnp.mean(x * x, axis=-1, keepdims=True) + NORM_EPS) * g


def _sigmoid(x):
    return 1.0 / (1.0 + jnp.exp(-x))


def _split3(x):
    hi = x.astype(BF16).astype(F32)
    r1 = x - hi
    mid = r1.astype(BF16).astype(F32)
    return hi, mid, r1 - mid


def _emit_pipeline(n, lags, stage_a, stage_b, stage_c):
    for tick in range(n + lags[1]):
        if tick < n:
            stage_a(tick)
        if 0 <= tick - lags[0] < n:
            stage_b(tick - lags[0])
        if 0 <= tick - lags[1] < n:
            stage_c(tick - lags[1])


def _const_spec(shape):
    return pl.BlockSpec(shape, lambda *_: (0,) * len(shape))


def _layer_spec(arr, l):
    nd = arr.ndim - 1
    return pl.BlockSpec((None,) + tuple(arr.shape[1:]), lambda *_: (l,) + (0,) * nd,
                        pipeline_mode=pl.Buffered(1))


def _row(p):
    return p.reshape(p.shape[0], 1, p.shape[1])


def _params(*sem):
    return pltpu.CompilerParams(dimension_semantics=sem, vmem_limit_bytes=VMEM_LIMIT)


def _rope_kernel(pos_ref, invf_ref, cos_ref, sin_ref):
    half = B_ROPE // 2
    ang = invf_ref[...] * pos_ref[...].astype(F32)
    for ref, table, rest in ((cos_ref, jnp.cos(ang), 1.0), (sin_ref, jnp.sin(ang), 0.0)):
        ref[0:B_NOPE, :] = jnp.full((B_NOPE, ang.shape[1]), rest, F32)
        ref[B_NOPE:B_NOPE + half, :] = table
        ref[B_NOPE + half:B_NOPE + B_ROPE, :] = table
        ref[B_NOPE + B_ROPE:HEAD_PAD, :] = jnp.full((HEAD_PAD - B_NOPE - B_ROPE, ang.shape[1]), rest, F32)


def _rope_tables(positions):
    half = B_ROPE // 2
    inv_freq = 1.0 / (ROPE_THETA ** (jnp.arange(half, dtype=F32) / half))
    tm = 2048
    return pl.pallas_call(
        _rope_kernel,
        grid=(N_TOK // tm,),
        in_specs=[pl.BlockSpec((1, tm), lambda i: (0, i)), _const_spec((half, 1))],
        out_specs=[pl.BlockSpec((HEAD_PAD, tm), lambda i: (0, i))] * 2,
        out_shape=[jax.ShapeDtypeStruct((HEAD_PAD, N_TOK), F32)] * 2,
        compiler_params=_params("parallel"),
        name="rope_tables",
    )(positions.reshape(1, N_TOK), inv_freq.reshape(half, 1))


COL_A, COL_B, COL_C, COL_D, COL_END = 0, 768, 1536, 2304, 2816
F_COPIES = 9
AUG_PER_HEAD = 2 * F_COPIES


def _pick3(sel, parts):
    return jnp.where(sel == 0, parts[0], jnp.where(sel == 1, parts[1], parts[2]))


def _inproj_kernel(x_ref, gmix_ref, w_ref, cos_ref, sin_ref, gq_ref, wuq_ref, gkv_ref, wukv_ref, bf_ref,
                   wdw_ref, bdw_ref, gln_ref, bln_ref, sel_ref, tri_ref, e_ref, oneaug_ref, onesblk_ref,
                   h_ref, qa_ref, ka_ref, va_ref, qb_ref, kb_ref, vb_ref,
                   qc_ref, kc_ref, vc_ref, od_ref, carry_ref, ext_ref, shift_ref):
    i = pl.program_id(0)

    @pl.when(i % TILES_PER_SEQ == 0)
    def _():
        carry_ref[...] = jnp.zeros_like(carry_ref)
        ext_ref[0:CONV_HALO, :] = jnp.zeros((CONV_HALO, CONV_CH), F32)

    h = _rms(x_ref[...], gmix_ref[...]).astype(BF16)
    h_ref[...] = h

    pd = _dot_nt(h, w_ref[COL_D:COL_END, :])
    ext_ref[CONV_HALO:CONV_HALO + TM, :] = pd[:, 0:CONV_CH] * _sigmoid(pd[:, CONV_CH:2 * CONV_CH])
    od_ref[...] = _conv_ln_swish(ext_ref, shift_ref, wdw_ref, bdw_ref, gln_ref, bln_ref)
    ext_ref[0:CONV_HALO, :] = ext_ref[TM:TM + CONV_HALO, :]

    row = lax.broadcasted_iota(jnp.int32, (HEAD_PAD, TM), 0)
    own_rows = (row < HEAD_DIM, row >= HEAD_DIM)
    ones_blk = onesblk_ref[...]

    def pair_rows(q):
        qt = q.T
        return [jnp.where(own_rows[hd % 2], qt[(hd // 2) * HEAD_PAD:(hd // 2 + 1) * HEAD_PAD], 0.0)
                for hd in range(N_HEADS)]

    def v_rows(v):
        vt = v.T
        blocks = []
        for hd in range(N_HEADS):
            blocks += [vt[hd * HEAD_DIM:(hd + 1) * HEAD_DIM], ones_blk]
        return jnp.concatenate(blocks, axis=0).astype(BF16)

    pa = _dot_nt(h, w_ref[COL_A:COL_B, :])
    ka_ref[...] = pa[:, 256:512].astype(BF16)
    qa_ref[...] = jnp.concatenate(pair_rows(pa[:, 0:256] * LOG2E), axis=0).astype(BF16)
    va_ref[...] = v_rows(pa[:, 512:768])

    pb = _dot_nt(h, w_ref[COL_B:COL_C, :])
    cos = cos_ref[...].T
    sin = sin_ref[...].T
    qn = _rms(pb[:, 0:B_Q_LORA], gq_ref[...]).astype(BF16)
    qq = _dot(qn, wuq_ref[...])
    kvn = _rms(pb[:, B_Q_LORA:B_Q_LORA + B_KV_LORA], gkv_ref[...]).astype(BF16)
    krot = pb[:, 384:512] * cos + pb[:, 512:640] * sin
    kv = _dot(kvn, wukv_ref[...])
    q_scale = (B_NOPE + B_ROPE) ** -0.5 * LOG2E
    for hd in range(N_HEADS):
        sl = slice(hd * HEAD_PAD, (hd + 1) * HEAD_PAD)
        rot = slice(QKV_PAD + hd * HEAD_PAD, QKV_PAD + (hd + 1) * HEAD_PAD)
        qb_ref[sl, :] = ((qq[:, sl] * cos + qq[:, rot] * sin) * q_scale).T.astype(BF16)
        kb_ref[:, sl] = (kv[:, sl] + krot).astype(BF16)
    vb_ref[...] = v_rows(kv[:, QKV_PAD:])

    pc = _dot_nt(h, w_ref[COL_C:COL_D, :])
    piece_a = sel_ref[0:1, :]
    piece_b = sel_ref[1:2, :]
    z = pb[:, 640:768] + bf_ref[...]
    logf = jnp.minimum(z, 0.0) - jnp.log(1.0 + jnp.exp(-jnp.abs(z)))
    part = _dot(tri_ref[...], _pick3(piece_a, _split3(logf)).astype(BF16))
    g = part * LOG2E + carry_ref[...]
    last = g[TM - SUBLANES:TM, :]
    total = last + pltpu.roll(last, LANES - 3 * N_HEADS, 1) + pltpu.roll(last, LANES - 6 * N_HEADS, 1)
    carry_ref[...] = jnp.where(sel_ref[2:3, :] == 1, total[SUBLANES - 1:], 0.0)
    aug = _dot(_pick3(piece_b, _split3(g)).astype(BF16), e_ref[...]) + oneaug_ref[...]
    kc_ref[...] = jnp.concatenate([pc[:, 256:512], aug[:, LANES:]], axis=-1).astype(BF16)
    aug_t = aug[:, :LANES].T
    q_blocks = []
    for hd, pair in enumerate(pair_rows(pc[:, 0:256] * LOG2E)):
        mine = jnp.where(row >= hd * AUG_PER_HEAD, jnp.where(row < (hd + 1) * AUG_PER_HEAD, aug_t, 0.0), 0.0)
        q_blocks += [pair, mine]
    qc_ref[...] = jnp.concatenate(q_blocks, axis=0).astype(BF16)
    vc_ref[...] = v_rows(pc[:, 512:768])


def _pad_last(w, before, after):
    return jnp.pad(w, ((0, 0),) * (w.ndim - 1) + ((before, after),))


def _pad_heads(w):
    return _pad_last(w, 0, HEAD_PAD - w.shape[-1]).reshape(w.shape[:-2] + (QKV_PAD,))


def _rot_half_cols(w):
    half = B_ROPE // 2
    return jnp.concatenate([-w[..., half:], w[..., :half]], axis=-1)


def _inproj_consts():
    tri = np.tril(np.ones((TM, TM), np.float32))
    sel = np.full((SUBLANES, LANES), 2, np.int32)
    e = np.zeros((LANES, 2 * LANES), np.float32)
    oneaug = np.zeros((1, 2 * LANES), np.float32)
    for copy in range(F_COPIES):
        for hd in range(N_HEADS):
            lane = copy * N_HEADS + hd
            sel[0, lane], sel[1, lane] = divmod(copy, 3)
            e[lane, hd * AUG_PER_HEAD + copy] = 1.0
            oneaug[0, LANES + hd * AUG_PER_HEAD + copy] = 1.0
            e[lane, LANES + hd * AUG_PER_HEAD + F_COPIES + copy] = -1.0
            oneaug[0, hd * AUG_PER_HEAD + F_COPIES + copy] = 1.0
    sel[2, :] = 0
    sel[2, :3 * N_HEADS] = 1
    ones_blk = np.zeros((V_ROWS - HEAD_DIM, TM), np.float32)
    ones_blk[0] = 1.0
    return (jnp.asarray(sel), jnp.asarray(tri, BF16), jnp.asarray(e, BF16), jnp.asarray(oneaug),
            jnp.asarray(ones_blk))


def _inproj_weights(w_in, w_uq, w_ukv, b_forget):
    wt = jnp.swapaxes(w_in, 1, 2).astype(BF16)
    o = 0
    cols = {}
    for name, width in (("aq", 256), ("ak", 256), ("av", 256), ("bq", B_Q_LORA), ("bkv", B_KV_LORA),
                        ("bkr", B_ROPE), ("cq", 256), ("ck", 256), ("cv", 256), ("cf", N_HEADS),
                        ("dv", CONV_CH), ("dg", CONV_CH)):
        cols[name] = wt[:, o:o + width, :]
        o += width
    scale = HEAD_DIM ** -0.5

    def pad_rows(w, before, after):
        return jnp.pad(w, ((0, 0), (before, after), (0, 0)))

    def rope_block(w):
        return pad_rows(w, B_NOPE, HEAD_PAD - B_NOPE - B_ROPE)

    half = B_ROPE // 2
    bkr_rot = jnp.concatenate([-cols["bkr"][:, half:], cols["bkr"][:, :half]], axis=1)
    cf = pad_rows(jnp.tile(cols["cf"], (1, F_COPIES, 1)), 0, LANES - F_COPIES * N_HEADS)
    w = jnp.concatenate([cols["aq"] * scale, cols["ak"], cols["av"],
                         cols["bq"], cols["bkv"], rope_block(cols["bkr"]), rope_block(bkr_rot), cf,
                         cols["cq"] * scale, cols["ck"], cols["cv"],
                         cols["dv"], cols["dg"]], axis=1)

    uq = w_uq.reshape(DEPTH, B_Q_LORA, N_HEADS, B_NOPE + B_ROPE)
    uq_rot = jnp.concatenate([jnp.zeros_like(uq[..., :B_NOPE]), _rot_half_cols(uq[..., B_NOPE:])], axis=-1)
    wuq = jnp.concatenate([_pad_heads(uq), _pad_heads(uq_rot)], axis=-1).astype(BF16)
    ukv = w_ukv.reshape(DEPTH, B_KV_LORA, N_HEADS, B_NOPE + B_V)
    wukv = jnp.concatenate([_pad_heads(ukv[..., :B_NOPE]),
                            ukv[..., B_NOPE:].reshape(DEPTH, B_KV_LORA, N_HEADS * B_V)], axis=-1).astype(BF16)
    bf = _row(_pad_last(jnp.tile(b_forget, (1, F_COPIES)), 0, LANES - F_COPIES * N_HEADS))
    return w, wuq, wukv, bf


def _inproj(l, x2, g_mix, w, wuq, wukv, bf, g_q_lat, g_kv_lat, cos_t, sin_t, conv_params, consts):
    assert w.shape == (DEPTH, COL_END, D_MODEL)
    row = lambda w: pl.BlockSpec((TM, w), lambda i: (i, 0))
    layer = lambda a: (a, _layer_spec(a, l))
    rope_spec = pl.BlockSpec((HEAD_PAD, TM), lambda i: (0, i))
    ins = [
        (x2, row(D_MODEL)), layer(g_mix), layer(w), (cos_t, rope_spec), (sin_t, rope_spec),
        layer(g_q_lat), layer(wuq), layer(g_kv_lat), layer(wukv), layer(bf),
    ] + [layer(p) for p in conv_params] + [(c, None) for c in consts]
    args = [a for a, _ in ins]
    specs = [s if s is not None else _const_spec(a.shape) for a, s in ins]

    def rows(w, dt):
        return jax.ShapeDtypeStruct((N_TOK, w), dt), row(w)

    def slab(nrows):
        return (jax.ShapeDtypeStruct((N_TOK // TM, nrows, TM), BF16),
                pl.BlockSpec((None, nrows, TM), lambda i: (i, 0, 0)))

    outs = [rows(D_MODEL, BF16), slab(QKV_PAD), rows(256, BF16), slab(VT_ROWS),
            slab(QKV_PAD), rows(QKV_PAD, BF16), slab(VT_ROWS),
            slab(2 * QKV_PAD), rows(3 * HEAD_PAD, BF16), slab(VT_ROWS), rows(CONV_CH, BF16)]
    return pl.pallas_call(
        _inproj_kernel,
        grid=(N_TOK // TM,),
        in_specs=specs,
        out_specs=[s for _, s in outs],
        out_shape=[a for a, _ in outs],
        scratch_shapes=[pltpu.VMEM((1, LANES), F32), pltpu.VMEM((CONV_HALO + TM, CONV_CH), F32),
                        pltpu.VMEM((SUBLANES, CONV_HALO + TM, CONV_CH), F32)],
        compiler_params=_params("arbitrary"),
        name="inproj",
    )(*args)


def _band_kernel(qt_ref, kp_ref, kc_ref, vtp_ref, vtc_ref, tbl_ref, o_ref):
    i = pl.program_id(1)
    w = TQ_A
    k_blocks = [(kp_ref, 0), (kp_ref, 1), (kc_ref, 0), (kc_ref, 1)]
    vt_blocks = [(vtp_ref, 0), (vtp_ref, 1), (vtc_ref, 0), (vtc_ref, 1)]

    m, acc = {}, {}

    def scores(t, j, hd):
        pair = slice((hd // 2) * HEAD_PAD, (hd // 2 + 1) * HEAD_PAD)
        qh = qt_ref[hd * HEAD_PAD:(hd + 1) * HEAD_PAD, t * w:(t + 1) * w]
        ref, half = k_blocks[t + j]
        s = _dot(ref[half * w:(half + 1) * w, pair], qh) + tbl_ref[j, hd]
        if t + j < 2:
            s = jnp.where(i > 0, s, NEG)
        return s

    def softmax(t, j, hd, s):
        smax = s.max(0, keepdims=True)
        if j == 0:
            m[t, hd], alpha = smax, None
        else:
            m_new = jnp.maximum(m[t, hd], smax)
            alpha = jnp.exp2(m[t, hd] - m_new)
            m[t, hd] = m_new
        return alpha, jnp.exp2(s - m[t, hd]).astype(BF16)

    def update(t, j, hd, alpha, p):
        ref, half = vt_blocks[t + j]
        part = _dot(ref[hd * V_ROWS:(hd + 1) * V_ROWS, half * w:(half + 1) * w], p)
        acc[t, hd] = part if alpha is None else alpha * acc[t, hd] + part

    units = [(t, j, hd) for t in range(TM // w) for j in range(3) for hd in range(N_HEADS)]
    s, sm = {}, {}
    _emit_pipeline(len(units), (2, 5),
                   lambda u: s.__setitem__(u, scores(*units[u])),
                   lambda u: sm.__setitem__(u, softmax(*units[u], s.pop(u))),
                   lambda u: update(*units[u], *sm.pop(u)))
    for t in range(TM // w):
        outs = [acc[t, hd][:HEAD_DIM, :] / acc[t, hd][HEAD_DIM:HEAD_DIM + 1, :] for hd in range(N_HEADS)]
        o_ref[t * w:(t + 1) * w, :] = jnp.concatenate(outs, axis=0).T.astype(BF16)


def _band_table(rel_bias):
    w = TQ_A
    key = np.arange(w)[:, None]
    qry = np.arange(w)[None, :]
    diags, valid = [], []
    for j in range(3):
        rel = (2 - j) * w + (np.arange(2 * w) - (w - 1))
        diags.append(rel_bias[..., np.clip(rel, -A_MAX_REL, A_MAX_REL) + A_MAX_REL])
        cd = (2 - j) * (w // CHUNK) + qry // CHUNK - key // CHUNK
        valid.append((cd >= 0) & (cd <= A_LEFT_CHUNKS))
    diags = jnp.stack(diags, axis=1)
    valid = jnp.asarray(np.stack(valid), F32)
    return pl.pallas_call(
        _band_table_kernel,
        grid=(DEPTH, 3),
        in_specs=[pl.BlockSpec((None, None, N_HEADS, 2 * w), lambda l, j: (l, j, 0, 0)),
                  pl.BlockSpec((None, w, w), lambda l, j: (j, 0, 0))],
        out_specs=pl.BlockSpec((None, None, N_HEADS, w, w), lambda l, j: (l, j, 0, 0, 0)),
        out_shape=jax.ShapeDtypeStruct((DEPTH, 3, N_HEADS, w, w), F32),
        compiler_params=_params("parallel", "parallel"),
        name="band_table",
    )(diags, valid)


def _band_table_kernel(diag_ref, valid_ref, o_ref):
    for hd in range(N_HEADS):
        rows = jnp.broadcast_to(diag_ref[hd:hd + 1, :], (TQ_A, 2 * TQ_A))
        toeplitz = pltpu.roll(rows, TQ_A + 1, 1, stride=1, stride_axis=0)[:, :TQ_A]
        o_ref[hd] = jnp.where(valid_ref[...] > 0.0, toeplitz * LOG2E, NEG)


def _band_attention(l, qat, ka, vat, tbl):
    assert TM == 2 * TQ_A
    nt = TILES_PER_SEQ

    def cur(b, i):
        return b * nt + i

    def prev(b, i):
        return b * nt + jnp.maximum(i - 1, 0)

    def slab_spec(which, nrows=QKV_PAD):
        return pl.BlockSpec((None, nrows, TM), lambda b, i: (which(b, i), 0, 0))

    def k_spec(which):
        return pl.BlockSpec((TM, 256), lambda b, i: (which(b, i), 0))

    return pl.pallas_call(
        _band_kernel,
        grid=(BATCH, nt),
        in_specs=[slab_spec(cur), k_spec(prev), k_spec(cur), slab_spec(prev, VT_ROWS), slab_spec(cur, VT_ROWS),
                  _layer_spec(tbl, l)],
        out_specs=pl.BlockSpec((TM, 256), lambda b, i: (b * nt + i, 0)),
        out_shape=jax.ShapeDtypeStruct((N_TOK, 256), BF16),
        compiler_params=_params("parallel", "parallel"),
        name="band_attention",
    )(qat, ka, ka, vat, vat, tbl)


def _flash_kernel(qt_ref, k_ref, vt_ref, o_ref, *, unit, pair_layout, lags):
    i = pl.program_id(1)
    t = T_ATT
    qrows = 2 * HEAD_PAD if pair_layout else HEAD_PAD

    def visible(sub, tk):
        key = lax.broadcasted_iota(jnp.int32, (tk, t), 0) + sub * tk
        last_visible = lax.broadcasted_iota(jnp.int32, (1, t), 1) | (unit - 1)
        return key <= last_visible

    def k_tile(k_ref, off, hd, rows):
        if not pair_layout:
            return k_ref[pl.ds(off, rows), hd * HEAD_PAD:(hd + 1) * HEAD_PAD]
        pair = hd // 2
        return jnp.concatenate([k_ref[pl.ds(off, rows), pair * HEAD_PAD:(pair + 1) * HEAD_PAD],
                                k_ref[pl.ds(off, rows), 2 * HEAD_PAD:3 * HEAD_PAD]], axis=1)

    def step(tiles, carry):
        heads = [slice(hd * V_ROWS, (hd + 1) * V_ROWS) for hd in range(N_HEADS)]
        m = [c[0] for c in carry]
        acc = [c[1] for c in carry]

        tk = t // FLASH_KEY_SPLIT

        def scores(j, sub, diag, hd):
            off = pl.multiple_of(j * t, t) + sub * tk
            s = _dot(k_tile(k_ref, off, hd, tk), qt_ref[hd * qrows:(hd + 1) * qrows, :])
            return jnp.where(visible(sub, tk), s, NEG) if diag else s

        def softmax(hd, s):
            m_new = jnp.maximum(m[hd], s.max(0, keepdims=True))
            alpha = jnp.exp2(m[hd] - m_new)
            m[hd] = m_new
            return alpha, jnp.exp2(s - m_new).astype(BF16)

        def update(j, sub, hd, alpha, p):
            acc[hd] = alpha * acc[hd] + _dot(vt_ref[j, heads[hd], sub * tk:(sub + 1) * tk], p)

        units = [(j, sub, diag, hd) for j, diag in tiles for sub in range(FLASH_KEY_SPLIT)
                 for hd in range(N_HEADS)]
        s, sm = {}, {}
        _emit_pipeline(len(units), lags,
                       lambda u: s.__setitem__(u, scores(*units[u])),
                       lambda u: sm.__setitem__(u, softmax(units[u][3], s.pop(u))),
                       lambda u: update(units[u][0], units[u][1], units[u][3], *sm.pop(u)))
        return tuple(zip(m, acc))

    init = tuple((jnp.full((1, t), NEG, F32), jnp.zeros((V_ROWS, t), F32)) for _ in range(N_HEADS))
    carry = lax.fori_loop(0, i // 2, lambda jj, c: step([(2 * jj, False), (2 * jj + 1, False)], c), init)
    def finish(tiles):
        outs = [acc[:HEAD_DIM, :] / acc[HEAD_DIM:HEAD_DIM + 1, :] for _, acc in step(tiles, carry)]
        o_ref[...] = jnp.concatenate(outs, axis=0).T.astype(BF16)

    @pl.when(i % 2 == 1)
    def _():
        finish([(i - 1, False), (i, True)])

    @pl.when(i % 2 == 0)
    def _():
        finish([(i, True)])


def _flash_attention(qt, k, vt, unit, name):
    nt = SEQ // T_ATT
    pair_layout = k.shape[1] == 3 * HEAD_PAD
    return pl.pallas_call(
        functools.partial(_flash_kernel, unit=unit, pair_layout=pair_layout,
                          lags=(2, 5) if pair_layout else (2, 5)),
        grid=(BATCH, nt),
        in_specs=[pl.BlockSpec((None, qt.shape[1], T_ATT), lambda b, i: (b * nt + i, 0, 0)),
                  pl.BlockSpec((SEQ, k.shape[1]), lambda b, i: (b, 0)),
                  pl.BlockSpec((nt, VT_ROWS, T_ATT), lambda b, i: (b, 0, 0))],
        out_specs=pl.BlockSpec((T_ATT, 256), lambda b, i: (b * nt + i, 0)),
        out_shape=jax.ShapeDtypeStruct((N_TOK, 256), BF16),
        compiler_params=_params("parallel", "parallel"),
        name=name,
    )(qt, k, vt)


def _conv_ln_swish(ext_ref, shift_ref, w_ref, b_ref, g_ref, beta_ref):
    first = CONV_HALO - (CONV_K - 1)
    for res in range(SUBLANES):
        span = max((first + kk) // SUBLANES for kk in range(CONV_K) if (first + kk) % SUBLANES == res) * SUBLANES + TM
        shift_ref[res, 0:span, :] = ext_ref[res:res + span, :]
    chunks = []
    for c in range(TM // CONV_ROWS):
        acc = jnp.zeros((CONV_ROWS, CONV_CH), F32) + b_ref[...]
        for kk in range(CONV_K):
            start = (first + kk) // SUBLANES * SUBLANES + c * CONV_ROWS
            acc = acc + shift_ref[(first + kk) % SUBLANES, start:start + CONV_ROWS, :] * w_ref[kk:kk + 1, :]
        mu = jnp.mean(acc, axis=-1, keepdims=True)
        xc = acc - mu
        var = jnp.mean(xc * xc, axis=-1, keepdims=True)
        y = xc * lax.rsqrt(var + NORM_EPS) * g_ref[...] + beta_ref[...]
        chunks.append((y * _sigmoid(y)).astype(BF16))
    return jnp.concatenate(chunks, axis=0)


def _merge_kernel(x_ref, h_ref, oa_ref, ob_ref, oc_ref, od_ref, wg_ref, bg_ref, wbr_ref, wo_ref, out_ref):
    halves = [slice(r * TM, (r + 1) * TM) for r in range(MERGE_ROWS // TM)]
    acc = [jnp.zeros((TM, D_MODEL), F32) for _ in halves]
    for n, o_ref in enumerate((oa_ref, ob_ref, oc_ref, od_ref)):
        sl = slice(n * D_MODEL, (n + 1) * D_MODEL)
        for r, rows in enumerate(halves):
            gate = _sigmoid(_dot(h_ref[rows, :], wg_ref[:, sl]) + bg_ref[:, sl])
            acc[r] = acc[r] + gate * _dot(o_ref[rows, :], wbr_ref[n])
    for r, rows in enumerate(halves):
        out_ref[rows, :] = x_ref[rows, :] + _dot(acc[r].astype(BF16), wo_ref[...])


def _merge(l, x2, h, oa, ob, oc, od, wg, bg, wbr, wo):
    row = lambda w: pl.BlockSpec((MERGE_ROWS, w), lambda i: (i, 0))
    return pl.pallas_call(
        _merge_kernel,
        grid=(N_TOK // MERGE_ROWS,),
        in_specs=[row(D_MODEL), row(D_MODEL)] + [row(BRANCH_W)] * 4
                 + [_layer_spec(p, l) for p in (wg, bg, wbr, wo)],
        out_specs=row(D_MODEL),
        out_shape=jax.ShapeDtypeStruct((N_TOK, D_MODEL), F32),
        compiler_params=_params("parallel"),
        name="merge",
    )(x2, h, oa, ob, oc, od, wg, bg, wbr, wo)


def _ffn_kernel(x_ref, g_ref, wup_ref, wdn_ref, gfin_ref, out_ref, *, final):
    halves = [slice(r * TM, (r + 1) * TM) for r in range(MERGE_ROWS // TM)]
    acc = [x_ref[rows, :] for rows in halves]
    hf = [_rms(a, g_ref[...]).astype(BF16) for a in acc]
    for c in range(D_FF // D_MODEL):
        sl = slice(c * D_MODEL, (c + 1) * D_MODEL)
        for r in range(len(halves)):
            up = jnp.maximum(_dot(hf[r], wup_ref[:, sl]), 0.0)
            acc[r] = acc[r] + _dot((up * up).astype(BF16), wdn_ref[sl, :])
    for r, rows in enumerate(halves):
        out_ref[rows, :] = _rms(acc[r], gfin_ref[...]) if final else acc[r]


def _ffn(l, x2, g_ffn, wup, wdn, g_final, final):
    row = pl.BlockSpec((MERGE_ROWS, D_MODEL), lambda i: (i, 0))
    return pl.pallas_call(
        functools.partial(_ffn_kernel, final=final),
        grid=(N_TOK // MERGE_ROWS,),
        in_specs=[row] + [_layer_spec(p, l) for p in (g_ffn, wup, wdn)] + [_const_spec((1, D_MODEL))],
        out_specs=row,
        out_shape=jax.ShapeDtypeStruct((N_TOK, D_MODEL), F32),
        compiler_params=_params("parallel"),
        name="ffn",
    )(x2, g_ffn, wup, wdn, g_final)


def kernel(x, positions, g_mix, w_in, w_gate, b_gate, rel_bias, g_q_lat, w_uq, g_kv_lat, w_ukv, b_forget,
           w_dw, b_dw, g_conv_ln, b_conv_ln, w_branch, w_o, g_ffn, w_up, w_down, g_final):
    x2 = x.reshape(N_TOK, D_MODEL)
    cos_t, sin_t = _rope_tables(positions)
    consts = _inproj_consts()
    w, wuq, wukv, bf = _inproj_weights(w_in, w_uq, w_ukv, b_forget)
    tbl = _band_table(rel_bias)
    w_taps = jnp.pad(w_dw, ((0, 0), (0, 1), (0, 0)))
    wg, wbr, wo, wup, wdn = (p.astype(BF16) for p in (w_gate, w_branch, w_o, w_up, w_down))
    conv_params = (w_taps, _row(b_dw), _row(g_conv_ln), _row(b_conv_ln))
    for l in range(DEPTH):
        (h, qa, ka, va, qb, kb, vb, qc, kc, vc, od) = _inproj(
            l, x2, _row(g_mix), w, wuq, wukv, bf, _row(g_q_lat), _row(g_kv_lat), cos_t, sin_t, conv_params, consts)
        oa = _band_attention(l, qa, ka, va, tbl)
        ob = _flash_attention(qb, kb, vb, CHUNK, "flash_b")
        oc = _flash_attention(qc, kc, vc, 1, "flash_c")
        x2 = _merge(l, x2, h, oa, ob, oc, od, wg, _row(b_gate), wbr, wo)
        x2 = _ffn(l, x2, _row(g_ffn), wup, wdn, g_final.reshape(1, -1), final=(l == DEPTH - 1))
    return x2.reshape(BATCH, SEQ, D_MODEL)
```
